```python
import math
import jax, jax.numpy as jnp
from jax import lax
import numpy as np

D_MODEL = 1024
BATCH = 16
SEQ = 2048
DEPTH = 2

HEAD_DIM = 64
DIL_GROUPS = ((128, 1), (512, 4), (2048, 16))
DIL_HEADS_PER_GROUP = 4
DIL_HEADS = DIL_HEADS_PER_GROUP * len(DIL_GROUPS)
DIL_WIDTH = DIL_HEADS * HEAD_DIM
DIL_OUT = DIL_HEADS_PER_GROUP * HEAD_DIM
HYENA_WIDTH = 512
HYENA_ORDER = 2
HYENA_BANDS = 16
HYENA_EMB = 2 * HYENA_BANDS + 1
HYENA_HIDDEN = 64
HYENA_DECAY_MIN = -math.log(1e-2) / 1.5
HYENA_DECAY_MAX = -math.log(1e-2) / 0.3
SHORT_CONV = 3
HY_IN = (HYENA_ORDER + 1) * HYENA_WIDTH
SWA_Q_HEADS = 8
SWA_KV_HEADS = 2
SWA_RADIUS = 128
SWA_BLOCK = 128
SWA_WIDTH = SWA_Q_HEADS * HEAD_DIM
SWA_IN = (SWA_Q_HEADS + 2 * SWA_KV_HEADS) * HEAD_DIM
N_BRANCHES = 3
GATE_IN = N_BRANCHES * D_MODEL
IN_SPLITS = (3 * DIL_WIDTH, 3 * DIL_WIDTH + HY_IN, 3 * DIL_WIDTH + HY_IN + SWA_IN)
IN_WIDTH = 3 * DIL_WIDTH + HY_IN + SWA_IN + GATE_IN
ROPE_THETA = 10000.0
PEER_HEADS = 8
PEER_KEYS = 128
PEER_EXPERTS = PEER_KEYS * PEER_KEYS
PEER_QUERY = 256
PEER_TOPK = 16
PEER_CHUNK = 128
ALPHA = (2 * DEPTH) ** 0.25
BETA = (8 * DEPTH) ** -0.25
LN_EPS = 1e-5
NEG_INF = -1e30

kernel_name = "hybrid_dilated_hyena_swa_peer_encoder"


def layer_norm(x, g, b):
    xf = x.astype(jnp.float32)
    mu = xf.mean(-1, keepdims=True)
    var = jnp.square(xf - mu).mean(-1, keepdims=True)
    return ((xf - mu) * lax.rsqrt(var + LN_EPS) * g.astype(jnp.float32) + b.astype(jnp.float32)).astype(x.dtype)


def rope_tables(seq):
    pos = jnp.arange(seq, dtype=jnp.float32)
    inv = ROPE_THETA ** (-jnp.arange(0, HEAD_DIM, 2, dtype=jnp.float32) / HEAD_DIM)
    ang = pos[:, None] * inv[None, :]
    return jnp.cos(ang), jnp.sin(ang)


def apply_rope(t, cos, sin):
    t1, t2 = jnp.split(t.astype(jnp.float32), 2, axis=-1)
    return jnp.concatenate([t1 * cos - t2 * sin, t2 * cos + t1 * sin], axis=-1).astype(t.dtype)


def to_heads(t, n):
    b, s, _ = t.shape
    return t.reshape(b, s, n, HEAD_DIM).transpose(0, 2, 1, 3)


def banded_attention(q, k, v, radius, block, sink=None):
    b, hk, g, L, hd = q.shape
    nb = -(-L // block)
    lp = nb * block
    q = jnp.pad(q, ((0, 0), (0, 0), (0, 0), (0, lp - L), (0, 0)))
    kv_pad = ((0, 0), (0, 0), (block, block + lp - L), (0, 0))
    kb = jnp.pad(k, kv_pad).reshape(b, hk, nb + 2, block, hd)
    vb = jnp.pad(v, kv_pad).reshape(b, hk, nb + 2, block, hd)
    kw = jnp.concatenate([kb[:, :, :-2], kb[:, :, 1:-1], kb[:, :, 2:]], axis=3)
    vw = jnp.concatenate([vb[:, :, :-2], vb[:, :, 1:-1], vb[:, :, 2:]], axis=3)
    qb = q.reshape(b, hk, g, nb, block, hd)
    qpos = jnp.arange(lp).reshape(nb, block)
    kpos = jnp.arange(nb)[:, None] * block + jnp.arange(-block, 2 * block)[None, :]
    mask = ((jnp.abs(qpos[:, :, None] - kpos[:, None, :]) <= radius)
            & (kpos >= 0)[:, None, :] & (kpos < L)[:, None, :])
    s = jnp.einsum('bhgnqd,bhnkd->bhgnqk', qb, kw).astype(jnp.float32) * (hd ** -0.5)
    s = jnp.where(mask, s, NEG_INF)
    m = s.max(-1, keepdims=True)
    if sink is not None:
        sk = sink.astype(jnp.float32)[None, :, :, None, None, None]
        m = jnp.maximum(m, sk)
    p = jnp.exp(s - m)
    denom = p.sum(-1, keepdims=True)
    if sink is not None:
        denom = denom + jnp.exp(sk - m)
    out = jnp.einsum('bhgnqk,bhnkd->bhgnqd', (p / denom).astype(v.dtype), vw)
    lse = (m + jnp.log(denom))[..., 0]
    return out.reshape(b, hk, g, lp, hd)[:, :, :, :L], lse.reshape(b, hk, g, lp)[..., :L]


def dilated_attention(q, k, v, dilation, radius):
    b, h, s, hd = q.shape
    ls = s // dilation

    def split(t):
        return t.reshape(b, h, ls, dilation, hd).transpose(0, 1, 3, 2, 4).reshape(b, h * dilation, ls, hd)

    out, lse = banded_attention(split(q)[:, :, None], split(k), split(v), radius, radius)
    out = out[:, :, 0].reshape(b, h, dilation, ls, hd).transpose(0, 1, 3, 2, 4).reshape(b, h, s, hd)
    lse = lse[:, :, 0].reshape(b, h, dilation, ls).transpose(0, 1, 3, 2).reshape(b, h, s)
    return out, lse


def short_conv(t, w, bias):
    half = SHORT_CONV // 2
    s = t.shape[1]
    tp = jnp.pad(t, ((0, 0), (half, half), (0, 0)))
    out = bias
    for i in range(SHORT_CONV):
        out = out + tp[:, i:i + s] * w[i]
    return out


def hyena_filter_spectrum(L, w1, b1, w2, b2, w3, freq, log_decay):
    f32 = jnp.float32
    idx = jnp.arange(L, dtype=f32)
    t = idx / max(L - 1, 1)
    w = 2.0 * math.pi * idx / L
    bands = jnp.linspace(1e-4, HYENA_BANDS - 1, HYENA_BANDS, dtype=f32)
    ang = w[:, None] * bands[None, :]
    z = jnp.concatenate([t[:, None], jnp.cos(ang), -jnp.sin(ang)], axis=-1)
    freq = freq.astype(f32)
    h = jnp.sin(freq[0] * (z @ w1.astype(f32) + b1.astype(f32)))
    h = jnp.sin(freq[1] * (h @ w2.astype(f32) + b2.astype(f32)))
    h = (h @ w3.astype(f32)) * jnp.exp(-t[:, None] * jnp.exp(log_decay.astype(f32))[None, :])
    h = h.reshape(L, 2, HYENA_ORDER, HYENA_WIDTH)
    filt = jnp.concatenate([h[:, 0], jnp.zeros((1, HYENA_ORDER, HYENA_WIDTH), f32), h[1:, 1][::-1]], axis=0)
    filt = filt * lax.rsqrt(jnp.sum(jnp.square(filt), axis=0, keepdims=True) + 1e-12)
    return jnp.fft.rfft(filt, axis=0)


def fft_conv(z, filt_f):
    L = z.shape[1]
    zf = jnp.fft.rfft(z.astype(jnp.float32), n=2 * L, axis=1)
    return jnp.fft.irfft(zf * filt_f[None], n=2 * L, axis=1)[:, :L].astype(z.dtype)


def hybrid_mixer(u, cos, sin, w_in, conv_w, conv_b, hy_w1, hy_b1, hy_w2, hy_b2, hy_w3, hy_freq,
                 hy_log_decay, hy_bias, attn_sink, w_branch_a, w_branch_b, w_branch_c, w_out):
    b, s, _ = u.shape
    proj = u @ w_in
    qkv_a, hy_in, qkv_c, gate_logits = jnp.split(proj, IN_SPLITS, axis=-1)

    qa, ka, va = [to_heads(t, DIL_HEADS) for t in jnp.split(qkv_a, 3, axis=-1)]
    qa, ka = apply_rope(qa, cos, sin), apply_rope(ka, cos, sin)
    outs, lses = [], []
    for gi, (window, dil) in enumerate(DIL_GROUPS):
        hs = slice(gi * DIL_HEADS_PER_GROUP, (gi + 1) * DIL_HEADS_PER_GROUP)
        o, lse = dilated_attention(qa[:, hs], ka[:, hs], va[:, hs], dil, window // (2 * dil))
        outs.append(o)
        lses.append(lse)
    wts = jax.nn.softmax(jnp.stack(lses, axis=0), axis=0)
    ya = jnp.einsum('gbhs,gbhsd->bshd', wts.astype(u.dtype), jnp.stack(outs, axis=0)).reshape(b, s, DIL_OUT)

    hy = short_conv(hy_in, conv_w, conv_b)
    hy_parts = jnp.split(hy, HYENA_ORDER + 1, axis=-1)
    filt_f = hyena_filter_spectrum(s, hy_w1, hy_b1, hy_w2, hy_b2, hy_w3, hy_freq, hy_log_decay)
    z = hy_parts[0]
    for o in range(HYENA_ORDER):
        z = hy_parts[o + 1] * (fft_conv(z, filt_f[:, o]) + hy_bias[o] * z)
    yb = z

    qc, kc, vc = jnp.split(qkv_c, (SWA_WIDTH, SWA_WIDTH + SWA_KV_HEADS * HEAD_DIM), axis=-1)
    qc = apply_rope(to_heads(qc, SWA_Q_HEADS), cos, sin)
    kc = apply_rope(to_heads(kc, SWA_KV_HEADS), cos, sin)
    vc = to_heads(vc, SWA_KV_HEADS)
    grp = SWA_Q_HEADS // SWA_KV_HEADS
    qc = qc.reshape(b, SWA_KV_HEADS, grp, s, HEAD_DIM)
    oc, _ = banded_attention(qc, kc, vc, SWA_RADIUS, SWA_BLOCK, attn_sink.reshape(SWA_KV_HEADS, grp))
    yc = oc.reshape(b, SWA_Q_HEADS, s, HEAD_DIM).transpose(0, 2, 1, 3).reshape(b, s, SWA_WIDTH)

    ga, gb, gc = jnp.split(jax.nn.sigmoid(gate_logits.astype(jnp.float32)).astype(u.dtype), N_BRANCHES, axis=-1)
    merged = ga * (ya @ w_branch_a) + gb * (yb @ w_branch_b) + gc * (yc @ w_branch_c)
    return merged @ w_out


def peer_ffn(x, w_query, sub_keys, expert_u, expert_v):
    b, s, d = x.shape
    xt = x.reshape(-1, PEER_CHUNK, d)

    def chunk(xc):
        q = (xc @ w_query).reshape(PEER_CHUNK, PEER_HEADS, 2, PEER_QUERY // 2)
        sc = jnp.einsum('thpc,hpnc->thpn', q, sub_keys).astype(jnp.float32)
        s1, i1 = lax.top_k(sc[:, :, 0], PEER_TOPK)
        s2, i2 = lax.top_k(sc[:, :, 1], PEER_TOPK)
        cand = (s1[..., :, None] + s2[..., None, :]).reshape(PEER_CHUNK, PEER_HEADS, PEER_TOPK * PEER_TOPK)
        best, ci = lax.top_k(cand, PEER_TOPK)
        e = (jnp.take_along_axis(i1, ci // PEER_TOPK, axis=-1) * PEER_KEYS
             + jnp.take_along_axis(i2, ci % PEER_TOPK, axis=-1))
        gate = jax.nn.softmax(best, axis=-1)
        act = jax.nn.gelu(jnp.einsum('thkd,td->thk', expert_u[e], xc), approximate=False)
        wgt = (gate * act.astype(jnp.float32)).astype(xc.dtype)
        return jnp.einsum('thk,thkd->td', wgt, expert_v[e])

    return lax.map(chunk, xt).reshape(b, s, d)


def setup_inputs(seed: int = 0) -> dict:
    key = jax.random.key(seed)
    ks = jax.random.split(key, 32)
    f32 = jnp.float32

    def nrm(k, shape, scale):
        return jax.random.normal(k, shape, f32) * scale

    hy_out = 2 * HYENA_ORDER * HYENA_WIDTH
    return {
        "x": nrm(ks[0], (BATCH, SEQ, D_MODEL), 1.0),
        "c": nrm(ks[1], (BATCH, D_MODEL), 1.0),
        "w_ada": nrm(ks[2], (DEPTH, D_MODEL, 6 * D_MODEL), 0.5 * D_MODEL ** -0.5),
        "b_ada": nrm(ks[3], (DEPTH, 6 * D_MODEL), 0.02),
        "w_in": nrm(ks[4], (DEPTH, D_MODEL, IN_WIDTH), D_MODEL ** -0.5),
        "conv_w": nrm(ks[5], (DEPTH, SHORT_CONV, HY_IN), SHORT_CONV ** -0.5),
        "conv_b": nrm(ks[6], (DEPTH, HY_IN), 0.02),
        "hy_w1": nrm(ks[7], (DEPTH, HYENA_EMB, HYENA_HIDDEN), HYENA_EMB ** -0.5),
        "hy_b1": nrm(ks[8], (DEPTH, HYENA_HIDDEN), 0.02),
        "hy_w2": nrm(ks[9], (DEPTH, HYENA_HIDDEN, HYENA_HIDDEN), HYENA_HIDDEN ** -0.5),
        "hy_b2": nrm(ks[10], (DEPTH, HYENA_HIDDEN), 0.02),
        "hy_w3": nrm(ks[11], (DEPTH, HYENA_HIDDEN, hy_out), HYENA_HIDDEN ** -0.5),
        "hy_freq": 1.0 + nrm(ks[12], (DEPTH, 2, HYENA_HIDDEN), 0.02),
        "hy_log_decay": jax.random.uniform(ks[13], (DEPTH, hy_out), f32,
                                           math.log(HYENA_DECAY_MIN), math.log(HYENA_DECAY_MAX)),
        "hy_bias": nrm(ks[14], (DEPTH, HYENA_ORDER, HYENA_WIDTH), 0.5),
        "attn_sink": nrm(ks[15], (DEPTH, SWA_Q_HEADS), 0.5),
        "w_branch_a": nrm(ks[16], (DEPTH, DIL_OUT, D_MODEL), BETA * DIL_OUT ** -0.5),
        "w_branch_b": nrm(ks[17], (DEPTH, HYENA_WIDTH, D_MODEL), BETA * HYENA_WIDTH ** -0.5),
        "w_branch_c": nrm(ks[18], (DEPTH, SWA_WIDTH, D_MODEL), BETA * SWA_WIDTH ** -0.5),
        "w_out": nrm(ks[19], (DEPTH, D_MODEL, D_MODEL), BETA * D_MODEL ** -0.5),
        "ln_g": 1.0 + nrm(ks[20], (DEPTH, 2, D_MODEL), 0.02),
        "ln_b": nrm(ks[21], (DEPTH, 2, D_MODEL), 0.02),
        "peer_wq": nrm(ks[22], (DEPTH, D_MODEL, PEER_HEADS * PEER_QUERY), D_MODEL ** -0.5),
        "peer_keys": nrm(ks[23], (DEPTH, PEER_HEADS, 2, PEER_KEYS, PEER_QUERY // 2), (PEER_QUERY // 2) ** -0.5),
        "peer_u": nrm(ks[24], (DEPTH, PEER_EXPERTS, D_MODEL), D_MODEL ** -0.5),
        "peer_v": nrm(ks[25], (DEPTH, PEER_EXPERTS, D_MODEL), BETA),
    }


def reference(x, c, w_ada, b_ada, w_in, conv_w, conv_b, hy_w1, hy_b1, hy_w2, hy_b2, hy_w3, hy_freq,
              hy_log_decay, hy_bias, attn_sink, w_branch_a, w_branch_b, w_branch_c, w_out, ln_g, ln_b,
              peer_wq, peer_keys, peer_u, peer_v):
    s = x.shape[1]
    cos, sin = rope_tables(s)
    cond = jax.nn.silu(c)
    for l in range(DEPTH):
        ada = cond @ w_ada[l] + b_ada[l]
        sh1, sc1, g1, sh2, sc2, g2 = [a[:, None, :] for a in jnp.split(ada, 6, axis=-1)]
        u = x * (1.0 + sc1) + sh1
        mix = hybrid_mixer(u, cos, sin, w_in[l], conv_w[l], conv_b[l], hy_w1[l], hy_b1[l], hy_w2[l],
                           hy_b2[l], hy_w3[l], hy_freq[l], hy_log_decay[l], hy_bias[l], attn_sink[l],
                           w_branch_a[l], w_branch_b[l], w_branch_c[l], w_out[l])
        x = layer_norm(ALPHA * x + g1 * mix, ln_g[l, 0], ln_b[l, 0])
        u = x * (1.0 + sc2) + sh2
        ffn = peer_ffn(u, peer_wq[l], peer_keys[l], peer_u[l], peer_v[l])
        x = layer_norm(ALPHA * x + g2 * ffn, ln_g[l, 1], ln_b[l, 1])
    return x
```

```python
import functools
import math

import jax
import jax.numpy as jnp
from jax import lax
from jax.experimental import pallas as pl
from jax.experimental.pallas import tpu as pltpu

F32 = jnp.float32
BF16 = jnp.bfloat16

D_MODEL = 1024
HEAD_DIM = 64
DIL_GROUPS = ((128, 1), (512, 4), (2048, 16))
DIL_RADIUS = 64
HYENA_WIDTH = 512
HYENA_BANDS = 16
SWA_RADIUS = 128
SWA_Q_HEADS = 8
SWA_KV_HEADS = 2
PEER_HEADS = 8
PEER_KEYS = 128
PEER_TOPK = 16
DEPTH = 2
ALPHA = (2 * DEPTH) ** 0.25
LN_EPS = 1e-5
NEG_INF = -1e30
ROPE_THETA = 10000.0

COL_GATES = 0
COL_HY = 3072
COL_DIL = 4608
COL_SWA = 6912
IN_WIDTH = 7680

VMEM_LIMIT = 56 * 1024 * 1024


def _cparams(sem, vmem=VMEM_LIMIT):
    return pltpu.CompilerParams(dimension_semantics=sem, vmem_limit_bytes=vmem)


def _ada_kernel(c_ref, w_ref, b_ref, o_ref):
    c = c_ref[...]
    cond = (c * jax.nn.sigmoid(c)).astype(BF16)
    o_ref[...] = jnp.dot(cond, w_ref[...].astype(BF16), preferred_element_type=F32) + b_ref[...]


def ada_ln(c, w_ada, b_ada):
    depth, d, n = w_ada.shape
    b = c.shape[0]
    tn = 1536
    return pl.pallas_call(
        _ada_kernel,
        out_shape=jax.ShapeDtypeStruct((depth, b, n), F32),
        grid=(depth, n // tn),
        in_specs=[
            pl.BlockSpec((b, d), lambda l, j: (0, 0)),
            pl.BlockSpec((None, d, tn), lambda l, j: (l, 0, j)),
            pl.BlockSpec((None, 1, tn), lambda l, j: (l, 0, j)),
        ],
        out_specs=pl.BlockSpec((None, b, tn), lambda l, j: (l, 0, j)),
        compiler_params=_cparams(("arbitrary", "arbitrary")),
        name="ada_ln",
    )(c, w_ada, b_ada.reshape(depth, 1, n))


def _inproj_kernel(x_ref, sc_ref, sh_ref, w_ref, o_ref, u_ref):
    @pl.when(pl.program_id(2) == 0)
    def _():
        u_ref[...] = (x_ref[...] * (1.0 + sc_ref[...]) + sh_ref[...]).astype(BF16)

    o_ref[...] = jnp.dot(u_ref[...], w_ref[...], preferred_element_type=F32).astype(o_ref.dtype)


def in_proj(x, sc, sh, w):
    b, s, d = x.shape
    n = w.shape[1]
    tm, tn = 1024, 1536
    return pl.pallas_call(
        _inproj_kernel,
        out_shape=jax.ShapeDtypeStruct((b, s, n), BF16),
        grid=(b, s // tm, n // tn),
        in_specs=[
            pl.BlockSpec((None, tm, d), lambda bi, i, j: (bi, i, 0)),
            pl.BlockSpec((None, 1, d), lambda bi, i, j: (bi, 0, 0)),
            pl.BlockSpec((None, 1, d), lambda bi, i, j: (bi, 0, 0)),
            pl.BlockSpec((d, tn), lambda bi, i, j: (0, j)),
        ],
        out_specs=pl.BlockSpec((None, tm, tn), lambda bi, i, j: (bi, i, j)),
        scratch_shapes=[pltpu.VMEM((tm, d), BF16)],
        compiler_params=_cparams(("parallel", "parallel", "arbitrary")),
        name="in_proj",
    )(x, sc, sh, w)


def _rope2(x, cos, sin_signed, lo):
    xr = jnp.where(lo, pltpu.roll(x, 96, 1), pltpu.roll(x, 32, 1))
    return x * cos + xr * sin_signed


def _lane_masks():
    lane = lax.broadcasted_iota(jnp.int32, (1, 128), 1)
    return (lane % 64) < 32, lane < 64


def _dil_attn_kernel(qkv_ref, cos_ref, sin_ref, o_ref, lse_ref, q_s, k_s, v_s, *, ls):
    lo, head0 = _lane_masks()
    cos = cos_ref[...]
    sin = sin_ref[...]
    pad = DIL_RADIUS
    zeros = jnp.zeros((pad, 128), BF16)
    k_s[0:pad, :] = zeros
    k_s[pad + ls:pad + ls + pad, :] = zeros
    v_s[0:pad, :] = zeros
    v_s[pad + ls:pad + ls + pad, :] = zeros
    qb = 128
    for hp in range(2):
        q = qkv_ref[:, hp * 128:(hp + 1) * 128].astype(F32)
        k = qkv_ref[:, 256 + hp * 128:256 + (hp + 1) * 128].astype(F32)
        q_s[...] = (_rope2(q, cos, sin, lo) * (HEAD_DIM ** -0.5)).astype(BF16)
        k_s[pad:pad + ls, :] = _rope2(k, cos, sin, lo).astype(BF16)
        v_s[pad:pad + ls, :] = qkv_ref[:, 512 + hp * 128:512 + (hp + 1) * 128]

        def blk(i, carry):
            q0 = pl.multiple_of(i * qb, qb)
            qblk = q_s[pl.ds(q0, qb), :]
            kw = k_s[pl.ds(q0, qb + 2 * pad), :]
            vw = v_s[pl.ds(q0, qb + 2 * pad), :]
            qpos = q0 + lax.broadcasted_iota(jnp.int32, (qb, 1), 0)
            kpos = q0 - pad + lax.broadcasted_iota(jnp.int32, (1, qb + 2 * pad), 1)
            kpos = jnp.where(kpos < 0, -4 * ls, jnp.where(kpos >= ls, -4 * ls, kpos))
            mask = jnp.abs(qpos - kpos) <= DIL_RADIUS
            res = []
            for hm in (head0, jnp.logical_not(head0)):
                qh = jnp.where(hm, qblk, jnp.zeros_like(qblk))
                s = lax.dot_general(qh, kw, (((1,), (1,)), ((), ())), preferred_element_type=F32)
                s = jnp.where(mask, s, NEG_INF)
                m = jnp.max(s, axis=-1, keepdims=True)
                p = jnp.exp(s - m)
                den = jnp.sum(p, axis=-1, keepdims=True)
                pv = jnp.dot(p.astype(BF16), vw, preferred_element_type=F32)
                res.append((pv / den, m + jnp.log(den)))
            o = jnp.where(head0, res[0][0], res[1][0])
            lse = jnp.where(head0, res[0][1], res[1][1])
            o_ref[pl.ds(q0, qb), hp * 128:(hp + 1) * 128] = o.astype(o_ref.dtype)
            lse_ref[pl.ds(q0, qb), hp * 128:(hp + 1) * 128] = lse
            return carry

        lax.fori_loop(0, ls // qb, blk, 0)


def dilated_attention(proj, cos_t, sin_t, group):
    b, s, n = proj.shape
    d = DIL_GROUPS[group][1]
    ls = s // d
    ncb = n // 768
    cb = COL_DIL // 768 + group
    out, lse = pl.pallas_call(
        functools.partial(_dil_attn_kernel, ls=ls),
        out_shape=(jax.ShapeDtypeStruct((b, ls, d * 256), BF16), jax.ShapeDtypeStruct((b, ls, d * 256), F32)),
        grid=(b, d),
        in_specs=[
            pl.BlockSpec((None, ls, 768), lambda bi, r: (bi, 0, r * ncb + cb)),
            pl.BlockSpec((ls, 128), lambda bi, r: (0, r)),
            pl.BlockSpec((ls, 128), lambda bi, r: (0, r)),
        ],
        out_specs=(
            pl.BlockSpec((None, ls, 256), lambda bi, r: (bi, 0, r)),
            pl.BlockSpec((None, ls, 256), lambda bi, r: (bi, 0, r)),
        ),
        scratch_shapes=[
            pltpu.VMEM((ls, 128), BF16),
            pltpu.VMEM((ls + 2 * DIL_RADIUS, 128), BF16),
            pltpu.VMEM((ls + 2 * DIL_RADIUS, 128), BF16),
        ],
        compiler_params=_cparams(("parallel", "parallel")),
        name=f"dil_attn_g{group}",
    )(proj.reshape(b, ls, d * n), cos_t.reshape(ls, d * 128), sin_t.reshape(ls, d * 128))
    return out.reshape(b, s, 256), lse.reshape(b, s, 256)


def _swa_kernel(qkv_ref, cos_ref, sin_ref, sink_ref, o_ref, q_s, k_s, v_s, *, s_len):
    lo, head0 = _lane_masks()
    cos = cos_ref[...]
    sin = sin_ref[...]
    pad = SWA_RADIUS
    qb = 128
    zeros = jnp.zeros((pad, 128), BF16)
    k_s[0:pad, :] = zeros
    k_s[pad + s_len:pad + s_len + pad, :] = zeros
    v_s[0:pad, :] = zeros
    v_s[pad + s_len:pad + s_len + pad, :] = zeros
    k = qkv_ref[:, 512:640].astype(F32)
    k_s[pad:pad + s_len, :] = _rope2(k, cos, sin, lo).astype(BF16)
    v_s[pad:pad + s_len, :] = qkv_ref[:, 640:768]
    for qp in range(4):
        q = qkv_ref[:, qp * 128:(qp + 1) * 128].astype(F32)
        q_s[...] = (_rope2(q, cos, sin, lo) * (HEAD_DIM ** -0.5)).astype(BF16)
        sinks = (sink_ref[qp:qp + 1, 0:1], sink_ref[4 + qp:5 + qp, 0:1])

        def blk(i, carry):
            q0 = pl.multiple_of(i * qb, qb)
            qblk = q_s[pl.ds(q0, qb), :]
            kw = k_s[pl.ds(q0, qb + 2 * pad), :]
            vw = v_s[pl.ds(q0, qb + 2 * pad), :]
            qpos = q0 + lax.broadcasted_iota(jnp.int32, (qb, 1), 0)
            kpos = q0 - pad + lax.broadcasted_iota(jnp.int32, (1, qb + 2 * pad), 1)
            kpos = jnp.where(kpos < 0, -4 * s_len, jnp.where(kpos >= s_len, -4 * s_len, kpos))
            mask = jnp.abs(qpos - kpos) <= SWA_RADIUS
            res = []
            for hm, sk in zip((head0, jnp.logical_not(head0)), sinks):
                qh = jnp.where(hm, qblk, jnp.zeros_like(qblk))
                s = lax.dot_general(qh, kw, (((1,), (1,)), ((), ())), preferred_element_type=F32)
                s = jnp.where(mask, s, NEG_INF)
                m = jnp.maximum(jnp.max(s, axis=-1, keepdims=True), sk)
                p = jnp.exp(s - m)
                den = jnp.sum(p, axis=-1, keepdims=True) + jnp.exp(sk - m)
                pv = jnp.dot(p.astype(BF16), vw, preferred_element_type=F32)
                res.append(pv / den)
            o = jnp.where(head0, res[0], res[1])
            o_ref[pl.ds(q0, qb), qp * 128:(qp + 1) * 128] = o.astype(o_ref.dtype)
            return carry

        lax.fori_loop(0, s_len // qb, blk, 0)


def swa_attention(proj, cos_t, sin_t, sink):
    b, s, n = proj.shape
    return pl.pallas_call(
        functools.partial(_swa_kernel, s_len=s),
        out_shape=jax.ShapeDtypeStruct((b, s, 512), BF16),
        grid=(b,),
        in_specs=[
            pl.BlockSpec((None, s, 768), lambda bi: (bi, 0, COL_SWA // 768)),
            pl.BlockSpec((s, 128), lambda bi: (0, 0)),
            pl.BlockSpec((s, 128), lambda bi: (0, 0)),
            pl.BlockSpec((8, 128), lambda bi: (0, 0)),
        ],
        out_specs=pl.BlockSpec((None, s, 512), lambda bi: (bi, 0, 0)),
        scratch_shapes=[
            pltpu.VMEM((s, 128), BF16),
            pltpu.VMEM((s + 2 * SWA_RADIUS, 128), BF16),
            pltpu.VMEM((s + 2 * SWA_RADIUS, 128), BF16),
        ],
        compiler_params=_cparams(("parallel",)),
        name="swa_attn",
    )(proj, cos_t, sin_t, jnp.broadcast_to(sink.astype(F32)[:, None], (8, 128)))


HY_CHUNKS = 4


def dft_tables(seq):
    kc = seq // HY_CHUNKS
    k = jnp.arange(seq, dtype=jnp.int32)
    phase = ((2 * k[:, None] + 1) * k[None, :]) % (4 * seq)
    ang = phase.astype(F32) * (2.0 * math.pi / (4 * seq))
    c = jnp.cos(ang).reshape(HY_CHUNKS, kc, seq)
    s = jnp.sin(ang).reshape(HY_CHUNKS, kc, seq)
    fwd = jnp.concatenate([c, s], axis=1).astype(BF16)
    inv = (jnp.concatenate([c, -s], axis=1) * (1.0 / seq)).transpose(0, 2, 1).astype(BF16)
    return fwd, inv


def _hy_filter_kernel(w1_ref, b1_ref, w2_ref, b2_ref, w3_ref, fr_ref, ld_ref, bands_ref, f_ref,
                      hr_ref, hi_ref, hs_s, hd_s, *, seq):
    hw = HYENA_WIDTH
    hi_p = lax.Precision.HIGHEST

    @pl.when(pl.program_id(2) == 0)
    def _():
        idx = lax.broadcasted_iota(jnp.int32, (seq, 1), 0).astype(F32)
        t = idx / max(seq - 1, 1)
        w = 2.0 * math.pi * idx / seq
        ang = w * bands_ref[...]
        w1 = w1_ref[...]
        pre = (t * w1[0:1, :]
               + jnp.dot(jnp.cos(ang), w1[1:1 + HYENA_BANDS, :], precision=hi_p, preferred_element_type=F32)
               - jnp.dot(jnp.sin(ang), w1[1 + HYENA_BANDS:1 + 2 * HYENA_BANDS, :], precision=hi_p,
                         preferred_element_type=F32)
               + b1_ref[...])
        h = jnp.sin(fr_ref[0:1, :] * pre)
        h = jnp.sin(fr_ref[1:2, :] * (jnp.dot(h, w2_ref[...], precision=hi_p, preferred_element_type=F32)
                                      + b2_ref[...]))
        h = jnp.dot(h, w3_ref[...], precision=hi_p, preferred_element_type=F32)
        h = h * jnp.exp(-t * jnp.exp(ld_ref[...]))
        hf = h[:, :hw]
        hb = jnp.where(idx > 0.0, h[:, hw:], 0.0)
        inv = lax.rsqrt(jnp.sum(hf * hf + hb * hb, axis=0, keepdims=True) + 1e-12)
        hs_s[...] = ((hf + hb) * inv).astype(BF16)
        hd_s[...] = ((hf - hb) * inv).astype(BF16)

    kc = seq // HY_CHUNKS
    hr_ref[...] = jnp.dot(f_ref[0:kc, :], hs_s[...], preferred_element_type=F32)
    hi_ref[...] = -jnp.dot(f_ref[kc:2 * kc, :], hd_s[...], preferred_element_type=F32)


def hyena_filters(w1, b1, w2, b2, w3p, freq, ldp, fwd, seq):
    depth = w1.shape[0]
    kc = seq // HY_CHUNKS
    hw = HYENA_WIDTH
    bands = jnp.linspace(1e-4, HYENA_BANDS - 1, HYENA_BANDS, dtype=F32).reshape(1, HYENA_BANDS)
    full = lambda *shape: pl.BlockSpec((None,) + shape, lambda l, o, c: (l,) + (0,) * len(shape))
    out_sds = jax.ShapeDtypeStruct((depth, 2, seq, hw), F32)
    return pl.pallas_call(
        functools.partial(_hy_filter_kernel, seq=seq),
        out_shape=(out_sds, out_sds),
        grid=(depth, 2, HY_CHUNKS),
        in_specs=[
            full(*w1.shape[1:]), full(1, b1.shape[-1]), full(*w2.shape[1:]), full(1, b2.shape[-1]),
            pl.BlockSpec((None, w3p.shape[1], 2 * hw), lambda l, o, c: (l, 0, o)),
            full(*freq.shape[1:]),
            pl.BlockSpec((None, 1, 2 * hw), lambda l, o, c: (l, 0, o)),
            pl.BlockSpec((1, HYENA_BANDS), lambda l, o, c: (0, 0)),
            pl.BlockSpec((None, 2 * kc, seq), lambda l, o, c: (c, 0, 0)),
        ],
        out_specs=(
            pl.BlockSpec((None, None, kc, hw), lambda l, o, c: (l, o, c, 0)),
            pl.BlockSpec((None, None, kc, hw), lambda l, o, c: (l, o, c, 0)),
        ),
        scratch_shapes=[pltpu.VMEM((seq, hw), BF16), pltpu.VMEM((seq, hw), BF16)],
        compiler_params=_cparams(("arbitrary", "arbitrary", "arbitrary")),
        name="hyena_filters",
    )(w1, b1[:, None, :], w2, b2[:, None, :], w3p, freq, ldp[:, None, :], bands, fwd)


def _hyena_kernel(hy_ref, cw_ref, cb_ref, bias_ref, hr_ref, hi_ref, f_ref, g_ref, o_ref, zb_s, zf_s, acc_s, *, seq):
    hw = HYENA_WIDTH
    o = pl.program_id(1)
    c = pl.program_id(2)
    kc = seq // HY_CHUNKS

    def short_conv(part):
        x = hy_ref[:, part * hw:(part + 1) * hw].astype(F32)
        row = lax.broadcasted_iota(jnp.int32, (seq, 1), 0)
        xm = jnp.where(row == 0, 0.0, pltpu.roll(x, 1, 0))
        xp = jnp.where(row == seq - 1, 0.0, pltpu.roll(x, seq - 1, 0))
        w = cw_ref[:, part * hw:(part + 1) * hw]
        return cb_ref[:, part * hw:(part + 1) * hw] + xm * w[0:1, :] + x * w[1:2, :] + xp * w[2:3, :]

    @pl.when((o == 0) & (c == 0))
    def _():
        z0 = short_conv(0)
        zf_s[...] = z0
        zb_s[...] = z0.astype(BF16)

    zz = jnp.dot(f_ref[...], zb_s[...], preferred_element_type=F32)
    zc = zz[:kc]
    zs = zz[kc:]
    hr = hr_ref[...]
    hi = hi_ref[...]
    yr = zc * hr + zs * hi
    yi = zc * hi - zs * hr
    yy = jnp.concatenate([yr, yi], axis=0).astype(BF16)
    part = jnp.dot(g_ref[...], yy, preferred_element_type=F32)

    @pl.when(c == 0)
    def _():
        acc_s[...] = part

    @pl.when(c > 0)
    def _():
        acc_s[...] += part

    @pl.when((c == HY_CHUNKS - 1) & (o == 0))
    def _():
        z1 = short_conv(1) * (acc_s[...] + bias_ref[0:1, :] * zf_s[...])
        zf_s[...] = z1
        zb_s[...] = z1.astype(BF16)

    @pl.when((c == HY_CHUNKS - 1) & (o == 1))
    def _():
        o_ref[...] = (short_conv(2) * (acc_s[...] + bias_ref[1:2, :] * zf_s[...])).astype(o_ref.dtype)


def hyena_mixer(proj, conv_w, conv_b, hy_bias, hr, hi, fwd, inv):
    b, s, n = proj.shape
    hw = HYENA_WIDTH
    kc = s // HY_CHUNKS
    return pl.pallas_call(
        functools.partial(_hyena_kernel, seq=s),
        out_shape=jax.ShapeDtypeStruct((b, s, hw), BF16),
        grid=(b, 2, HY_CHUNKS),
        in_specs=[
            pl.BlockSpec((None, s, 3 * hw), lambda bi, o, c: (bi, 0, COL_HY // (3 * hw))),
            pl.BlockSpec((3, 3 * hw), lambda bi, o, c: (0, 0)),
            pl.BlockSpec((1, 3 * hw), lambda bi, o, c: (0, 0)),
            pl.BlockSpec((2, hw), lambda bi, o, c: (0, 0)),
            pl.BlockSpec((None, kc, hw), lambda bi, o, c: (o, c, 0)),
            pl.BlockSpec((None, kc, hw), lambda bi, o, c: (o, c, 0)),
            pl.BlockSpec((None, 2 * kc, s), lambda bi, o, c: (c, 0, 0)),
            pl.BlockSpec((None, s, 2 * kc), lambda bi, o, c: (c, 0, 0)),
        ],
        out_specs=pl.BlockSpec((None, s, hw), lambda bi, o, c: (bi, 0, 0)),
        scratch_shapes=[pltpu.VMEM((s, hw), BF16), pltpu.VMEM((s, hw), F32), pltpu.VMEM((s, hw), F32)],
        compiler_params=_cparams(("parallel", "arbitrary", "arbitrary")),
        name="hyena_conv",
    )(proj, conv_w, conv_b.reshape(1, -1), hy_bias, hr, hi, fwd, inv)


def _layer_norm(y, g, b):
    mu = jnp.mean(y, axis=-1, keepdims=True)
    yc = y - mu
    var = jnp.mean(yc * yc, axis=-1, keepdims=True)
    return yc * lax.rsqrt(var + LN_EPS) * g + b


def _merge_kernel(oa0, oa1, oa2, l0, l1, l2, yb_ref, yc_ref, gl_ref, x_ref, g1_ref, sc2_ref, sh2_ref,
                  lng_ref, lnb_ref, wa_ref, wb_ref, wc_ref, wo_ref, xo_ref, u2_ref):
    d = D_MODEL
    la, lb, lc = l0[...], l1[...], l2[...]
    m = jnp.maximum(jnp.maximum(la, lb), lc)
    ea, eb, ec = jnp.exp(la - m), jnp.exp(lb - m), jnp.exp(lc - m)
    inv = 1.0 / (ea + eb + ec)
    ya = ((ea * inv) * oa0[...].astype(F32) + (eb * inv) * oa1[...].astype(F32)
          + (ec * inv) * oa2[...].astype(F32))
    za = jnp.dot(ya.astype(BF16), wa_ref[...], preferred_element_type=F32)
    zb = jnp.dot(yb_ref[...], wb_ref[...], preferred_element_type=F32)
    zc = jnp.dot(yc_ref[...], wc_ref[...], preferred_element_type=F32)
    merged = (jax.nn.sigmoid(gl_ref[:, 0:d].astype(F32)) * za
              + jax.nn.sigmoid(gl_ref[:, d:2 * d].astype(F32)) * zb
              + jax.nn.sigmoid(gl_ref[:, 2 * d:3 * d].astype(F32)) * zc)
    mix = jnp.dot(merged.astype(BF16), wo_ref[...], preferred_element_type=F32)
    xn = _layer_norm(ALPHA * x_ref[...] + g1_ref[...] * mix, lng_ref[...], lnb_ref[...])
    xo_ref[...] = xn
    u2_ref[...] = (xn * (1.0 + sc2_ref[...]) + sh2_ref[...]).astype(BF16)


def merge_mixers(oa, lse, yb, yc, proj, x, g1, sc2, sh2, ln_g, ln_b, wa, wb, wc, wo):
    b, s, d = x.shape
    tm = 512
    row = lambda w: pl.BlockSpec((None, tm, w), lambda bi, i: (bi, i, 0))
    per_b = pl.BlockSpec((None, 1, d), lambda bi, i: (bi, 0, 0))
    const = lambda r, c: pl.BlockSpec((r, c), lambda bi, i: (0, 0))
    return pl.pallas_call(
        _merge_kernel,
        out_shape=(jax.ShapeDtypeStruct((b, s, d), F32), jax.ShapeDtypeStruct((b, s, d), BF16)),
        grid=(b, s // tm),
        in_specs=[row(256)] * 6 + [row(512), row(512), row(3 * d), row(d), per_b, per_b, per_b,
                                    const(1, d), const(1, d), const(256, d), const(512, d), const(512, d),
                                    const(d, d)],
        out_specs=(row(d), row(d)),
        compiler_params=_cparams(("parallel", "parallel")),
        name="merge_mixers",
    )(*oa, *lse, yb, yc, proj, x, g1, sc2, sh2, ln_g.reshape(1, d), ln_b.reshape(1, d), wa, wb, wc, wo)


_PEER_PAIRS = [(i, j) for i in range(1, PEER_TOPK + 1) for j in range(1, PEER_TOPK + 1) if i * j <= PEER_TOPK]
PEER_NOT_TOP = 100.0


def _peer_route_kernel(u_ref, wq_ref, keys_ref, e1_ref, cut_ref, e2_ref, r2_ref, q_s, s_s, ab_s, st_s, *, tr):
    nh, nk, topk = PEER_HEADS, PEER_KEYS, PEER_TOPK
    qt = lax.dot_general(wq_ref[...], u_ref[...], (((1,), (1,)), ((), ())), preferred_element_type=F32)
    q_s[...] = qt.astype(BF16)
    for ph in range(2 * nh):
        s_s[ph] = jnp.dot(keys_ref[ph], q_s[ph * 128:(ph + 1) * 128, :], preferred_element_type=F32)

    def extract(ph, carry):
        p = ph // nh
        h = ph % nh

        def rnd(r, prev):
            sv = s_s[ph]
            cur = jnp.max(jnp.where(sv < prev, sv, -jnp.inf), axis=0, keepdims=True)
            ab_s[p, r, pl.ds(h, 1), :] = cur
            return cur

        lax.fori_loop(0, topk, rnd, jnp.full((1, tr), jnp.inf, F32))
        return carry

    lax.fori_loop(0, 2 * nh, extract, 0)

    for ch in range(tr // 128):
        ln = slice(ch * 128, (ch + 1) * 128)
        a = [ab_s[0, r, :, ln] for r in range(topk)]
        b = [ab_s[1, r, :, ln] for r in range(topk)]
        cand = [a[i - 1] + b[j - 1] for (i, j) in _PEER_PAIRS]
        tau = jnp.full((nh, 128), -jnp.inf, F32)
        for x, (ix, jx) in enumerate(_PEER_PAIRS):
            cnt = jnp.zeros((nh, 128), F32)
            for y, (iy, jy) in enumerate(_PEER_PAIRS):
                if iy <= ix and jy <= jx:
                    cnt = cnt + 1.0
                elif iy >= ix and jy >= jx:
                    continue
                else:
                    cnt = cnt + jnp.where(cand[y] >= cand[x], 1.0, 0.0)
            tau = jnp.maximum(tau, jnp.where(cnt >= float(topk), cand[x], -jnp.inf))
        top = cand[0]
        z = jnp.zeros((nh, 128), F32)
        for cx in cand:
            z = z + jnp.where(cx >= tau, jnp.exp(cx - top), 0.0)
        st_s[0, :, ln] = tau
        st_s[1, :, ln] = 1.0 / z

    for h in range(nh):
        rows = slice(h * nk, (h + 1) * nk)
        s1 = s_s[h]
        s2 = s_s[nh + h]
        hrow = slice(h, h + 1)
        tau = st_s[0, hrow, :]
        in1 = s1 >= ab_s[0, topk - 1, hrow, :]
        in2 = s2 >= ab_s[1, topk - 1, hrow, :]
        cut = jnp.zeros_like(s1)
        rank = jnp.ones_like(s2)
        for r in range(topk):
            br = ab_s[1, r, hrow, :]
            cut = cut + jnp.where(s1 + br >= tau, 1.0, 0.0)
            rank = rank + jnp.where(br > s2, 1.0, 0.0)
        e1_ref[rows, :] = jnp.where(in1, jnp.exp(s1 - ab_s[0, 0, hrow, :]), 0.0)
        cut_ref[rows, :] = jnp.where(in1, cut, 0.0)
        e2_ref[rows, :] = jnp.where(in2, jnp.exp(s2 - ab_s[1, 0, hrow, :]) * st_s[1, hrow, :], 0.0).astype(BF16)
        r2_ref[rows, :] = jnp.where(in2, rank, PEER_NOT_TOP).astype(BF16)


def peer_route(u2, wq_t, keys):
    t, d = u2.shape
    tr = 512
    nrow = PEER_HEADS * PEER_KEYS
    tab = lambda dt: jax.ShapeDtypeStruct((nrow, t), dt)
    col = pl.BlockSpec((nrow, tr), lambda i: (0, i))
    return pl.pallas_call(
        functools.partial(_peer_route_kernel, tr=tr),
        out_shape=(tab(F32), tab(F32), tab(BF16), tab(BF16)),
        grid=(t // tr,),
        in_specs=[
            pl.BlockSpec((tr, d), lambda i: (i, 0)),
            pl.BlockSpec(wq_t.shape, lambda i: (0, 0)),
            pl.BlockSpec(keys.shape, lambda i: (0, 0, 0)),
        ],
        out_specs=(col, col, col, col),
        scratch_shapes=[
            pltpu.VMEM((2 * nrow, tr), BF16),
            pltpu.VMEM((2 * PEER_HEADS, PEER_KEYS, tr), F32),
            pltpu.VMEM((2, PEER_TOPK, PEER_HEADS, tr), F32),
            pltpu.VMEM((2, PEER_HEADS, tr), F32),
        ],
        compiler_params=_cparams(("parallel",)),
        name="peer_route",
    )(u2, wq_t, keys)


def _peer_dense_kernel(x_ref, u_ref, vt_ref, e1_ref, cut_ref, e2_ref, r2_ref, o_ref, acc_s, wg_s, *, te):
    j = pl.program_id(1)
    nk = PEER_KEYS
    at = lax.dot_general(u_ref[...], x_ref[...], (((1,), (1,)), ((), ())), preferred_element_type=F32)
    g = (0.5 * at * (1.0 + lax.erf(at * (2.0 ** -0.5)))).astype(BF16)
    for ii in range(te // nk):
        i = j * (te // nk) + ii
        w = None
        for h in range(PEER_HEADS):
            cb = cut_ref[pl.ds(h * nk + i, 1), :].astype(BF16)
            e1 = e1_ref[pl.ds(h * nk + i, 1), :].astype(BF16)
            sel = jnp.where(r2_ref[h * nk:(h + 1) * nk, :] <= cb, e2_ref[h * nk:(h + 1) * nk, :],
                            jnp.zeros((), BF16)) * e1
            w = sel if w is None else w + sel
        wg_s[ii * nk:(ii + 1) * nk, :] = g[ii * nk:(ii + 1) * nk, :] * w
    part = jnp.dot(vt_ref[...], wg_s[...], preferred_element_type=F32)

    @pl.when(j == 0)
    def _():
        acc_s[...] = part

    @pl.when(j > 0)
    def _():
        acc_s[...] += part

    @pl.when(j == pl.num_programs(1) - 1)
    def _():
        o_ref[...] = acc_s[...].T


def peer_dense(u2, u_tab, vt_tab, e1, cut, e2, r2):
    t, d = u2.shape
    ne = u_tab.shape[0]
    tn, te = 512, 1024
    nrow = PEER_HEADS * PEER_KEYS
    col = pl.BlockSpec((nrow, tn), lambda i, j: (0, i))
    return pl.pallas_call(
        functools.partial(_peer_dense_kernel, te=te),
        out_shape=jax.ShapeDtypeStruct((t, d), F32),
        grid=(t // tn, ne // te),
        in_specs=[
            pl.BlockSpec((tn, d), lambda i, j: (i, 0)),
            pl.BlockSpec((te, d), lambda i, j: (j, 0)),
            pl.BlockSpec((d, te), lambda i, j: (0, j)),
            col, col, col, col,
        ],
        out_specs=pl.BlockSpec((tn, d), lambda i, j: (i, 0)),
        scratch_shapes=[pltpu.VMEM((d, tn), F32), pltpu.VMEM((te, tn), BF16)],
        compiler_params=_cparams(("parallel", "arbitrary")),
        name="peer_dense",
    )(u2, u_tab, vt_tab, e1, cut, e2, r2)


def _peer_out_kernel(x_ref, f_ref, g2_ref, lng_ref, lnb_ref, o_ref):
    o_ref[...] = _layer_norm(ALPHA * x_ref[...] + g2_ref[...] * f_ref[...], lng_ref[...], lnb_ref[...])


def peer_residual(x, ffn, g2, ln_g, ln_b):
    b, s, d = x.shape
    tm = 1024
    row = pl.BlockSpec((None, tm, d), lambda bi, i: (bi, i, 0))
    const = pl.BlockSpec((1, d), lambda bi, i: (0, 0))
    return pl.pallas_call(
        _peer_out_kernel,
        out_shape=jax.ShapeDtypeStruct((b, s, d), F32),
        grid=(b, s // tm),
        in_specs=[row, row, pl.BlockSpec((None, 1, d), lambda bi, i: (bi, 0, 0)), const, const],
        out_specs=row,
        compiler_params=_cparams(("parallel", "parallel")),
        name="peer_residual",
    )(x, ffn.reshape(b, s, d), g2, ln_g.reshape(1, d), ln_b.reshape(1, d))


def rope_tables(seq):
    pos = jnp.arange(seq, dtype=F32)
    inv = ROPE_THETA ** (-jnp.arange(0, HEAD_DIM, 2, dtype=F32) / HEAD_DIM)
    ang = pos[:, None] * inv[None, :]
    sign = jnp.where((jnp.arange(128) % 64) < 32, -1.0, 1.0).astype(F32)
    return jnp.tile(jnp.cos(ang), (1, 4)), jnp.tile(jnp.sin(ang), (1, 4)) * sign[None, :]


def _swa_head_perm():
    heads = [h for p in range(4) for h in (p, 4 + p)]
    return jnp.concatenate([jnp.arange(h * HEAD_DIM, (h + 1) * HEAD_DIM) for h in heads])


def relayout_w_in(w):
    qa, ka, va = w[:, 0:768], w[:, 768:1536], w[:, 1536:2304]
    hy = w[:, 2304:3840]
    qc, kvc = w[:, 3840:4352], w[:, 4352:4608]
    gates = w[:, 4608:7680]
    dil = [jnp.concatenate([t[:, 256 * g:256 * (g + 1)] for t in (qa, ka, va)], axis=1) for g in range(3)]
    return jnp.concatenate([gates, hy] + dil + [qc[:, _swa_head_perm()], kvc], axis=1).astype(BF16)


def token_mixer(x, ada, l, w_in_r, conv_w, conv_b, hy_bias, attn_sink, hr, hi, fwd, inv, cos_t, sin_t,
                wa, wb, wc, wo, ln_g, ln_b):
    sh1, sc1, g1, sh2, sc2, _ = [a[:, None, :] for a in jnp.split(ada, 6, axis=-1)]
    proj = in_proj(x, sc1, sh1, w_in_r)
    oa, lse = zip(*[dilated_attention(proj, cos_t, sin_t, g) for g in range(3)])
    yb = hyena_mixer(proj, conv_w, conv_b, hy_bias, hr, hi, fwd, inv)
    yc = swa_attention(proj, cos_t, sin_t, attn_sink)
    return merge_mixers(oa, lse, yb, yc, proj, x, g1, sc2, sh2, ln_g, ln_b, wa, wb, wc, wo)


def peer_ffn(u2, wq, keys, u_tab, v_tab):
    d = u2.shape[1]
    nh, nk = PEER_HEADS, PEER_KEYS
    wq_t = wq.reshape(d, nh, 2, nk).transpose(2, 1, 3, 0).reshape(2 * nh * nk, d).astype(BF16)
    keys_r = keys.transpose(1, 0, 2, 3).reshape(2 * nh, nk, keys.shape[-1]).astype(BF16)
    tabs = peer_route(u2, wq_t, keys_r)
    return peer_dense(u2, u_tab.astype(BF16), v_tab.T.astype(BF16), *tabs)


def kernel(x, c, w_ada, b_ada, w_in, conv_w, conv_b, hy_w1, hy_b1, hy_w2, hy_b2, hy_w3, hy_freq, hy_log_decay,
           hy_bias, attn_sink, w_branch_a, w_branch_b, w_branch_c, w_out, ln_g, ln_b, peer_wq, peer_keys, peer_u,
           peer_v):
    b, s, d = x.shape
    depth = w_in.shape[0]
    hw = HYENA_WIDTH
    cos_t, sin_t = rope_tables(s)
    fwd, inv = dft_tables(s)
    ada = ada_ln(c, w_ada, b_ada)
    w3p = hy_w3.reshape(depth, -1, 2, 2, hw).transpose(0, 1, 3, 2, 4).reshape(depth, -1, 4 * hw)
    ldp = hy_log_decay.reshape(depth, 2, 2, hw).transpose(0, 2, 1, 3).reshape(depth, 4 * hw)
    hr, hi = hyena_filters(hy_w1, hy_b1, hy_w2, hy_b2, w3p, hy_freq, ldp, fwd, s)
    perm = _swa_head_perm()
    for l in range(depth):
        x1, u2 = token_mixer(x, ada[l], l, relayout_w_in(w_in[l]), conv_w[l], conv_b[l], hy_bias[l], attn_sink[l],
                             hr[l], hi[l], fwd, inv, cos_t, sin_t, w_branch_a[l].astype(BF16),
                             w_branch_b[l].astype(BF16), w_branch_c[l][perm].astype(BF16), w_out[l].astype(BF16),
                             ln_g[l, 0], ln_b[l, 0])
        ffn = peer_ffn(u2.reshape(b * s, d), peer_wq[l], peer_keys[l], peer_u[l], peer_v[l])
        x = peer_residual(x1, ffn, ada[l][:, None, 5 * d:6 * d], ln_g[l, 1], ln_b[l, 1])
    return x
```

```python
import functools
import math

import jax
import jax.numpy as jnp
from jax import lax
from jax.experimental import pallas as pl
from jax.experimental.pallas import tpu as pltpu

F32 = jnp.float32
BF16 = jnp.bfloat16

D_MODEL = 1024
HEAD_DIM = 64
DIL_GROUPS = ((128, 1), (512, 4), (2048, 16))
DIL_RADIUS = 64
HYENA_WIDTH = 512
HYENA_BANDS = 16
SWA_RADIUS = 128
SWA_Q_HEADS = 8
SWA_KV_HEADS = 2
PEER_HEADS = 8
PEER_KEYS = 128
PEER_TOPK = 16
DEPTH = 2
ALPHA = (2 * DEPTH) ** 0.25
LN_EPS = 1e-5
NEG_INF = -1e30
ROPE_THETA = 10000.0

COL_GATES = 0
COL_HY = 3072
COL_DIL0 = 4608
COL_SWA = 5376
MAIN_WIDTH = 6144
QKV_W = 768

VMEM_LIMIT = 56 * 1024 * 1024


def _cparams(sem, vmem=VMEM_LIMIT):
    return pltpu.CompilerParams(dimension_semantics=sem, vmem_limit_bytes=vmem)


def _ada_kernel(c_ref, w_ref, b_ref, o_ref):
    c = c_ref[...]
    cond = (c * jax.nn.sigmoid(c)).astype(BF16)
    o_ref[...] = jnp.dot(cond, w_ref[...].astype(BF16), preferred_element_type=F32) + b_ref[...]


def ada_ln(c, w_ada, b_ada):
    depth, d, n = w_ada.shape
    b = c.shape[0]
    tn = 1536
    return pl.pallas_call(
        _ada_kernel,
        out_shape=jax.ShapeDtypeStruct((depth, b, n), F32),
        grid=(depth, n // tn),
        in_specs=[
            pl.BlockSpec((b, d), lambda l, j: (0, 0)),
            pl.BlockSpec((None, d, tn), lambda l, j: (l, 0, j)),
            pl.BlockSpec((None, 1, tn), lambda l, j: (l, 0, j)),
        ],
        out_specs=pl.BlockSpec((None, b, tn), lambda l, j: (l, 0, j)),
        compiler_params=_cparams(("arbitrary", "arbitrary")),
        name="ada_ln",
    )(c, w_ada, b_ada.reshape(depth, 1, n))


def _inproj_kernel(x_ref, sc_ref, sh_ref, w_ref, o_ref, u_ref):
    @pl.when(pl.program_id(2) == 0)
    def _():
        u_ref[...] = (x_ref[...] * (1.0 + sc_ref[...]) + sh_ref[...]).astype(BF16)

    o_ref[...] = jnp.dot(u_ref[...], w_ref[...], preferred_element_type=F32).astype(o_ref.dtype)


def in_proj(x, sc, sh, w):
    b, s, d = x.shape
    n = w.shape[1]
    tm, tn = 1024, 1536
    return pl.pallas_call(
        _inproj_kernel,
        out_shape=jax.ShapeDtypeStruct((b, s, n), BF16),
        grid=(b, s // tm, n // tn),
        in_specs=[
            pl.BlockSpec((None, tm, d), lambda bi, i, j: (bi, i, 0)),
            pl.BlockSpec((None, 1, d), lambda bi, i, j: (bi, 0, 0)),
            pl.BlockSpec((None, 1, d), lambda bi, i, j: (bi, 0, 0)),
            pl.BlockSpec((d, tn), lambda bi, i, j: (0, j)),
        ],
        out_specs=pl.BlockSpec((None, tm, tn), lambda bi, i, j: (bi, i, j)),
        scratch_shapes=[pltpu.VMEM((tm, d), BF16)],
        compiler_params=_cparams(("parallel", "parallel", "arbitrary")),
        name="in_proj",
    )(x, sc, sh, w)


def _inproj_dil_kernel(x_ref, sc_ref, sh_ref, w_ref, o1_ref, o2_ref, res_s):
    tm = x_ref.shape[0]
    u = (x_ref[...] * (1.0 + sc_ref[...]) + sh_ref[...]).astype(BF16)
    res = jnp.dot(u, w_ref[...], preferred_element_type=F32)
    nslab = QKV_W // 128
    for c in range(2 * nslab):
        res_s[c] = res[:, c * 128:(c + 1) * 128]
    for g, o_ref in ((0, o1_ref), (1, o2_ref)):
        dil = o_ref.shape[0]
        for r in range(dil):
            for c in range(nslab):
                o_ref[r, :, c * 128:(c + 1) * 128] = res_s[g * nslab + c, pl.ds(r, tm // dil, stride=dil), :].astype(
                    o_ref.dtype)


def in_proj_dil(x, sc, sh, w):
    b, s, d = x.shape
    tm = 1024
    d1, d2 = DIL_GROUPS[1][1], DIL_GROUPS[2][1]
    return pl.pallas_call(
        _inproj_dil_kernel,
        out_shape=(jax.ShapeDtypeStruct((b, d1, s // d1, QKV_W), BF16),
                   jax.ShapeDtypeStruct((b, d2, s // d2, QKV_W), BF16)),
        grid=(b, s // tm),
        in_specs=[
            pl.BlockSpec((None, tm, d), lambda bi, i: (bi, i, 0)),
            pl.BlockSpec((None, 1, d), lambda bi, i: (bi, 0, 0)),
            pl.BlockSpec((None, 1, d), lambda bi, i: (bi, 0, 0)),
            pl.BlockSpec((d, 2 * QKV_W), lambda bi, i: (0, 0)),
        ],
        out_specs=(
            pl.BlockSpec((None, d1, tm // d1, QKV_W), lambda bi, i: (bi, 0, i, 0)),
            pl.BlockSpec((None, d2, tm // d2, QKV_W), lambda bi, i: (bi, 0, i, 0)),
        ),
        scratch_shapes=[pltpu.VMEM((2 * QKV_W // 128, tm, 128), F32)],
        compiler_params=_cparams(("parallel", "parallel")),
        name="in_proj_dil",
    )(x, sc, sh, w)


def _rope2(x, cos, sin_signed, lo):
    xr = jnp.where(lo, pltpu.roll(x, 96, 1), pltpu.roll(x, 32, 1))
    return x * cos + xr * sin_signed


def _lane_masks():
    lane = lax.broadcasted_iota(jnp.int32, (1, 128), 1)
    return (lane % 64) < 32, lane < 64


def _band_mask(q0, nq, nk, pad, radius, length):
    qpos = q0 + (lax.broadcasted_iota(jnp.int32, (nq, 1), 0) & 127)
    kpos = q0 - pad + lax.broadcasted_iota(jnp.int32, (1, nk), 1)
    kpos = jnp.where(kpos < 0, -4 * length, jnp.where(kpos >= length, -4 * length, kpos))
    return jnp.abs(qpos - kpos) <= radius


def _dil_attn_kernel(qkv_ref, cos_ref, sin_ref, olo_ref, ohi_ref, llo_ref, lhi_ref, q_s, k_s, v_s, *, dil, ls):
    lo, head0 = _lane_masks()
    heads = (head0, jnp.logical_not(head0))
    cos = cos_ref[...]
    sin = sin_ref[...]
    pad = DIL_RADIUS
    qb = 128
    nblk = ls // qb
    ngroup = 4
    zeros = jnp.zeros((dil, pad, 128), BF16)
    k_s[:, 0:pad, :] = zeros
    k_s[:, pad + ls:pad + ls + pad, :] = zeros
    v_s[:, 0:pad, :] = zeros
    v_s[:, pad + ls:pad + ls + pad, :] = zeros
    for hp, (o_ref, l_ref) in enumerate(((olo_ref, llo_ref), (ohi_ref, lhi_ref))):
        q = qkv_ref[:, :, hp * 128:(hp + 1) * 128].astype(F32).reshape(dil * ls, 128)
        k = qkv_ref[:, :, 256 + hp * 128:256 + (hp + 1) * 128].astype(F32).reshape(dil * ls, 128)
        q_s[...] = (_rope2(q, cos, sin, lo) * (HEAD_DIM ** -0.5)).astype(BF16)
        k_s[:, pad:pad + ls, :] = _rope2(k, cos, sin, lo).astype(BF16).reshape(dil, ls, 128)
        v_s[:, pad:pad + ls, :] = qkv_ref[:, :, 512 + hp * 128:512 + (hp + 1) * 128]

        def grp(gi, carry):
            tiles = []
            for t in range(ngroup):
                c = gi * ngroup + t
                r = c // nblk
                q0 = pl.multiple_of((c % nblk) * qb, qb)
                qblk = q_s[pl.ds(pl.multiple_of(c * qb, qb), qb), :]
                kw = k_s[r, pl.ds(q0, qb + 2 * pad), :]
                vw = v_s[r, pl.ds(q0, qb + 2 * pad), :]
                mask = _band_mask(q0, qb, qb + 2 * pad, pad, DIL_RADIUS, ls)
                ss = [lax.dot_general(jnp.where(hm, qblk, jnp.zeros_like(qblk)), kw, (((1,), (1,)), ((), ())),
                                      preferred_element_type=F32) for hm in heads]
                tiles.append((r, q0, vw, mask, ss))
            soft = []
            for r, q0, vw, mask, ss in tiles:
                for s in ss:
                    s = jnp.where(mask, s, NEG_INF)
                    m = jnp.max(s, axis=-1, keepdims=True)
                    p = jnp.exp(s - m)
                    soft.append((p.astype(BF16), jnp.sum(p, axis=-1, keepdims=True), m))
            for ti, (r, q0, vw, mask, ss) in enumerate(tiles):
                res = []
                for h in range(2):
                    p, den, m = soft[2 * ti + h]
                    res.append((jnp.dot(p, vw, preferred_element_type=F32) / den, m + jnp.log(den)))
                o = jnp.where(head0, res[0][0], res[1][0])
                lse = jnp.where(head0, res[0][1], res[1][1])
                rows = pl.ds(q0, qb) if dil == 1 else pl.ds(q0 * dil + r, qb, stride=dil)
                o_ref[rows, :] = o
                l_ref[rows, :] = lse
            return carry

        lax.fori_loop(0, dil * nblk // ngroup, grp, 0)


def dilated_attention(qkv, col_block, cos_d, sin_d):
    b, dil, ls, _ = qkv.shape
    s = dil * ls
    out_sds = jax.ShapeDtypeStruct((b, s, 128), F32)
    out_spec = pl.BlockSpec((None, s, 128), lambda bi: (bi, 0, 0))
    return pl.pallas_call(
        functools.partial(_dil_attn_kernel, dil=dil, ls=ls),
        out_shape=(out_sds,) * 4,
        grid=(b,),
        in_specs=[
            pl.BlockSpec((None, dil, ls, QKV_W), lambda bi: (bi, 0, 0, col_block)),
            pl.BlockSpec((s, 128), lambda bi: (0, 0)),
            pl.BlockSpec((s, 128), lambda bi: (0, 0)),
        ],
        out_specs=(out_spec,) * 4,
        scratch_shapes=[
            pltpu.VMEM((s, 128), BF16),
            pltpu.VMEM((dil, ls + 2 * DIL_RADIUS, 128), BF16),
            pltpu.VMEM((dil, ls + 2 * DIL_RADIUS, 128), BF16),
        ],
        compiler_params=_cparams(("parallel",)),
        name=f"dil_attn_d{dil}",
    )(qkv, cos_d, sin_d)


def _swa_kernel(qkv_ref, cos_ref, sin_ref, sink_ref, o_ref, q_s, k_s, v_s, *, s_len):
    lo, head0 = _lane_masks()
    heads = (head0, jnp.logical_not(head0))
    cos = cos_ref[...]
    sin = sin_ref[...]
    pad = SWA_RADIUS
    qb = 128
    grp = SWA_Q_HEADS // SWA_KV_HEADS
    zeros = jnp.zeros((pad, 128), BF16)
    k_s[0:pad, :] = zeros
    k_s[pad + s_len:pad + s_len + pad, :] = zeros
    v_s[0:pad, :] = zeros
    v_s[pad + s_len:pad + s_len + pad, :] = zeros
    k = qkv_ref[:, 512:640].astype(F32)
    k_s[pad:pad + s_len, :] = _rope2(k, cos, sin, lo).astype(BF16)
    v_s[pad:pad + s_len, :] = qkv_ref[:, 640:768]
    for qp in range(grp):
        q = qkv_ref[:, qp * 128:(qp + 1) * 128].astype(F32)
        q_s[qp] = (_rope2(q, cos, sin, lo) * (HEAD_DIM ** -0.5)).astype(BF16)
    sinks = [jnp.concatenate([jnp.broadcast_to(sink_ref[kv * grp + p:kv * grp + p + 1, 0:1], (qb, 1))
                              for p in range(grp)], axis=0) for kv in range(SWA_KV_HEADS)]

    def blk(i, carry):
        q0 = pl.multiple_of(i * qb, qb)
        kw = k_s[pl.ds(q0, qb + 2 * pad), :]
        vw = v_s[pl.ds(q0, qb + 2 * pad), :]
        qall = jnp.concatenate([q_s[p, pl.ds(q0, qb), :] for p in range(grp)], axis=0)
        mask = _band_mask(q0, grp * qb, qb + 2 * pad, pad, SWA_RADIUS, s_len)
        ss = [lax.dot_general(jnp.where(hm, qall, jnp.zeros_like(qall)), kw, (((1,), (1,)), ((), ())),
                              preferred_element_type=F32) for hm in heads]
        soft = []
        for s, sk in zip(ss, sinks):
            s = jnp.where(mask, s, NEG_INF)
            m = jnp.maximum(jnp.max(s, axis=-1, keepdims=True), sk)
            p = jnp.exp(s - m)
            soft.append((p.astype(BF16), jnp.sum(p, axis=-1, keepdims=True) + jnp.exp(sk - m)))
        res = [jnp.dot(p, vw, preferred_element_type=F32) / den for p, den in soft]
        o = jnp.where(head0, res[0], res[1]).astype(o_ref.dtype)
        for p in range(grp):
            o_ref[pl.ds(q0, qb), p * 128:(p + 1) * 128] = o[p * qb:(p + 1) * qb, :]
        return carry

    lax.fori_loop(0, s_len // qb, blk, 0)


def swa_attention(proj, cos_t, sin_t, sink):
    b, s, n = proj.shape
    return pl.pallas_call(
        functools.partial(_swa_kernel, s_len=s),
        out_shape=jax.ShapeDtypeStruct((b, s, 512), BF16),
        grid=(b,),
        in_specs=[
            pl.BlockSpec((None, s, QKV_W), lambda bi: (bi, 0, COL_SWA // QKV_W)),
            pl.BlockSpec((s, 128), lambda bi: (0, 0)),
            pl.BlockSpec((s, 128), lambda bi: (0, 0)),
            pl.BlockSpec((8, 128), lambda bi: (0, 0)),
        ],
        out_specs=pl.BlockSpec((None, s, 512), lambda bi: (bi, 0, 0)),
        scratch_shapes=[
            pltpu.VMEM((SWA_Q_HEADS // SWA_KV_HEADS, s, 128), BF16),
            pltpu.VMEM((s + 2 * SWA_RADIUS, 128), BF16),
            pltpu.VMEM((s + 2 * SWA_RADIUS, 128), BF16),
        ],
        compiler_params=_cparams(("parallel",)),
        name="swa_attn",
    )(proj, cos_t, sin_t, jnp.broadcast_to(sink.astype(F32)[:, None], (8, 128)))


HY_CHUNKS = 4


def dft_tables(seq):
    kc = seq // HY_CHUNKS
    k = jnp.arange(seq, dtype=jnp.int32)
    phase = ((2 * k[:, None] + 1) * k[None, :]) % (4 * seq)
    ang = phase.astype(F32) * (2.0 * math.pi / (4 * seq))
    c = jnp.cos(ang).reshape(HY_CHUNKS, kc, seq)
    s = jnp.sin(ang).reshape(HY_CHUNKS, kc, seq)
    fwd = jnp.concatenate([c, s], axis=1).astype(BF16)
    inv = (jnp.concatenate([c, -s], axis=1) * (1.0 / seq)).transpose(0, 2, 1).astype(BF16)
    return fwd, inv


def _hy_filter_kernel(w1_ref, b1_ref, w2_ref, b2_ref, w3_ref, fr_ref, ld_ref, bands_ref, f_ref,
                      hr_ref, hi_ref, hs_s, hd_s, *, seq):
    hw = HYENA_WIDTH
    hi_p = lax.Precision.HIGHEST

    @pl.when(pl.program_id(2) == 0)
    def _():
        idx = lax.broadcasted_iota(jnp.int32, (seq, 1), 0).astype(F32)
        t = idx / max(seq - 1, 1)
        w = 2.0 * math.pi * idx / seq
        ang = w * bands_ref[...]
        w1 = w1_ref[...]
        pre = (t * w1[0:1, :]
               + jnp.dot(jnp.cos(ang), w1[1:1 + HYENA_BANDS, :], precision=hi_p, preferred_element_type=F32)
               - jnp.dot(jnp.sin(ang), w1[1 + HYENA_BANDS:1 + 2 * HYENA_BANDS, :], precision=hi_p,
                         preferred_element_type=F32)
               + b1_ref[...])
        h = jnp.sin(fr_ref[0:1, :] * pre)
        h = jnp.sin(fr_ref[1:2, :] * (jnp.dot(h, w2_ref[...], precision=hi_p, preferred_element_type=F32)
                                      + b2_ref[...]))
        h = jnp.dot(h, w3_ref[...], precision=hi_p, preferred_element_type=F32)
        h = h * jnp.exp(-t * jnp.exp(ld_ref[...]))
        hf = h[:, :hw]
        hb = jnp.where(idx > 0.0, h[:, hw:], 0.0)
        inv = lax.rsqrt(jnp.sum(hf * hf + hb * hb, axis=0, keepdims=True) + 1e-12)
        hs_s[...] = ((hf + hb) * inv).astype(BF16)
        hd_s[...] = ((hf - hb) * inv).astype(BF16)

    kc = seq // HY_CHUNKS
    hr_ref[...] = jnp.dot(f_ref[0:kc, :], hs_s[...], preferred_element_type=F32)
    hi_ref[...] = -jnp.dot(f_ref[kc:2 * kc, :], hd_s[...], preferred_element_type=F32)


def hyena_filters(w1, b1, w2, b2, w3p, freq, ldp, fwd, seq):
    depth = w1.shape[0]
    kc = seq // HY_CHUNKS
    hw = HYENA_WIDTH
    bands = jnp.linspace(1e-4, HYENA_BANDS - 1, HYENA_BANDS, dtype=F32).reshape(1, HYENA_BANDS)
    full = lambda *shape: pl.BlockSpec((None,) + shape, lambda l, o, c: (l,) + (0,) * len(shape))
    out_sds = jax.ShapeDtypeStruct((depth, 2, seq, hw), F32)
    return pl.pallas_call(
        functools.partial(_hy_filter_kernel, seq=seq),
        out_shape=(out_sds, out_sds),
        grid=(depth, 2, HY_CHUNKS),
        in_specs=[
            full(*w1.shape[1:]), full(1, b1.shape[-1]), full(*w2.shape[1:]), full(1, b2.shape[-1]),
            pl.BlockSpec((None, w3p.shape[1], 2 * hw), lambda l, o, c: (l, 0, o)),
            full(*freq.shape[1:]),
            pl.BlockSpec((None, 1, 2 * hw), lambda l, o, c: (l, 0, o)),
            pl.BlockSpec((1, HYENA_BANDS), lambda l, o, c: (0, 0)),
            pl.BlockSpec((None, 2 * kc, seq), lambda l, o, c: (c, 0, 0)),
        ],
        out_specs=(
            pl.BlockSpec((None, None, kc, hw), lambda l, o, c: (l, o, c, 0)),
            pl.BlockSpec((None, None, kc, hw), lambda l, o, c: (l, o, c, 0)),
        ),
        scratch_shapes=[pltpu.VMEM((seq, hw), BF16), pltpu.VMEM((seq, hw), BF16)],
        compiler_params=_cparams(("arbitrary", "arbitrary", "arbitrary")),
        name="hyena_filters",
    )(w1, b1[:, None, :], w2, b2[:, None, :], w3p, freq, ldp[:, None, :], bands, fwd)


def _hyena_kernel(hy_ref, cw_ref, cb_ref, bias_ref, hr_ref, hi_ref, f_ref, g_ref, o_ref, zb_s, zf_s, acc_s, *, seq):
    hw = HYENA_WIDTH
    o = pl.program_id(1)
    c = pl.program_id(2)
    kc = seq // HY_CHUNKS

    def short_conv(part):
        x = hy_ref[:, part * hw:(part + 1) * hw].astype(F32)
        row = lax.broadcasted_iota(jnp.int32, (seq, 1), 0)
        xm = jnp.where(row == 0, 0.0, pltpu.roll(x, 1, 0))
        xp = jnp.where(row == seq - 1, 0.0, pltpu.roll(x, seq - 1, 0))
        w = cw_ref[:, part * hw:(part + 1) * hw]
        return cb_ref[:, part * hw:(part + 1) * hw] + xm * w[0:1, :] + x * w[1:2, :] + xp * w[2:3, :]

    @pl.when((o == 0) & (c == 0))
    def _():
        z0 = short_conv(0)
        zf_s[...] = z0
        zb_s[...] = z0.astype(BF16)

    zz = jnp.dot(f_ref[...], zb_s[...], preferred_element_type=F32)
    zc = zz[:kc]
    zs = zz[kc:]
    hr = hr_ref[...]
    hi = hi_ref[...]
    yr = zc * hr + zs * hi
    yi = zc * hi - zs * hr
    yy = jnp.concatenate([yr, yi], axis=0).astype(BF16)
    part = jnp.dot(g_ref[...], yy, preferred_element_type=F32)

    @pl.when(c == 0)
    def _():
        acc_s[...] = part

    @pl.when(c > 0)
    def _():
        acc_s[...] += part

    @pl.when((c == HY_CHUNKS - 1) & (o == 0))
    def _():
        z1 = short_conv(1) * (acc_s[...] + bias_ref[0:1, :] * zf_s[...])
        zf_s[...] = z1
        zb_s[...] = z1.astype(BF16)

    @pl.when((c == HY_CHUNKS - 1) & (o == 1))
    def _():
        o_ref[...] = (short_conv(2) * (acc_s[...] + bias_ref[1:2, :] * zf_s[...])).astype(o_ref.dtype)


def hyena_mixer(proj, conv_w, conv_b, hy_bias, hr, hi, fwd, inv):
    b, s, n = proj.shape
    hw = HYENA_WIDTH
    kc = s // HY_CHUNKS
    return pl.pallas_call(
        functools.partial(_hyena_kernel, seq=s),
        out_shape=jax.ShapeDtypeStruct((b, s, hw), BF16),
        grid=(b, 2, HY_CHUNKS),
        in_specs=[
            pl.BlockSpec((None, s, 3 * hw), lambda bi, o, c: (bi, 0, COL_HY // (3 * hw))),
            pl.BlockSpec((3, 3 * hw), lambda bi, o, c: (0, 0)),
            pl.BlockSpec((1, 3 * hw), lambda bi, o, c: (0, 0)),
            pl.BlockSpec((2, hw), lambda bi, o, c: (0, 0)),
            pl.BlockSpec((None, kc, hw), lambda bi, o, c: (o, c, 0)),
            pl.BlockSpec((None, kc, hw), lambda bi, o, c: (o, c, 0)),
            pl.BlockSpec((None, 2 * kc, s), lambda bi, o, c: (c, 0, 0)),
            pl.BlockSpec((None, s, 2 * kc), lambda bi, o, c: (c, 0, 0)),
        ],
        out_specs=pl.BlockSpec((None, s, hw), lambda bi, o, c: (bi, 0, 0)),
        scratch_shapes=[pltpu.VMEM((s, hw), BF16), pltpu.VMEM((s, hw), F32), pltpu.VMEM((s, hw), F32)],
        compiler_params=_cparams(("parallel", "arbitrary", "arbitrary")),
        name="hyena_conv",
    )(proj, conv_w, conv_b.reshape(1, -1), hy_bias, hr, hi, fwd, inv)


def _layer_norm(y, g, b):
    mu = jnp.mean(y, axis=-1, keepdims=True)
    yc = y - mu
    var = jnp.mean(yc * yc, axis=-1, keepdims=True)
    return yc * lax.rsqrt(var + LN_EPS) * g + b


def _merge_kernel(*refs):
    att = refs[:12]
    (yb_ref, yc_ref, gl_ref, x_ref, g1_ref, sc2_ref, sh2_ref, lng_ref, lnb_ref, wa_ref, wb_ref, wc_ref, wo_ref,
     xo_ref, u2_ref) = refs[12:]
    d = D_MODEL
    halves = []
    for half in range(2):
        la, lb, lc = (att[4 * g + 2 + half][...] for g in range(3))
        m = jnp.maximum(jnp.maximum(la, lb), lc)
        ea, eb, ec = jnp.exp(la - m), jnp.exp(lb - m), jnp.exp(lc - m)
        inv = 1.0 / (ea + eb + ec)
        halves.append((ea * inv) * att[half][...] + (eb * inv) * att[4 + half][...] + (ec * inv) * att[8 + half][...])
    ya = jnp.concatenate(halves, axis=-1)
    za = jnp.dot(ya.astype(BF16), wa_ref[...], preferred_element_type=F32)
    zb = jnp.dot(yb_ref[...], wb_ref[...], preferred_element_type=F32)
    zc = jnp.dot(yc_ref[...], wc_ref[...], preferred_element_type=F32)
    merged = (jax.nn.sigmoid(gl_ref[:, 0:d].astype(F32)) * za
              + jax.nn.sigmoid(gl_ref[:, d:2 * d].astype(F32)) * zb
              + jax.nn.sigmoid(gl_ref[:, 2 * d:3 * d].astype(F32)) * zc)
    mix = jnp.dot(merged.astype(BF16), wo_ref[...], preferred_element_type=F32)
    xn = _layer_norm(ALPHA * x_ref[...] + g1_ref[...] * mix, lng_ref[...], lnb_ref[...])
    xo_ref[...] = xn
    u2_ref[...] = (xn * (1.0 + sc2_ref[...]) + sh2_ref[...]).T.astype(BF16)


def merge_mixers(att, yb, yc, proj, x, g1, sc2, sh2, ln_g, ln_b, wa, wb, wc, wo):
    b, s, d = x.shape
    tm = 512
    row = lambda w: pl.BlockSpec((None, tm, w), lambda bi, i: (bi, i, 0))
    per_b = pl.BlockSpec((None, 1, d), lambda bi, i: (bi, 0, 0))
    const = lambda r, c: pl.BlockSpec((r, c), lambda bi, i: (0, 0))
    return pl.pallas_call(
        _merge_kernel,
        out_shape=(jax.ShapeDtypeStruct((b, s, d), F32), jax.ShapeDtypeStruct((d, b * s), BF16)),
        grid=(b, s // tm),
        in_specs=[row(128)] * 12 + [row(512), row(512), row(3 * d), row(d), per_b, per_b, per_b,
                                    const(1, d), const(1, d), const(256, d), const(512, d), const(512, d),
                                    const(d, d)],
        out_specs=(row(d), pl.BlockSpec((d, tm), lambda bi, i: (0, bi * (s // tm) + i))),
        compiler_params=_cparams(("parallel", "parallel")),
        name="merge_mixers",
    )(*att, yb, yc, proj, x, g1, sc2, sh2, ln_g.reshape(1, d), ln_b.reshape(1, d), wa, wb, wc, wo)


_PEER_PAIRS = [(i, j) for i in range(1, PEER_TOPK + 1) for j in range(1, PEER_TOPK + 1) if i * j <= PEER_TOPK]
PEER_NOT_TOP = 100.0


def _peer_route_kernel(u_ref, wq_ref, keys_ref, e1_ref, cut_ref, e2_ref, r2_ref, q_s, s_s, ab_s, st_s, *, tr):
    nh, nk, topk = PEER_HEADS, PEER_KEYS, PEER_TOPK
    qt = jnp.dot(wq_ref[...], u_ref[...], preferred_element_type=F32)
    q_s[...] = qt.astype(BF16)
    for ph in range(2 * nh):
        s_s[ph] = jnp.dot(keys_ref[ph], q_s[ph * 128:(ph + 1) * 128, :], preferred_element_type=F32)

    def extract(ph, carry):
        p = ph // nh
        h = ph % nh

        def rnd(r, prev):
            sv = s_s[ph]
            cur = jnp.max(jnp.where(sv < prev, sv, -jnp.inf), axis=0, keepdims=True)
            ab_s[p, r, pl.ds(h, 1), :] = cur
            return cur

        lax.fori_loop(0, topk, rnd, jnp.full((1, tr), jnp.inf, F32))
        return carry

    lax.fori_loop(0, 2 * nh, extract, 0)

    for ch in range(tr // 128):
        ln = slice(ch * 128, (ch + 1) * 128)
        a = [ab_s[0, r, :, ln] for r in range(topk)]
        b = [ab_s[1, r, :, ln] for r in range(topk)]
        cand = [a[i - 1] + b[j - 1] for (i, j) in _PEER_PAIRS]
        tau = jnp.full((nh, 128), -jnp.inf, F32)
        for x, (ix, jx) in enumerate(_PEER_PAIRS):
            cnt = jnp.zeros((nh, 128), F32)
            for y, (iy, jy) in enumerate(_PEER_PAIRS):
                if iy <= ix and jy <= jx:
                    cnt = cnt + 1.0
                elif iy >= ix and jy >= jx:
                    continue
                else:
                    cnt = cnt + jnp.where(cand[y] >= cand[x], 1.0, 0.0)
            tau = jnp.maximum(tau, jnp.where(cnt >= float(topk), cand[x], -jnp.inf))
        top = cand[0]
        z = jnp.zeros((nh, 128), F32)
        for cx in cand:
            z = z + jnp.where(cx >= tau, jnp.exp(cx - top), 0.0)
        st_s[0, :, ln] = tau
        st_s[1, :, ln] = 1.0 / z

    for h in range(nh):
        rows = slice(h * nk, (h + 1) * nk)
        s1 = s_s[h]
        s2 = s_s[nh + h]
        hrow = slice(h, h + 1)
        tau = st_s[0, hrow, :]
        in1 = s1 >= ab_s[0, topk - 1, hrow, :]
        in2 = s2 >= ab_s[1, topk - 1, hrow, :]
        cut = jnp.zeros_like(s1)
        rank = jnp.ones_like(s2)
        for r in range(topk):
            br = ab_s[1, r, hrow, :]
            cut = cut + jnp.where(s1 + br >= tau, 1.0, 0.0)
            rank = rank + jnp.where(br > s2, 1.0, 0.0)
        e1_ref[rows, :] = jnp.where(in1, jnp.exp(s1 - ab_s[0, 0, hrow, :]), 0.0)
        cut_ref[rows, :] = jnp.where(in1, cut, 0.0)
        e2_ref[rows, :] = jnp.where(in2, jnp.exp(s2 - ab_s[1, 0, hrow, :]) * st_s[1, hrow, :], 0.0).astype(BF16)
        r2_ref[rows, :] = jnp.where(in2, rank, PEER_NOT_TOP).astype(BF16)


def peer_route(u2, wq_t, keys):
    d, t = u2.shape
    tr = 512
    nrow = PEER_HEADS * PEER_KEYS
    tab = lambda dt: jax.ShapeDtypeStruct((nrow, t), dt)
    col = pl.BlockSpec((nrow, tr), lambda i: (0, i))
    return pl.pallas_call(
        functools.partial(_peer_route_kernel, tr=tr),
        out_shape=(tab(F32), tab(F32), tab(BF16), tab(BF16)),
        grid=(t // tr,),
        in_specs=[
            pl.BlockSpec((d, tr), lambda i: (0, i)),
            pl.BlockSpec(wq_t.shape, lambda i: (0, 0)),
            pl.BlockSpec(keys.shape, lambda i: (0, 0, 0)),
        ],
        out_specs=(col, col, col, col),
        scratch_shapes=[
            pltpu.VMEM((2 * nrow, tr), BF16),
            pltpu.VMEM((2 * PEER_HEADS, PEER_KEYS, tr), F32),
            pltpu.VMEM((2, PEER_TOPK, PEER_HEADS, tr), F32),
            pltpu.VMEM((2, PEER_HEADS, tr), F32),
        ],
        compiler_params=_cparams(("parallel",)),
        name="peer_route",
    )(u2, wq_t, keys)


def _peer_dense_kernel(x_ref, u_ref, vt_ref, e1_ref, cut_ref, e2_ref, r2_ref, o_ref, acc_s, wga_s, wgb_s, *, te, n_e):
    j = pl.program_id(1)
    nk = PEER_KEYS
    tn = x_ref.shape[1]
    sub = 16

    @pl.when(j == 0)
    def _():
        acc_s[...] = jnp.zeros_like(acc_s)
        wgb_s[...] = jnp.zeros_like(wgb_s)

    def step(w_cur, w_prev):
        jj = jnp.minimum(j, n_e - 1)
        qrows = te // 4
        for q in range(4):
            rows = slice(q * qrows, (q + 1) * qrows)
            at = jnp.dot(u_ref[rows, :], x_ref[...], preferred_element_type=F32)
            acc_s[...] += jnp.dot(vt_ref[:, rows], w_prev[rows, :], preferred_element_type=F32)
            g = (0.5 * at * (1.0 + lax.erf(at * (2.0 ** -0.5)))).astype(BF16)
            for iq in range(qrows // nk):
                ii = q * (qrows // nk) + iq
                i = jj * (te // nk) + ii
                w = None
                for h in range(PEER_HEADS):
                    cb = jnp.broadcast_to(cut_ref[pl.ds(h * nk + i, 1), :], (sub, tn)).astype(BF16)[None]
                    e1 = jnp.broadcast_to(e1_ref[pl.ds(h * nk + i, 1), :], (sub, tn)).astype(BF16)[None]
                    r2 = r2_ref[h * nk:(h + 1) * nk, :].reshape(nk // sub, sub, tn)
                    e2 = e2_ref[h * nk:(h + 1) * nk, :].reshape(nk // sub, sub, tn)
                    sel = jnp.where(r2 <= cb, e2, jnp.zeros((), BF16)) * e1
                    w = sel if w is None else w + sel
                w_cur[ii * nk:(ii + 1) * nk, :] = g[iq * nk:(iq + 1) * nk, :] * w.reshape(nk, tn)

    @pl.when((j % 2 == 0) & (j < n_e))
    def _():
        step(wga_s, wgb_s)

    @pl.when(j % 2 == 1)
    def _():
        step(wgb_s, wga_s)

    @pl.when(j == n_e)
    def _():
        o_ref[...] = (acc_s[...] + jnp.dot(vt_ref[...], wgb_s[...], preferred_element_type=F32)).T


def peer_dense(u2, u_tab, vt_tab, e1, cut, e2, r2):
    d, t = u2.shape
    ne = u_tab.shape[0]
    tn, te = 512, 1024
    n_e = ne // te
    nrow = PEER_HEADS * PEER_KEYS
    col = pl.BlockSpec((nrow, tn), lambda i, j: (0, i))
    return pl.pallas_call(
        functools.partial(_peer_dense_kernel, te=te, n_e=n_e),
        out_shape=jax.ShapeDtypeStruct((t, d), F32),
        grid=(t // tn, n_e + 1),
        in_specs=[
            pl.BlockSpec((d, tn), lambda i, j: (0, i)),
            pl.BlockSpec((te, d), lambda i, j: (jnp.minimum(j, n_e - 1), 0)),
            pl.BlockSpec((d, te), lambda i, j: (0, jnp.maximum(j - 1, 0))),
            col, col, col, col,
        ],
        out_specs=pl.BlockSpec((tn, d), lambda i, j: (i, 0)),
        scratch_shapes=[pltpu.VMEM((d, tn), F32), pltpu.VMEM((te, tn), BF16), pltpu.VMEM((te, tn), BF16)],
        compiler_params=_cparams(("parallel", "arbitrary")),
        name="peer_dense",
    )(u2, u_tab, vt_tab, e1, cut, e2, r2)


def _peer_out_kernel(x_ref, f_ref, g2_ref, lng_ref, lnb_ref, o_ref):
    o_ref[...] = _layer_norm(ALPHA * x_ref[...] + g2_ref[...] * f_ref[...], lng_ref[...], lnb_ref[...])


def peer_residual(x, ffn, g2, ln_g, ln_b):
    b, s, d = x.shape
    tm = 1024
    row = pl.BlockSpec((None, tm, d), lambda bi, i: (bi, i, 0))
    const = pl.BlockSpec((1, d), lambda bi, i: (0, 0))
    return pl.pallas_call(
        _peer_out_kernel,
        out_shape=jax.ShapeDtypeStruct((b, s, d), F32),
        grid=(b, s // tm),
        in_specs=[row, row, pl.BlockSpec((None, 1, d), lambda bi, i: (bi, 0, 0)), const, const],
        out_specs=row,
        compiler_params=_cparams(("parallel", "parallel")),
        name="peer_residual",
    )(x, ffn.reshape(b, s, d), g2, ln_g.reshape(1, d), ln_b.reshape(1, d))


def rope_tables(seq):
    pos = jnp.arange(seq, dtype=F32)
    inv = ROPE_THETA ** (-jnp.arange(0, HEAD_DIM, 2, dtype=F32) / HEAD_DIM)
    ang = pos[:, None] * inv[None, :]
    sign = jnp.where((jnp.arange(128) % 64) < 32, -1.0, 1.0).astype(F32)
    return jnp.tile(jnp.cos(ang), (1, 4)), jnp.tile(jnp.sin(ang), (1, 4)) * sign[None, :]


def _swa_head_perm():
    heads = [h for p in range(4) for h in (p, 4 + p)]
    return jnp.concatenate([jnp.arange(h * HEAD_DIM, (h + 1) * HEAD_DIM) for h in heads])


def relayout_w_in(w):
    qa, ka, va = w[:, 0:768], w[:, 768:1536], w[:, 1536:2304]
    hy = w[:, 2304:3840]
    qc, kvc = w[:, 3840:4352], w[:, 4352:4608]
    gates = w[:, 4608:7680]
    dil = [jnp.concatenate([t[:, 256 * g:256 * (g + 1)] for t in (qa, ka, va)], axis=1) for g in range(3)]
    main = jnp.concatenate([gates, hy, dil[0], qc[:, _swa_head_perm()], kvc], axis=1).astype(BF16)
    return main, jnp.concatenate(dil[1:], axis=1).astype(BF16)


def residue_major(tab, dil):
    s = tab.shape[0]
    return tab.reshape(s // dil, dil, 128).transpose(1, 0, 2).reshape(s, 128)


def token_mixer(x, ada, w_main, w_dil, conv_w, conv_b, hy_bias, attn_sink, hr, hi, fwd, inv, rope_tabs,
                wa, wb, wc, wo, ln_g, ln_b):
    b, s, _ = x.shape
    sh1, sc1, g1, sh2, sc2, _ = [a[:, None, :] for a in jnp.split(ada, 6, axis=-1)]
    proj = in_proj(x, sc1, sh1, w_main)
    qkv1, qkv2 = in_proj_dil(x, sc1, sh1, w_dil)
    att = (dilated_attention(proj.reshape(b, 1, s, MAIN_WIDTH), COL_DIL0 // QKV_W, *rope_tabs[0])
           + dilated_attention(qkv1, 0, *rope_tabs[1]) + dilated_attention(qkv2, 0, *rope_tabs[2]))
    yb = hyena_mixer(proj, conv_w, conv_b, hy_bias, hr, hi, fwd, inv)
    yc = swa_attention(proj, *rope_tabs[0], attn_sink)
    return merge_mixers(att, yb, yc, proj, x, g1, sc2, sh2, ln_g, ln_b, wa, wb, wc, wo)


def peer_ffn(u2, wq, keys, u_tab, v_tab):
    d = u2.shape[0]
    nh, nk = PEER_HEADS, PEER_KEYS
    wq_t = wq.reshape(d, nh, 2, nk).transpose(2, 1, 3, 0).reshape(2 * nh * nk, d).astype(BF16)
    keys_r = keys.transpose(1, 0, 2, 3).reshape(2 * nh, nk, keys.shape[-1]).astype(BF16)
    tabs = peer_route(u2, wq_t, keys_r)
    return peer_dense(u2, u_tab.astype(BF16), v_tab.T.astype(BF16), *tabs)


def kernel(x, c, w_ada, b_ada, w_in, conv_w, conv_b, hy_w1, hy_b1, hy_w2, hy_b2, hy_w3, hy_freq, hy_log_decay,
           hy_bias, attn_sink, w_branch_a, w_branch_b, w_branch_c, w_out, ln_g, ln_b, peer_wq, peer_keys, peer_u,
           peer_v):
    b, s, d = x.shape
    depth = w_in.shape[0]
    hw = HYENA_WIDTH
    cos_t, sin_t = rope_tables(s)
    fwd, inv = dft_tables(s)
    ada = ada_ln(c, w_ada, b_ada)
    w3p = hy_w3.reshape(depth, -1, 2, 2, hw).transpose(0, 1, 3, 2, 4).reshape(depth, -1, 4 * hw)
    ldp = hy_log_decay.reshape(depth, 2, 2, hw).transpose(0, 2, 1, 3).reshape(depth, 4 * hw)
    hr, hi = hyena_filters(hy_w1, hy_b1, hy_w2, hy_b2, w3p, hy_freq, ldp, fwd, s)
    perm = _swa_head_perm()
    rope_tabs = [(residue_major(cos_t, dil), residue_major(sin_t, dil)) for _, dil in DIL_GROUPS]
    for l in range(depth):
        x1, u2 = token_mixer(x, ada[l], *relayout_w_in(w_in[l]), conv_w[l], conv_b[l], hy_bias[l], attn_sink[l],
                             hr[l], hi[l], fwd, inv, rope_tabs, w_branch_a[l].astype(BF16),
                             w_branch_b[l].astype(BF16), w_branch_c[l][perm].astype(BF16), w_out[l].astype(BF16),
                             ln_g[l, 0], ln_b[l, 0])
        ffn = peer_ffn(u2, peer_wq[l], peer_keys[l], peer_u[l], peer_v[l])
        x = peer_residual(x1, ffn, ada[l][:, None, 5 * d:6 * d], ln_g[l, 1], ln_b[l, 1])
    return x
```

```python
import functools
import math

import jax
import jax.numpy as jnp
from jax import lax
from jax.experimental import pallas as pl
from jax.experimental.pallas import tpu as pltpu

F32 = jnp.float32
BF16 = jnp.bfloat16

D_MODEL = 1024
HEAD_DIM = 64
DIL_GROUPS = ((128, 1), (512, 4), (2048, 16))
DIL_RADIUS = 64
HYENA_WIDTH = 512
HYENA_BANDS = 16
SWA_RADIUS = 128
SWA_Q_HEADS = 8
SWA_KV_HEADS = 2
PEER_HEADS = 8
PEER_KEYS = 128
PEER_TOPK = 16
DEPTH = 2
ALPHA = (2 * DEPTH) ** 0.25
LN_EPS = 1e-5
NEG_INF = -1e30
ROPE_THETA = 10000.0

COL_GATES = 0
COL_HY = 3072
COL_DIL0 = 4608
COL_SWA = 5376
MAIN_WIDTH = 6144
QKV_W = 768

VMEM_LIMIT = 56 * 1024 * 1024


def _cparams(sem, vmem=VMEM_LIMIT):
    return pltpu.CompilerParams(dimension_semantics=sem, vmem_limit_bytes=vmem)


def _ada_kernel(c_ref, w_ref, b_ref, o_ref):
    c = c_ref[...]
    cond = (c * jax.nn.sigmoid(c)).astype(BF16)
    o_ref[...] = jnp.dot(cond, w_ref[...].astype(BF16), preferred_element_type=F32) + b_ref[...]


def ada_ln(c, w_ada, b_ada):
    depth, d, n = w_ada.shape
    b = c.shape[0]
    tn = 1536
    return pl.pallas_call(
        _ada_kernel,
        out_shape=jax.ShapeDtypeStruct((depth, b, n), F32),
        grid=(depth, n // tn),
        in_specs=[
            pl.BlockSpec((b, d), lambda l, j: (0, 0)),
            pl.BlockSpec((None, d, tn), lambda l, j: (l, 0, j)),
            pl.BlockSpec((None, 1, tn), lambda l, j: (l, 0, j)),
        ],
        out_specs=pl.BlockSpec((None, b, tn), lambda l, j: (l, 0, j)),
        compiler_params=_cparams(("arbitrary", "arbitrary")),
        name="ada_ln",
    )(c, w_ada, b_ada.reshape(depth, 1, n))


def _inproj_kernel(x_ref, sc_ref, sh_ref, w_ref, o_ref, u_ref):
    @pl.when(pl.program_id(2) == 0)
    def _():
        u_ref[...] = (x_ref[...] * (1.0 + sc_ref[...]) + sh_ref[...]).astype(BF16)

    o_ref[...] = jnp.dot(u_ref[...], w_ref[...], preferred_element_type=F32).astype(o_ref.dtype)


def in_proj(x, sc, sh, w):
    b, s, d = x.shape
    n = w.shape[1]
    tm, tn = 1024, 1536
    return pl.pallas_call(
        _inproj_kernel,
        out_shape=jax.ShapeDtypeStruct((b, s, n), BF16),
        grid=(b, s // tm, n // tn),
        in_specs=[
            pl.BlockSpec((None, tm, d), lambda bi, i, j: (bi, i, 0)),
            pl.BlockSpec((None, 1, d), lambda bi, i, j: (bi, 0, 0)),
            pl.BlockSpec((None, 1, d), lambda bi, i, j: (bi, 0, 0)),
            pl.BlockSpec((d, tn), lambda bi, i, j: (0, j)),
        ],
        out_specs=pl.BlockSpec((None, tm, tn), lambda bi, i, j: (bi, i, j)),
        scratch_shapes=[pltpu.VMEM((tm, d), BF16)],
        compiler_params=_cparams(("parallel", "parallel", "arbitrary")),
        name="in_proj",
    )(x, sc, sh, w)


def _inproj_dil_kernel(x_ref, sc_ref, sh_ref, w_ref, o1_ref, o2_ref, res_s):
    tm = x_ref.shape[0]
    u = (x_ref[...] * (1.0 + sc_ref[...]) + sh_ref[...]).astype(BF16)
    res = jnp.dot(u, w_ref[...], preferred_element_type=F32)
    nslab = QKV_W // 128
    for c in range(2 * nslab):
        res_s[c] = res[:, c * 128:(c + 1) * 128]
    for g, o_ref in ((0, o1_ref), (1, o2_ref)):
        dil = o_ref.shape[0]
        for r in range(dil):
            for c in range(nslab):
                o_ref[r, :, c * 128:(c + 1) * 128] = res_s[g * nslab + c, pl.ds(r, tm // dil, stride=dil), :].astype(
                    o_ref.dtype)


def in_proj_dil(x, sc, sh, w):
    b, s, d = x.shape
    tm = 1024
    d1, d2 = DIL_GROUPS[1][1], DIL_GROUPS[2][1]
    return pl.pallas_call(
        _inproj_dil_kernel,
        out_shape=(jax.ShapeDtypeStruct((b, d1, s // d1, QKV_W), BF16),
                   jax.ShapeDtypeStruct((b, d2, s // d2, QKV_W), BF16)),
        grid=(b, s // tm),
        in_specs=[
            pl.BlockSpec((None, tm, d), lambda bi, i: (bi, i, 0)),
            pl.BlockSpec((None, 1, d), lambda bi, i: (bi, 0, 0)),
            pl.BlockSpec((None, 1, d), lambda bi, i: (bi, 0, 0)),
            pl.BlockSpec((d, 2 * QKV_W), lambda bi, i: (0, 0)),
        ],
        out_specs=(
            pl.BlockSpec((None, d1, tm // d1, QKV_W), lambda bi, i: (bi, 0, i, 0)),
            pl.BlockSpec((None, d2, tm // d2, QKV_W), lambda bi, i: (bi, 0, i, 0)),
        ),
        scratch_shapes=[pltpu.VMEM((2 * QKV_W // 128, tm, 128), F32)],
        compiler_params=_cparams(("parallel", "parallel")),
        name="in_proj_dil",
    )(x, sc, sh, w)


def _rope2(x, cos, sin_signed, lo):
    xr = jnp.where(lo, pltpu.roll(x, 96, 1), pltpu.roll(x, 32, 1))
    return x * cos + xr * sin_signed


def _lane_masks():
    lane = lax.broadcasted_iota(jnp.int32, (1, 128), 1)
    return (lane % 64) < 32, lane < 64


def _band_mask(q0, nq, nk, pad, radius, length):
    qpos = q0 + (lax.broadcasted_iota(jnp.int32, (nq, 1), 0) & 127)
    kpos = q0 - pad + lax.broadcasted_iota(jnp.int32, (1, nk), 1)
    kpos = jnp.where(kpos < 0, -4 * length, jnp.where(kpos >= length, -4 * length, kpos))
    return jnp.abs(qpos - kpos) <= radius


def _dil_attn_kernel(qkv_ref, cos_ref, sin_ref, olo_ref, ohi_ref, llo_ref, lhi_ref, q_s, k_s, v_s, *, dil, ls):
    lo, head0 = _lane_masks()
    heads = (head0, jnp.logical_not(head0))
    cos = cos_ref[...]
    sin = sin_ref[...]
    pad = DIL_RADIUS
    qb = 128
    nblk = ls // qb
    ngroup = 4
    zeros = jnp.zeros((dil, pad, 128), BF16)
    k_s[:, 0:pad, :] = zeros
    k_s[:, pad + ls:pad + ls + pad, :] = zeros
    v_s[:, 0:pad, :] = zeros
    v_s[:, pad + ls:pad + ls + pad, :] = zeros
    for hp, (o_ref, l_ref) in enumerate(((olo_ref, llo_ref), (ohi_ref, lhi_ref))):
        q = qkv_ref[:, :, hp * 128:(hp + 1) * 128].astype(F32).reshape(dil * ls, 128)
        k = qkv_ref[:, :, 256 + hp * 128:256 + (hp + 1) * 128].astype(F32).reshape(dil * ls, 128)
        q_s[...] = (_rope2(q, cos, sin, lo) * (HEAD_DIM ** -0.5)).astype(BF16)
        k_s[:, pad:pad + ls, :] = _rope2(k, cos, sin, lo).astype(BF16).reshape(dil, ls, 128)
        v_s[:, pad:pad + ls, :] = qkv_ref[:, :, 512 + hp * 128:512 + (hp + 1) * 128]

        def grp(gi, carry):
            tiles = []
            for t in range(ngroup):
                c = gi * ngroup + t
                r = c // nblk
                q0 = pl.multiple_of((c % nblk) * qb, qb)
                qblk = q_s[pl.ds(pl.multiple_of(c * qb, qb), qb), :]
                kw = k_s[r, pl.ds(q0, qb + 2 * pad), :]
                vw = v_s[r, pl.ds(q0, qb + 2 * pad), :]
                mask = _band_mask(q0, qb, qb + 2 * pad, pad, DIL_RADIUS, ls)
                ss = [lax.dot_general(jnp.where(hm, qblk, jnp.zeros_like(qblk)), kw, (((1,), (1,)), ((), ())),
                                      preferred_element_type=F32) for hm in heads]
                tiles.append((r, q0, vw, mask, ss))
            soft = []
            for r, q0, vw, mask, ss in tiles:
                for s in ss:
                    s = jnp.where(mask, s, NEG_INF)
                    m = jnp.max(s, axis=-1, keepdims=True)
                    p = jnp.exp(s - m)
                    soft.append((p.astype(BF16), jnp.sum(p, axis=-1, keepdims=True), m))
            for ti, (r, q0, vw, mask, ss) in enumerate(tiles):
                res = []
                for h in range(2):
                    p, den, m = soft[2 * ti + h]
                    res.append((jnp.dot(p, vw, preferred_element_type=F32) / den, m + jnp.log(den)))
                o = jnp.where(head0, res[0][0], res[1][0])
                lse = jnp.where(head0, res[0][1], res[1][1])
                rows = pl.ds(q0, qb) if dil == 1 else pl.ds(q0 * dil + r, qb, stride=dil)
                o_ref[rows, :] = o
                l_ref[rows, :] = lse
            return carry

        lax.fori_loop(0, dil * nblk // ngroup, grp, 0)


def dilated_attention(qkv, col_block, cos_d, sin_d):
    b, dil, ls, _ = qkv.shape
    s = dil * ls
    out_sds = jax.ShapeDtypeStruct((b, s, 128), F32)
    out_spec = pl.BlockSpec((None, s, 128), lambda bi: (bi, 0, 0))
    return pl.pallas_call(
        functools.partial(_dil_attn_kernel, dil=dil, ls=ls),
        out_shape=(out_sds,) * 4,
        grid=(b,),
        in_specs=[
            pl.BlockSpec((None, dil, ls, QKV_W), lambda bi: (bi, 0, 0, col_block)),
            pl.BlockSpec((s, 128), lambda bi: (0, 0)),
            pl.BlockSpec((s, 128), lambda bi: (0, 0)),
        ],
        out_specs=(out_spec,) * 4,
        scratch_shapes=[
            pltpu.VMEM((s, 128), BF16),
            pltpu.VMEM((dil, ls + 2 * DIL_RADIUS, 128), BF16),
            pltpu.VMEM((dil, ls + 2 * DIL_RADIUS, 128), BF16),
        ],
        compiler_params=_cparams(("parallel",)),
        name=f"dil_attn_d{dil}",
    )(qkv, cos_d, sin_d)


def _swa_kernel(qkv_ref, cos_ref, sin_ref, sink_ref, o_ref, q_s, k_s, v_s, *, s_len):
    lo, head0 = _lane_masks()
    heads = (head0, jnp.logical_not(head0))
    cos = cos_ref[...]
    sin = sin_ref[...]
    pad = SWA_RADIUS
    qb = 128
    grp = SWA_Q_HEADS // SWA_KV_HEADS
    zeros = jnp.zeros((pad, 128), BF16)
    k_s[0:pad, :] = zeros
    k_s[pad + s_len:pad + s_len + pad, :] = zeros
    v_s[0:pad, :] = zeros
    v_s[pad + s_len:pad + s_len + pad, :] = zeros
    k = qkv_ref[:, 512:640].astype(F32)
    k_s[pad:pad + s_len, :] = _rope2(k, cos, sin, lo).astype(BF16)
    v_s[pad:pad + s_len, :] = qkv_ref[:, 640:768]
    for qp in range(grp):
        q = qkv_ref[:, qp * 128:(qp + 1) * 128].astype(F32)
        q_s[qp] = (_rope2(q, cos, sin, lo) * (HEAD_DIM ** -0.5)).astype(BF16)
    sinks = [jnp.concatenate([jnp.broadcast_to(sink_ref[kv * grp + p:kv * grp + p + 1, 0:1], (qb, 1))
                              for p in range(grp)], axis=0) for kv in range(SWA_KV_HEADS)]

    def blk(i, carry):
        q0 = pl.multiple_of(i * qb, qb)
        kw = k_s[pl.ds(q0, qb + 2 * pad), :]
        vw = v_s[pl.ds(q0, qb + 2 * pad), :]
        qall = jnp.concatenate([q_s[p, pl.ds(q0, qb), :] for p in range(grp)], axis=0)
        mask = _band_mask(q0, grp * qb, qb + 2 * pad, pad, SWA_RADIUS, s_len)
        ss = [lax.dot_general(jnp.where(hm, qall, jnp.zeros_like(qall)), kw, (((1,), (1,)), ((), ())),
                              preferred_element_type=F32) for hm in heads]
        soft = []
        for s, sk in zip(ss, sinks):
            s = jnp.where(mask, s, NEG_INF)
            m = jnp.maximum(jnp.max(s, axis=-1, keepdims=True), sk)
            p = jnp.exp(s - m)
            soft.append((p.astype(BF16), jnp.sum(p, axis=-1, keepdims=True) + jnp.exp(sk - m)))
        res = [jnp.dot(p, vw, preferred_element_type=F32) / den for p, den in soft]
        o = jnp.where(head0, res[0], res[1]).astype(o_ref.dtype)
        for p in range(grp):
            o_ref[pl.ds(q0, qb), p * 128:(p + 1) * 128] = o[p * qb:(p + 1) * qb, :]
        return carry

    lax.fori_loop(0, s_len // qb, blk, 0)


def swa_attention(proj, cos_t, sin_t, sink):
    b, s, n = proj.shape
    return pl.pallas_call(
        functools.partial(_swa_kernel, s_len=s),
        out_shape=jax.ShapeDtypeStruct((b, s, 512), BF16),
        grid=(b,),
        in_specs=[
            pl.BlockSpec((None, s, QKV_W), lambda bi: (bi, 0, COL_SWA // QKV_W)),
            pl.BlockSpec((s, 128), lambda bi: (0, 0)),
            pl.BlockSpec((s, 128), lambda bi: (0, 0)),
            pl.BlockSpec((8, 128), lambda bi: (0, 0)),
        ],
        out_specs=pl.BlockSpec((None, s, 512), lambda bi: (bi, 0, 0)),
        scratch_shapes=[
            pltpu.VMEM((SWA_Q_HEADS // SWA_KV_HEADS, s, 128), BF16),
            pltpu.VMEM((s + 2 * SWA_RADIUS, 128), BF16),
            pltpu.VMEM((s + 2 * SWA_RADIUS, 128), BF16),
        ],
        compiler_params=_cparams(("parallel",)),
        name="swa_attn",
    )(proj, cos_t, sin_t, jnp.broadcast_to(sink.astype(F32)[:, None], (8, 128)))


HY_CHUNKS = 4


def dft_tables(seq):
    kc = seq // HY_CHUNKS
    k = jnp.arange(seq, dtype=jnp.int32)
    phase = ((2 * k[:, None] + 1) * k[None, :]) % (4 * seq)
    ang = phase.astype(F32) * (2.0 * math.pi / (4 * seq))
    c = jnp.cos(ang).reshape(HY_CHUNKS, kc, seq)
    s = jnp.sin(ang).reshape(HY_CHUNKS, kc, seq)
    fwd = jnp.concatenate([c, s], axis=1).astype(BF16)
    inv = (jnp.concatenate([c, -s], axis=1) * (1.0 / seq)).transpose(0, 2, 1).astype(BF16)
    return fwd, inv


def _hy_filter_kernel(w1_ref, b1_ref, w2_ref, b2_ref, w3_ref, fr_ref, ld_ref, bands_ref, f_ref,
                      hr_ref, hi_ref, hs_s, hd_s, *, seq):
    hw = HYENA_WIDTH
    hi_p = lax.Precision.HIGHEST

    @pl.when(pl.program_id(2) == 0)
    def _():
        idx = lax.broadcasted_iota(jnp.int32, (seq, 1), 0).astype(F32)
        t = idx / max(seq - 1, 1)
        w = 2.0 * math.pi * idx / seq
        ang = w * bands_ref[...]
        w1 = w1_ref[...]
        pre = (t * w1[0:1, :]
               + jnp.dot(jnp.cos(ang), w1[1:1 + HYENA_BANDS, :], precision=hi_p, preferred_element_type=F32)
               - jnp.dot(jnp.sin(ang), w1[1 + HYENA_BANDS:1 + 2 * HYENA_BANDS, :], precision=hi_p,
                         preferred_element_type=F32)
               + b1_ref[...])
        h = jnp.sin(fr_ref[0:1, :] * pre)
        h = jnp.sin(fr_ref[1:2, :] * (jnp.dot(h, w2_ref[...], precision=hi_p, preferred_element_type=F32)
                                      + b2_ref[...]))
        h = jnp.dot(h, w3_ref[...], precision=hi_p, preferred_element_type=F32)
        h = h * jnp.exp(-t * jnp.exp(ld_ref[...]))
        hf = h[:, :hw]
        hb = jnp.where(idx > 0.0, h[:, hw:], 0.0)
        inv = lax.rsqrt(jnp.sum(hf * hf + hb * hb, axis=0, keepdims=True) + 1e-12)
        hs_s[...] = ((hf + hb) * inv).astype(BF16)
        hd_s[...] = ((hf - hb) * inv).astype(BF16)

    kc = seq // HY_CHUNKS
    hr_ref[...] = jnp.dot(f_ref[0:kc, :], hs_s[...], preferred_element_type=F32)
    hi_ref[...] = -jnp.dot(f_ref[kc:2 * kc, :], hd_s[...], preferred_element_type=F32)


def hyena_filters(w1, b1, w2, b2, w3p, freq, ldp, fwd, seq):
    depth = w1.shape[0]
    kc = seq // HY_CHUNKS
    hw = HYENA_WIDTH
    bands = jnp.linspace(1e-4, HYENA_BANDS - 1, HYENA_BANDS, dtype=F32).reshape(1, HYENA_BANDS)
    full = lambda *shape: pl.BlockSpec((None,) + shape, lambda l, o, c: (l,) + (0,) * len(shape))
    out_sds = jax.ShapeDtypeStruct((depth, 2, seq, hw), F32)
    return pl.pallas_call(
        functools.partial(_hy_filter_kernel, seq=seq),
        out_shape=(out_sds, out_sds),
        grid=(depth, 2, HY_CHUNKS),
        in_specs=[
            full(*w1.shape[1:]), full(1, b1.shape[-1]), full(*w2.shape[1:]), full(1, b2.shape[-1]),
            pl.BlockSpec((None, w3p.shape[1], 2 * hw), lambda l, o, c: (l, 0, o)),
            full(*freq.shape[1:]),
            pl.BlockSpec((None, 1, 2 * hw), lambda l, o, c: (l, 0, o)),
            pl.BlockSpec((1, HYENA_BANDS), lambda l, o, c: (0, 0)),
            pl.BlockSpec((None, 2 * kc, seq), lambda l, o, c: (c, 0, 0)),
        ],
        out_specs=(
            pl.BlockSpec((None, None, kc, hw), lambda l, o, c: (l, o, c, 0)),
            pl.BlockSpec((None, None, kc, hw), lambda l, o, c: (l, o, c, 0)),
        ),
        scratch_shapes=[pltpu.VMEM((seq, hw), BF16), pltpu.VMEM((seq, hw), BF16)],
        compiler_params=_cparams(("arbitrary", "arbitrary", "arbitrary")),
        name="hyena_filters",
    )(w1, b1[:, None, :], w2, b2[:, None, :], w3p, freq, ldp[:, None, :], bands, fwd)


def _hyena_kernel(hy_ref, cw_ref, cb_ref, bias_ref, hr_ref, hi_ref, f_ref, g_ref, o_ref, zb_s, zf_s, acc_s, *, seq):
    hw = HYENA_WIDTH
    o = pl.program_id(1)
    c = pl.program_id(2)
    kc = seq // HY_CHUNKS

    def short_conv(part):
        x = hy_ref[:, part * hw:(part + 1) * hw].astype(F32)
        row = lax.broadcasted_iota(jnp.int32, (seq, 1), 0)
        xm = jnp.where(row == 0, 0.0, pltpu.roll(x, 1, 0))
        xp = jnp.where(row == seq - 1, 0.0, pltpu.roll(x, seq - 1, 0))
        w = cw_ref[:, part * hw:(part + 1) * hw]
        return cb_ref[:, part * hw:(part + 1) * hw] + xm * w[0:1, :] + x * w[1:2, :] + xp * w[2:3, :]

    @pl.when((o == 0) & (c == 0))
    def _():
        z0 = short_conv(0)
        zf_s[...] = z0
        zb_s[...] = z0.astype(BF16)

    zz = jnp.dot(f_ref[...], zb_s[...], preferred_element_type=F32)
    zc = zz[:kc]
    zs = zz[kc:]
    hr = hr_ref[...]
    hi = hi_ref[...]
    yr = zc * hr + zs * hi
    yi = zc * hi - zs * hr
    yy = jnp.concatenate([yr, yi], axis=0).astype(BF16)
    part = jnp.dot(g_ref[...], yy, preferred_element_type=F32)

    @pl.when(c == 0)
    def _():
        acc_s[...] = part

    @pl.when(c > 0)
    def _():
        acc_s[...] += part

    @pl.when((c == HY_CHUNKS - 1) & (o == 0))
    def _():
        z1 = short_conv(1) * (acc_s[...] + bias_ref[0:1, :] * zf_s[...])
        zf_s[...] = z1
        zb_s[...] = z1.astype(BF16)

    @pl.when((c == HY_CHUNKS - 1) & (o == 1))
    def _():
        o_ref[...] = (short_conv(2) * (acc_s[...] + bias_ref[1:2, :] * zf_s[...])).astype(o_ref.dtype)


def hyena_mixer(proj, conv_w, conv_b, hy_bias, hr, hi, fwd, inv):
    b, s, n = proj.shape
    hw = HYENA_WIDTH
    kc = s // HY_CHUNKS
    return pl.pallas_call(
        functools.partial(_hyena_kernel, seq=s),
        out_shape=jax.ShapeDtypeStruct((b, s, hw), BF16),
        grid=(b, 2, HY_CHUNKS),
        in_specs=[
            pl.BlockSpec((None, s, 3 * hw), lambda bi, o, c: (bi, 0, COL_HY // (3 * hw))),
            pl.BlockSpec((3, 3 * hw), lambda bi, o, c: (0, 0)),
            pl.BlockSpec((1, 3 * hw), lambda bi, o, c: (0, 0)),
            pl.BlockSpec((2, hw), lambda bi, o, c: (0, 0)),
            pl.BlockSpec((None, kc, hw), lambda bi, o, c: (o, c, 0)),
            pl.BlockSpec((None, kc, hw), lambda bi, o, c: (o, c, 0)),
            pl.BlockSpec((None, 2 * kc, s), lambda bi, o, c: (c, 0, 0)),
            pl.BlockSpec((None, s, 2 * kc), lambda bi, o, c: (c, 0, 0)),
        ],
        out_specs=pl.BlockSpec((None, s, hw), lambda bi, o, c: (bi, 0, 0)),
        scratch_shapes=[pltpu.VMEM((s, hw), BF16), pltpu.VMEM((s, hw), F32), pltpu.VMEM((s, hw), F32)],
        compiler_params=_cparams(("parallel", "arbitrary", "arbitrary")),
        name="hyena_conv",
    )(proj, conv_w, conv_b.reshape(1, -1), hy_bias, hr, hi, fwd, inv)


def _layer_norm(y, g, b):
    mu = jnp.mean(y, axis=-1, keepdims=True)
    yc = y - mu
    var = jnp.mean(yc * yc, axis=-1, keepdims=True)
    return yc * lax.rsqrt(var + LN_EPS) * g + b


def _merge_kernel(*refs):
    att = refs[:12]
    (yb_ref, yc_ref, gl_ref, x_ref, g1_ref, sc2_ref, sh2_ref, lng_ref, lnb_ref, wa_ref, wb_ref, wc_ref, wo_ref,
     xo_ref, u2_ref) = refs[12:]
    d = D_MODEL
    halves = []
    for half in range(2):
        la, lb, lc = (att[4 * g + 2 + half][...] for g in range(3))
        m = jnp.maximum(jnp.maximum(la, lb), lc)
        ea, eb, ec = jnp.exp(la - m), jnp.exp(lb - m), jnp.exp(lc - m)
        inv = 1.0 / (ea + eb + ec)
        halves.append((ea * inv) * att[half][...] + (eb * inv) * att[4 + half][...] + (ec * inv) * att[8 + half][...])
    ya = jnp.concatenate(halves, axis=-1)
    za = jnp.dot(ya.astype(BF16), wa_ref[...], preferred_element_type=F32)
    zb = jnp.dot(yb_ref[...], wb_ref[...], preferred_element_type=F32)
    zc = jnp.dot(yc_ref[...], wc_ref[...], preferred_element_type=F32)
    merged = (jax.nn.sigmoid(gl_ref[:, 0:d].astype(F32)) * za
              + jax.nn.sigmoid(gl_ref[:, d:2 * d].astype(F32)) * zb
              + jax.nn.sigmoid(gl_ref[:, 2 * d:3 * d].astype(F32)) * zc)
    mix = jnp.dot(merged.astype(BF16), wo_ref[...], preferred_element_type=F32)
    xn = _layer_norm(ALPHA * x_ref[...] + g1_ref[...] * mix, lng_ref[...], lnb_ref[...])
    xo_ref[...] = xn
    u2_ref[...] = (xn * (1.0 + sc2_ref[...]) + sh2_ref[...]).T.astype(BF16)


def merge_mixers(att, yb, yc, proj, x, g1, sc2, sh2, ln_g, ln_b, wa, wb, wc, wo):
    b, s, d = x.shape
    tm = 512
    row = lambda w: pl.BlockSpec((None, tm, w), lambda bi, i: (bi, i, 0))
    per_b = pl.BlockSpec((None, 1, d), lambda bi, i: (bi, 0, 0))
    const = lambda r, c: pl.BlockSpec((r, c), lambda bi, i: (0, 0))
    return pl.pallas_call(
        _merge_kernel,
        out_shape=(jax.ShapeDtypeStruct((b, s, d), F32), jax.ShapeDtypeStruct((d, b * s), BF16)),
        grid=(b, s // tm),
        in_specs=[row(128)] * 12 + [row(512), row(512), row(3 * d), row(d), per_b, per_b, per_b,
                                    const(1, d), const(1, d), const(256, d), const(512, d), const(512, d),
                                    const(d, d)],
        out_specs=(row(d), pl.BlockSpec((d, tm), lambda bi, i: (0, bi * (s // tm) + i))),
        compiler_params=_cparams(("parallel", "parallel")),
        name="merge_mixers",
    )(*att, yb, yc, proj, x, g1, sc2, sh2, ln_g.reshape(1, d), ln_b.reshape(1, d), wa, wb, wc, wo)


_PEER_PAIRS = [(i, j) for i in range(1, PEER_TOPK + 1) for j in range(1, PEER_TOPK + 1) if i * j <= PEER_TOPK]
PEER_NOT_TOP = 100.0


def _peer_route_kernel(u_ref, wq_ref, keys_ref, e1_ref, cut_ref, e2_ref, r2_ref, q_s, s_s, ab_s, st_s, *, tr):
    nh, nk, topk = PEER_HEADS, PEER_KEYS, PEER_TOPK
    qt = jnp.dot(wq_ref[...], u_ref[...], preferred_element_type=F32)
    q_s[...] = qt.astype(BF16)
    for ph in range(2 * nh):
        s_s[ph] = jnp.dot(keys_ref[ph], q_s[ph * 128:(ph + 1) * 128, :], preferred_element_type=F32)

    def extract(ph, carry):
        p = ph // nh
        h = ph % nh

        def rnd(r, prev):
            sv = s_s[ph]
            cur = jnp.max(jnp.where(sv < prev, sv, -jnp.inf), axis=0, keepdims=True)
            ab_s[p, r, pl.ds(h, 1), :] = cur
            return cur

        lax.fori_loop(0, topk, rnd, jnp.full((1, tr), jnp.inf, F32))
        return carry

    lax.fori_loop(0, 2 * nh, extract, 0)

    for ch in range(tr // 128):
        ln = slice(ch * 128, (ch + 1) * 128)
        a = [ab_s[0, r, :, ln] for r in range(topk)]
        b = [ab_s[1, r, :, ln] for r in range(topk)]
        cand = [a[i - 1] + b[j - 1] for (i, j) in _PEER_PAIRS]
        tau = jnp.full((nh, 128), -jnp.inf, F32)
        for x, (ix, jx) in enumerate(_PEER_PAIRS):
            cnt = jnp.zeros((nh, 128), F32)
            for y, (iy, jy) in enumerate(_PEER_PAIRS):
                if iy <= ix and jy <= jx:
                    cnt = cnt + 1.0
                elif iy >= ix and jy >= jx:
                    continue
                else:
                    cnt = cnt + jnp.where(cand[y] >= cand[x], 1.0, 0.0)
            tau = jnp.maximum(tau, jnp.where(cnt >= float(topk), cand[x], -jnp.inf))
        top = cand[0]
        z = jnp.zeros((nh, 128), F32)
        for cx in cand:
            z = z + jnp.where(cx >= tau, jnp.exp(cx - top), 0.0)
        st_s[0, :, ln] = tau
        st_s[1, :, ln] = 1.0 / z

    for h in range(nh):
        rows = slice(h * nk, (h + 1) * nk)
        s1 = s_s[h]
        s2 = s_s[nh + h]
        hrow = slice(h, h + 1)
        tau = st_s[0, hrow, :]
        in1 = s1 >= ab_s[0, topk - 1, hrow, :]
        in2 = s2 >= ab_s[1, topk - 1, hrow, :]
        cut = jnp.zeros_like(s1)
        rank = jnp.ones_like(s2)
        for r in range(topk):
            br = ab_s[1, r, hrow, :]
            cut = cut + jnp.where(s1 + br >= tau, 1.0, 0.0)
            rank = rank + jnp.where(br > s2, 1.0, 0.0)
        e1_ref[rows, :] = jnp.where(in1, jnp.exp(s1 - ab_s[0, 0, hrow, :]), 0.0)
        cut_ref[rows, :] = jnp.where(in1, cut, 0.0)
        e2_ref[rows, :] = jnp.where(in2, jnp.exp(s2 - ab_s[1, 0, hrow, :]) * st_s[1, hrow, :], 0.0).astype(BF16)
        r2_ref[rows, :] = jnp.where(in2, rank, PEER_NOT_TOP).astype(BF16)


def peer_route(u2, wq_t, keys):
    d, t = u2.shape
    tr = 512
    nrow = PEER_HEADS * PEER_KEYS
    tab = lambda dt: jax.ShapeDtypeStruct((nrow, t), dt)
    col = pl.BlockSpec((nrow, tr), lambda i: (0, i))
    return pl.pallas_call(
        functools.partial(_peer_route_kernel, tr=tr),
        out_shape=(tab(F32), tab(F32), tab(BF16), tab(BF16)),
        grid=(t // tr,),
        in_specs=[
            pl.BlockSpec((d, tr), lambda i: (0, i)),
            pl.BlockSpec(wq_t.shape, lambda i: (0, 0)),
            pl.BlockSpec(keys.shape, lambda i: (0, 0, 0)),
        ],
        out_specs=(col, col, col, col),
        scratch_shapes=[
            pltpu.VMEM((2 * nrow, tr), BF16),
            pltpu.VMEM((2 * PEER_HEADS, PEER_KEYS, tr), F32),
            pltpu.VMEM((2, PEER_TOPK, PEER_HEADS, tr), F32),
            pltpu.VMEM((2, PEER_HEADS, tr), F32),
        ],
        compiler_params=_cparams(("parallel",)),
        name="peer_route",
    )(u2, wq_t, keys)


def _peer_dense_kernel(x_ref, u_ref, vt_ref, e1_ref, cut_ref, e2_ref, r2_ref, o_ref, acc_s, wga_s, wgb_s, *, te, n_e):
    j = pl.program_id(1)
    nk = PEER_KEYS
    tn = x_ref.shape[1]
    sub = 16

    @pl.when(j == 0)
    def _():
        acc_s[...] = jnp.zeros_like(acc_s)
        wgb_s[...] = jnp.zeros_like(wgb_s)

    def step(w_cur, w_prev):
        jj = jnp.minimum(j, n_e - 1)
        mt = 128
        nsub = te // mt
        macc = acc_s.shape[0] // nsub
        for sb in range(nsub):
            rows = slice(sb * mt, (sb + 1) * mt)
            at = jnp.dot(u_ref[rows, :], x_ref[...], preferred_element_type=F32).astype(BF16)
            for mrow in range(sb * macc, (sb + 1) * macc, mt):
                acc_s[mrow:mrow + mt, :] += jnp.dot(vt_ref[mrow:mrow + mt, :], w_prev[...],
                                                    preferred_element_type=F32)
            g = (0.5 * at) * (1.0 + lax.erf(at * (2.0 ** -0.5)))
            for iq in range(mt // nk):
                ii = sb * (mt // nk) + iq
                i = jj * (te // nk) + ii
                w = None
                for h in range(PEER_HEADS):
                    cb = jnp.broadcast_to(cut_ref[pl.ds(h * nk + i, 1), :], (sub, tn)).astype(BF16)[None]
                    e1 = jnp.broadcast_to(e1_ref[pl.ds(h * nk + i, 1), :], (sub, tn)).astype(BF16)[None]
                    r2 = r2_ref[h * nk:(h + 1) * nk, :].reshape(nk // sub, sub, tn)
                    e2 = e2_ref[h * nk:(h + 1) * nk, :].reshape(nk // sub, sub, tn)
                    sel = jnp.where(r2 <= cb, e2, jnp.zeros((), BF16)) * e1
                    w = sel if w is None else w + sel
                w_cur[ii * nk:(ii + 1) * nk, :] = g[iq * nk:(iq + 1) * nk, :] * w.reshape(nk, tn)

    @pl.when((j % 2 == 0) & (j < n_e))
    def _():
        step(wga_s, wgb_s)

    @pl.when(j % 2 == 1)
    def _():
        step(wgb_s, wga_s)

    @pl.when(j == n_e)
    def _():
        o_ref[...] = (acc_s[...] + jnp.dot(vt_ref[...], wgb_s[...], preferred_element_type=F32)).T


def peer_dense(u2, u_tab, vt_tab, e1, cut, e2, r2):
    d, t = u2.shape
    ne = u_tab.shape[0]
    tn, te = 1024, 512
    n_e = ne // te
    nrow = PEER_HEADS * PEER_KEYS
    col = pl.BlockSpec((nrow, tn), lambda i, j: (0, i))
    return pl.pallas_call(
        functools.partial(_peer_dense_kernel, te=te, n_e=n_e),
        out_shape=jax.ShapeDtypeStruct((t, d), F32),
        grid=(t // tn, n_e + 1),
        in_specs=[
            pl.BlockSpec((d, tn), lambda i, j: (0, i)),
            pl.BlockSpec((te, d), lambda i, j: (jnp.minimum(j, n_e - 1), 0)),
            pl.BlockSpec((d, te), lambda i, j: (0, jnp.maximum(j - 1, 0))),
            col, col, col, col,
        ],
        out_specs=pl.BlockSpec((tn, d), lambda i, j: (i, 0)),
        scratch_shapes=[pltpu.VMEM((d, tn), F32), pltpu.VMEM((te, tn), BF16), pltpu.VMEM((te, tn), BF16)],
        compiler_params=_cparams(("parallel", "arbitrary")),
        name="peer_dense",
    )(u2, u_tab, vt_tab, e1, cut, e2, r2)


def _peer_out_kernel(x_ref, f_ref, g2_ref, lng_ref, lnb_ref, o_ref):
    o_ref[...] = _layer_norm(ALPHA * x_ref[...] + g2_ref[...] * f_ref[...], lng_ref[...], lnb_ref[...])


def peer_residual(x, ffn, g2, ln_g, ln_b):
    b, s, d = x.shape
    tm = 1024
    row = pl.BlockSpec((None, tm, d), lambda bi, i: (bi, i, 0))
    const = pl.BlockSpec((1, d), lambda bi, i: (0, 0))
    return pl.pallas_call(
        _peer_out_kernel,
        out_shape=jax.ShapeDtypeStruct((b, s, d), F32),
        grid=(b, s // tm),
        in_specs=[row, row, pl.BlockSpec((None, 1, d), lambda bi, i: (bi, 0, 0)), const, const],
        out_specs=row,
        compiler_params=_cparams(("parallel", "parallel")),
        name="peer_residual",
    )(x, ffn.reshape(b, s, d), g2, ln_g.reshape(1, d), ln_b.reshape(1, d))


def rope_tables(seq):
    pos = jnp.arange(seq, dtype=F32)
    inv = ROPE_THETA ** (-jnp.arange(0, HEAD_DIM, 2, dtype=F32) / HEAD_DIM)
    ang = pos[:, None] * inv[None, :]
    sign = jnp.where((jnp.arange(128) % 64) < 32, -1.0, 1.0).astype(F32)
    return jnp.tile(jnp.cos(ang), (1, 4)), jnp.tile(jnp.sin(ang), (1, 4)) * sign[None, :]


def _swa_head_perm():
    heads = [h for p in range(4) for h in (p, 4 + p)]
    return jnp.concatenate([jnp.arange(h * HEAD_DIM, (h + 1) * HEAD_DIM) for h in heads])


def relayout_w_in(w):
    qa, ka, va = w[:, 0:768], w[:, 768:1536], w[:, 1536:2304]
    hy = w[:, 2304:3840]
    qc, kvc = w[:, 3840:4352], w[:, 4352:4608]
    gates = w[:, 4608:7680]
    dil = [jnp.concatenate([t[:, 256 * g:256 * (g + 1)] for t in (qa, ka, va)], axis=1) for g in range(3)]
    main = jnp.concatenate([gates, hy, dil[0], qc[:, _swa_head_perm()], kvc], axis=1).astype(BF16)
    return main, jnp.concatenate(dil[1:], axis=1).astype(BF16)


def residue_major(tab, dil):
    s = tab.shape[0]
    return tab.reshape(s // dil, dil, 128).transpose(1, 0, 2).reshape(s, 128)


def token_mixer(x, ada, w_main, w_dil, conv_w, conv_b, hy_bias, attn_sink, hr, hi, fwd, inv, rope_tabs,
                wa, wb, wc, wo, ln_g, ln_b):
    b, s, _ = x.shape
    sh1, sc1, g1, sh2, sc2, _ = [a[:, None, :] for a in jnp.split(ada, 6, axis=-1)]
    proj = in_proj(x, sc1, sh1, w_main)
    qkv1, qkv2 = in_proj_dil(x, sc1, sh1, w_dil)
    att = (dilated_attention(proj.reshape(b, 1, s, MAIN_WIDTH), COL_DIL0 // QKV_W, *rope_tabs[0])
           + dilated_attention(qkv1, 0, *rope_tabs[1]) + dilated_attention(qkv2, 0, *rope_tabs[2]))
    yb = hyena_mixer(proj, conv_w, conv_b, hy_bias, hr, hi, fwd, inv)
    yc = swa_attention(proj, *rope_tabs[0], attn_sink)
    return merge_mixers(att, yb, yc, proj, x, g1, sc2, sh2, ln_g, ln_b, wa, wb, wc, wo)


def peer_ffn(u2, wq, keys, u_tab, v_tab):
    d = u2.shape[0]
    nh, nk = PEER_HEADS, PEER_KEYS
    wq_t = wq.reshape(d, nh, 2, nk).transpose(2, 1, 3, 0).reshape(2 * nh * nk, d).astype(BF16)
    keys_r = keys.transpose(1, 0, 2, 3).reshape(2 * nh, nk, keys.shape[-1]).astype(BF16)
    tabs = peer_route(u2, wq_t, keys_r)
    return peer_dense(u2, u_tab.astype(BF16), v_tab.T.astype(BF16), *tabs)


def kernel(x, c, w_ada, b_ada, w_in, conv_w, conv_b, hy_w1, hy_b1, hy_w2, hy_b2, hy_w3, hy_freq, hy_log_decay,
           hy_bias, attn_sink, w_branch_a, w_branch_b, w_branch_c, w_out, ln_g, ln_b, peer_wq, peer_keys, peer_u,
           peer_v):
    b, s, d = x.shape
    depth = w_in.shape[0]
    hw = HYENA_WIDTH
    cos_t, sin_t = rope_tables(s)
    fwd, inv = dft_tables(s)
    ada = ada_ln(c, w_ada, b_ada)
    w3p = hy_w3.reshape(depth, -1, 2, 2, hw).transpose(0, 1, 3, 2, 4).reshape(depth, -1, 4 * hw)
    ldp = hy_log_decay.reshape(depth, 2, 2, hw).transpose(0, 2, 1, 3).reshape(depth, 4 * hw)
    hr, hi = hyena_filters(hy_w1, hy_b1, hy_w2, hy_b2, w3p, hy_freq, ldp, fwd, s)
    perm = _swa_head_perm()
    rope_tabs = [(residue_major(cos_t, dil), residue_major(sin_t, dil)) for _, dil in DIL_GROUPS]
    for l in range(depth):
        x1, u2 = token_mixer(x, ada[l], *relayout_w_in(w_in[l]), conv_w[l], conv_b[l], hy_bias[l], attn_sink[l],
                             hr[l], hi[l], fwd, inv, rope_tabs, w_branch_a[l].astype(BF16),
                             w_branch_b[l].astype(BF16), w_branch_c[l][perm].astype(BF16), w_out[l].astype(BF16),
                             ln_g[l, 0], ln_b[l, 0])
        ffn = peer_ffn(u2, peer_wq[l], peer_keys[l], peer_u[l], peer_v[l])
        x = peer_residual(x1, ffn, ada[l][:, None, 5 * d:6 * d], ln_g[l, 1], ln_b[l, 1])
    return x
```

```python
import functools
import math

import jax
import jax.numpy as jnp
from jax import lax
from jax.experimental import pallas as pl
from jax.experimental.pallas import tpu as pltpu

F32 = jnp.float32
BF16 = jnp.bfloat16

D_MODEL = 1024
HEAD_DIM = 64
DIL_GROUPS = ((128, 1), (512, 4), (2048, 16))
DIL_RADIUS = 64
HYENA_WIDTH = 512
HYENA_BANDS = 16
SWA_RADIUS = 128
SWA_Q_HEADS = 8
SWA_KV_HEADS = 2
PEER_HEADS = 8
PEER_KEYS = 128
PEER_TOPK = 16
DEPTH = 2
ALPHA = (2 * DEPTH) ** 0.25
LN_EPS = 1e-5
NEG_INF = -1e30
ROPE_THETA = 10000.0

COL_GATES = 0
COL_HY = 3072
COL_DIL0 = 4608
COL_SWA = 5376
MAIN_WIDTH = 6144
QKV_W = 768

VMEM_LIMIT = 56 * 1024 * 1024


def _cparams(sem, vmem=VMEM_LIMIT):
    return pltpu.CompilerParams(dimension_semantics=sem, vmem_limit_bytes=vmem)


def _ada_kernel(c_ref, w_ref, b_ref, o_ref):
    c = c_ref[...]
    cond = (c * jax.nn.sigmoid(c)).astype(BF16)
    o_ref[...] = jnp.dot(cond, w_ref[...].astype(BF16), preferred_element_type=F32) + b_ref[...]


def ada_ln(c, w_ada, b_ada):
    depth, d, n = w_ada.shape
    b = c.shape[0]
    tn = 1536
    return pl.pallas_call(
        _ada_kernel,
        out_shape=jax.ShapeDtypeStruct((depth, b, n), F32),
        grid=(depth, n // tn),
        in_specs=[
            pl.BlockSpec((b, d), lambda l, j: (0, 0)),
            pl.BlockSpec((None, d, tn), lambda l, j: (l, 0, j)),
            pl.BlockSpec((None, 1, tn), lambda l, j: (l, 0, j)),
        ],
        out_specs=pl.BlockSpec((None, b, tn), lambda l, j: (l, 0, j)),
        compiler_params=_cparams(("arbitrary", "arbitrary")),
        name="ada_ln",
    )(c, w_ada, b_ada.reshape(depth, 1, n))


def _inproj_kernel(x_ref, sc_ref, sh_ref, w_ref, o_ref, u_ref):
    @pl.when(pl.program_id(2) == 0)
    def _():
        u_ref[...] = (x_ref[...] * (1.0 + sc_ref[...]) + sh_ref[...]).astype(BF16)

    o_ref[...] = jnp.dot(u_ref[...], w_ref[...], preferred_element_type=F32).astype(o_ref.dtype)


def in_proj(x, sc, sh, w):
    b, s, d = x.shape
    n = w.shape[1]
    tm, tn = 1024, 1536
    return pl.pallas_call(
        _inproj_kernel,
        out_shape=jax.ShapeDtypeStruct((b, s, n), BF16),
        grid=(b, s // tm, n // tn),
        in_specs=[
            pl.BlockSpec((None, tm, d), lambda bi, i, j: (bi, i, 0)),
            pl.BlockSpec((None, 1, d), lambda bi, i, j: (bi, 0, 0)),
            pl.BlockSpec((None, 1, d), lambda bi, i, j: (bi, 0, 0)),
            pl.BlockSpec((d, tn), lambda bi, i, j: (0, j)),
        ],
        out_specs=pl.BlockSpec((None, tm, tn), lambda bi, i, j: (bi, i, j)),
        scratch_shapes=[pltpu.VMEM((tm, d), BF16)],
        compiler_params=_cparams(("parallel", "parallel", "arbitrary")),
        name="in_proj",
    )(x, sc, sh, w)


def _inproj_dil_kernel(x_ref, sc_ref, sh_ref, w_ref, o1_ref, o2_ref, res_s):
    tm = x_ref.shape[0]
    u = (x_ref[...] * (1.0 + sc_ref[...]) + sh_ref[...]).astype(BF16)
    res = jnp.dot(u, w_ref[...], preferred_element_type=F32)
    nslab = QKV_W // 128
    for c in range(2 * nslab):
        res_s[c] = res[:, c * 128:(c + 1) * 128]
    for g, o_ref in ((0, o1_ref), (1, o2_ref)):
        dil = o_ref.shape[0]
        for r in range(dil):
            for c in range(nslab):
                o_ref[r, :, c * 128:(c + 1) * 128] = res_s[g * nslab + c, pl.ds(r, tm // dil, stride=dil), :].astype(
                    o_ref.dtype)


def in_proj_dil(x, sc, sh, w):
    b, s, d = x.shape
    tm = 1024
    d1, d2 = DIL_GROUPS[1][1], DIL_GROUPS[2][1]
    return pl.pallas_call(
        _inproj_dil_kernel,
        out_shape=(jax.ShapeDtypeStruct((b, d1, s // d1, QKV_W), BF16),
                   jax.ShapeDtypeStruct((b, d2, s // d2, QKV_W), BF16)),
        grid=(b, s // tm),
        in_specs=[
            pl.BlockSpec((None, tm, d), lambda bi, i: (bi, i, 0)),
            pl.BlockSpec((None, 1, d), lambda bi, i: (bi, 0, 0)),
            pl.BlockSpec((None, 1, d), lambda bi, i: (bi, 0, 0)),
            pl.BlockSpec((d, 2 * QKV_W), lambda bi, i: (0, 0)),
        ],
        out_specs=(
            pl.BlockSpec((None, d1, tm // d1, QKV_W), lambda bi, i: (bi, 0, i, 0)),
            pl.BlockSpec((None, d2, tm // d2, QKV_W), lambda bi, i: (bi, 0, i, 0)),
        ),
        scratch_shapes=[pltpu.VMEM((2 * QKV_W // 128, tm, 128), F32)],
        compiler_params=_cparams(("parallel", "parallel")),
        name="in_proj_dil",
    )(x, sc, sh, w)


def _rope2(x, cos, sin_signed, lo):
    xr = jnp.where(lo, pltpu.roll(x, 96, 1), pltpu.roll(x, 32, 1))
    return x * cos + xr * sin_signed


def _lane_masks():
    lane = lax.broadcasted_iota(jnp.int32, (1, 128), 1)
    return (lane % 64) < 32, lane < 64


def _band_mask(q0, nq, nk, pad, radius, length):
    qpos = q0 + (lax.broadcasted_iota(jnp.int32, (nq, 1), 0) & 127)
    kpos = q0 - pad + lax.broadcasted_iota(jnp.int32, (1, nk), 1)
    kpos = jnp.where(kpos < 0, -4 * length, jnp.where(kpos >= length, -4 * length, kpos))
    return jnp.abs(qpos - kpos) <= radius


def _dil_attn_kernel(qkv_ref, cos_ref, sin_ref, olo_ref, ohi_ref, llo_ref, lhi_ref, q_s, k_s, v_s, *, dil, ls):
    lo, head0 = _lane_masks()
    heads = (head0, jnp.logical_not(head0))
    cos = cos_ref[...]
    sin = sin_ref[...]
    pad = DIL_RADIUS
    qb = 128
    nblk = ls // qb
    ngroup = 4
    zeros = jnp.zeros((dil, pad, 128), BF16)
    k_s[:, 0:pad, :] = zeros
    k_s[:, pad + ls:pad + ls + pad, :] = zeros
    v_s[:, 0:pad, :] = zeros
    v_s[:, pad + ls:pad + ls + pad, :] = zeros
    for hp, (o_ref, l_ref) in enumerate(((olo_ref, llo_ref), (ohi_ref, lhi_ref))):
        q = qkv_ref[:, :, hp * 128:(hp + 1) * 128].astype(F32).reshape(dil * ls, 128)
        k = qkv_ref[:, :, 256 + hp * 128:256 + (hp + 1) * 128].astype(F32).reshape(dil * ls, 128)
        q_s[...] = (_rope2(q, cos, sin, lo) * (HEAD_DIM ** -0.5)).astype(BF16)
        k_s[:, pad:pad + ls, :] = _rope2(k, cos, sin, lo).astype(BF16).reshape(dil, ls, 128)
        v_s[:, pad:pad + ls, :] = qkv_ref[:, :, 512 + hp * 128:512 + (hp + 1) * 128]

        def grp(gi, carry):
            tiles = []
            for t in range(ngroup):
                c = gi * ngroup + t
                r = c // nblk
                q0 = pl.multiple_of((c % nblk) * qb, qb)
                qblk = q_s[pl.ds(pl.multiple_of(c * qb, qb), qb), :]
                kw = k_s[r, pl.ds(q0, qb + 2 * pad), :]
                vw = v_s[r, pl.ds(q0, qb + 2 * pad), :]
                mask = _band_mask(q0, qb, qb + 2 * pad, pad, DIL_RADIUS, ls)
                ss = [lax.dot_general(jnp.where(hm, qblk, jnp.zeros_like(qblk)), kw, (((1,), (1,)), ((), ())),
                                      preferred_element_type=F32) for hm in heads]
                tiles.append((r, q0, vw, mask, ss))
            soft = []
            for r, q0, vw, mask, ss in tiles:
                for s in ss:
                    s = jnp.where(mask, s, NEG_INF)
                    m = jnp.max(s, axis=-1, keepdims=True)
                    p = jnp.exp(s - m)
                    soft.append((p.astype(BF16), jnp.sum(p, axis=-1, keepdims=True), m))
            for ti, (r, q0, vw, mask, ss) in enumerate(tiles):
                res = []
                for h in range(2):
                    p, den, m = soft[2 * ti + h]
                    res.append((jnp.dot(p, vw, preferred_element_type=F32) / den, m + jnp.log(den)))
                o = jnp.where(head0, res[0][0], res[1][0])
                lse = jnp.where(head0, res[0][1], res[1][1])
                rows = pl.ds(q0, qb) if dil == 1 else pl.ds(q0 * dil + r, qb, stride=dil)
                o_ref[rows, :] = o
                l_ref[rows, :] = lse
            return carry

        lax.fori_loop(0, dil * nblk // ngroup, grp, 0)


def dilated_attention(qkv, col_block, cos_d, sin_d):
    b, dil, ls, _ = qkv.shape
    s = dil * ls
    out_sds = jax.ShapeDtypeStruct((b, s, 128), F32)
    out_spec = pl.BlockSpec((None, s, 128), lambda bi: (bi, 0, 0))
    return pl.pallas_call(
        functools.partial(_dil_attn_kernel, dil=dil, ls=ls),
        out_shape=(out_sds,) * 4,
        grid=(b,),
        in_specs=[
            pl.BlockSpec((None, dil, ls, QKV_W), lambda bi: (bi, 0, 0, col_block)),
            pl.BlockSpec((s, 128), lambda bi: (0, 0)),
            pl.BlockSpec((s, 128), lambda bi: (0, 0)),
        ],
        out_specs=(out_spec,) * 4,
        scratch_shapes=[
            pltpu.VMEM((s, 128), BF16),
            pltpu.VMEM((dil, ls + 2 * DIL_RADIUS, 128), BF16),
            pltpu.VMEM((dil, ls + 2 * DIL_RADIUS, 128), BF16),
        ],
        compiler_params=_cparams(("parallel",)),
        name=f"dil_attn_d{dil}",
    )(qkv, cos_d, sin_d)


def _swa_kernel(qkv_ref, cos_ref, sin_ref, sink_ref, o_ref, q_s, k_s, v_s, *, s_len):
    lo, head0 = _lane_masks()
    heads = (head0, jnp.logical_not(head0))
    cos = cos_ref[...]
    sin = sin_ref[...]
    pad = SWA_RADIUS
    qb = 128
    grp = SWA_Q_HEADS // SWA_KV_HEADS
    zeros = jnp.zeros((pad, 128), BF16)
    k_s[0:pad, :] = zeros
    k_s[pad + s_len:pad + s_len + pad, :] = zeros
    v_s[0:pad, :] = zeros
    v_s[pad + s_len:pad + s_len + pad, :] = zeros
    k = qkv_ref[:, 512:640].astype(F32)
    k_s[pad:pad + s_len, :] = _rope2(k, cos, sin, lo).astype(BF16)
    v_s[pad:pad + s_len, :] = qkv_ref[:, 640:768]
    for qp in range(grp):
        q = qkv_ref[:, qp * 128:(qp + 1) * 128].astype(F32)
        q_s[qp] = (_rope2(q, cos, sin, lo) * (HEAD_DIM ** -0.5)).astype(BF16)
    sinks = [jnp.concatenate([jnp.broadcast_to(sink_ref[kv * grp + p:kv * grp + p + 1, 0:1], (qb, 1))
                              for p in range(grp)], axis=0) for kv in range(SWA_KV_HEADS)]

    def blk(i, carry):
        q0 = pl.multiple_of(i * qb, qb)
        kw = k_s[pl.ds(q0, qb + 2 * pad), :]
        vw = v_s[pl.ds(q0, qb + 2 * pad), :]
        qall = jnp.concatenate([q_s[p, pl.ds(q0, qb), :] for p in range(grp)], axis=0)
        mask = _band_mask(q0, grp * qb, qb + 2 * pad, pad, SWA_RADIUS, s_len)
        ss = [lax.dot_general(jnp.where(hm, qall, jnp.zeros_like(qall)), kw, (((1,), (1,)), ((), ())),
                              preferred_element_type=F32) for hm in heads]
        soft = []
        for s, sk in zip(ss, sinks):
            s = jnp.where(mask, s, NEG_INF)
            m = jnp.maximum(jnp.max(s, axis=-1, keepdims=True), sk)
            p = jnp.exp(s - m)
            soft.append((p.astype(BF16), jnp.sum(p, axis=-1, keepdims=True) + jnp.exp(sk - m)))
        res = [jnp.dot(p, vw, preferred_element_type=F32) / den for p, den in soft]
        o = jnp.where(head0, res[0], res[1]).astype(o_ref.dtype)
        for p in range(grp):
            o_ref[pl.ds(q0, qb), p * 128:(p + 1) * 128] = o[p * qb:(p + 1) * qb, :]
        return carry

    lax.fori_loop(0, s_len // qb, blk, 0)


def swa_attention(proj, cos_t, sin_t, sink):
    b, s, n = proj.shape
    return pl.pallas_call(
        functools.partial(_swa_kernel, s_len=s),
        out_shape=jax.ShapeDtypeStruct((b, s, 512), BF16),
        grid=(b,),
        in_specs=[
            pl.BlockSpec((None, s, QKV_W), lambda bi: (bi, 0, COL_SWA // QKV_W)),
            pl.BlockSpec((s, 128), lambda bi: (0, 0)),
            pl.BlockSpec((s, 128), lambda bi: (0, 0)),
            pl.BlockSpec((8, 128), lambda bi: (0, 0)),
        ],
        out_specs=pl.BlockSpec((None, s, 512), lambda bi: (bi, 0, 0)),
        scratch_shapes=[
            pltpu.VMEM((SWA_Q_HEADS // SWA_KV_HEADS, s, 128), BF16),
            pltpu.VMEM((s + 2 * SWA_RADIUS, 128), BF16),
            pltpu.VMEM((s + 2 * SWA_RADIUS, 128), BF16),
        ],
        compiler_params=_cparams(("parallel",)),
        name="swa_attn",
    )(proj, cos_t, sin_t, jnp.broadcast_to(sink.astype(F32)[:, None], (8, 128)))


HY_CHUNKS = 4


def dft_tables(seq):
    kc = seq // HY_CHUNKS
    k = jnp.arange(seq, dtype=jnp.int32)
    phase = ((2 * k[:, None] + 1) * k[None, :]) % (4 * seq)
    ang = phase.astype(F32) * (2.0 * math.pi / (4 * seq))
    c = jnp.cos(ang).reshape(HY_CHUNKS, kc, seq)
    s = jnp.sin(ang).reshape(HY_CHUNKS, kc, seq)
    fwd = jnp.concatenate([c, s], axis=1).astype(BF16)
    inv = (jnp.concatenate([c, -s], axis=1) * (1.0 / seq)).transpose(0, 2, 1).astype(BF16)
    return fwd, inv


def _hy_filter_kernel(w1_ref, b1_ref, w2_ref, b2_ref, w3_ref, fr_ref, ld_ref, bands_ref, f_ref,
                      hr_ref, hi_ref, hs_s, hd_s, *, seq):
    hw = HYENA_WIDTH
    hi_p = lax.Precision.HIGHEST

    @pl.when(pl.program_id(2) == 0)
    def _():
        idx = lax.broadcasted_iota(jnp.int32, (seq, 1), 0).astype(F32)
        t = idx / max(seq - 1, 1)
        w = 2.0 * math.pi * idx / seq
        ang = w * bands_ref[...]
        w1 = w1_ref[...]
        pre = (t * w1[0:1, :]
               + jnp.dot(jnp.cos(ang), w1[1:1 + HYENA_BANDS, :], precision=hi_p, preferred_element_type=F32)
               - jnp.dot(jnp.sin(ang), w1[1 + HYENA_BANDS:1 + 2 * HYENA_BANDS, :], precision=hi_p,
                         preferred_element_type=F32)
               + b1_ref[...])
        h = jnp.sin(fr_ref[0:1, :] * pre)
        h = jnp.sin(fr_ref[1:2, :] * (jnp.dot(h, w2_ref[...], precision=hi_p, preferred_element_type=F32)
                                      + b2_ref[...]))
        h = jnp.dot(h, w3_ref[...], precision=hi_p, preferred_element_type=F32)
        h = h * jnp.exp(-t * jnp.exp(ld_ref[...]))
        hf = h[:, :hw]
        hb = jnp.where(idx > 0.0, h[:, hw:], 0.0)
        inv = lax.rsqrt(jnp.sum(hf * hf + hb * hb, axis=0, keepdims=True) + 1e-12)
        hs_s[...] = ((hf + hb) * inv).astype(BF16)
        hd_s[...] = ((hf - hb) * inv).astype(BF16)

    kc = seq // HY_CHUNKS
    hr_ref[...] = jnp.dot(f_ref[0:kc, :], hs_s[...], preferred_element_type=F32)
    hi_ref[...] = -jnp.dot(f_ref[kc:2 * kc, :], hd_s[...], preferred_element_type=F32)


def hyena_filters(w1, b1, w2, b2, w3p, freq, ldp, fwd, seq):
    depth = w1.shape[0]
    kc = seq // HY_CHUNKS
    hw = HYENA_WIDTH
    bands = jnp.linspace(1e-4, HYENA_BANDS - 1, HYENA_BANDS, dtype=F32).reshape(1, HYENA_BANDS)
    full = lambda *shape: pl.BlockSpec((None,) + shape, lambda l, o, c: (l,) + (0,) * len(shape))
    out_sds = jax.ShapeDtypeStruct((depth, 2, seq, hw), F32)
    return pl.pallas_call(
        functools.partial(_hy_filter_kernel, seq=seq),
        out_shape=(out_sds, out_sds),
        grid=(depth, 2, HY_CHUNKS),
        in_specs=[
            full(*w1.shape[1:]), full(1, b1.shape[-1]), full(*w2.shape[1:]), full(1, b2.shape[-1]),
            pl.BlockSpec((None, w3p.shape[1], 2 * hw), lambda l, o, c: (l, 0, o)),
            full(*freq.shape[1:]),
            pl.BlockSpec((None, 1, 2 * hw), lambda l, o, c: (l, 0, o)),
            pl.BlockSpec((1, HYENA_BANDS), lambda l, o, c: (0, 0)),
            pl.BlockSpec((None, 2 * kc, seq), lambda l, o, c: (c, 0, 0)),
        ],
        out_specs=(
            pl.BlockSpec((None, None, kc, hw), lambda l, o, c: (l, o, c, 0)),
            pl.BlockSpec((None, None, kc, hw), lambda l, o, c: (l, o, c, 0)),
        ),
        scratch_shapes=[pltpu.VMEM((seq, hw), BF16), pltpu.VMEM((seq, hw), BF16)],
        compiler_params=_cparams(("arbitrary", "arbitrary", "arbitrary")),
        name="hyena_filters",
    )(w1, b1[:, None, :], w2, b2[:, None, :], w3p, freq, ldp[:, None, :], bands, fwd)


def _hyena_kernel(hy_ref, cw_ref, cb_ref, bias_ref, hr_ref, hi_ref, f_ref, g_ref, o_ref, zb_s, zf_s, acc_s, *, seq):
    hw = HYENA_WIDTH
    o = pl.program_id(1)
    c = pl.program_id(2)
    kc = seq // HY_CHUNKS

    def short_conv(part):
        x = hy_ref[:, part * hw:(part + 1) * hw].astype(F32)
        row = lax.broadcasted_iota(jnp.int32, (seq, 1), 0)
        xm = jnp.where(row == 0, 0.0, pltpu.roll(x, 1, 0))
        xp = jnp.where(row == seq - 1, 0.0, pltpu.roll(x, seq - 1, 0))
        w = cw_ref[:, part * hw:(part + 1) * hw]
        return cb_ref[:, part * hw:(part + 1) * hw] + xm * w[0:1, :] + x * w[1:2, :] + xp * w[2:3, :]

    @pl.when((o == 0) & (c == 0))
    def _():
        z0 = short_conv(0)
        zf_s[...] = z0
        zb_s[...] = z0.astype(BF16)

    zz = jnp.dot(f_ref[...], zb_s[...], preferred_element_type=F32)
    zc = zz[:kc]
    zs = zz[kc:]
    hr = hr_ref[...]
    hi = hi_ref[...]
    yr = zc * hr + zs * hi
    yi = zc * hi - zs * hr
    yy = jnp.concatenate([yr, yi], axis=0).astype(BF16)
    part = jnp.dot(g_ref[...], yy, preferred_element_type=F32)

    @pl.when(c == 0)
    def _():
        acc_s[...] = part

    @pl.when(c > 0)
    def _():
        acc_s[...] += part

    @pl.when((c == HY_CHUNKS - 1) & (o == 0))
    def _():
        z1 = short_conv(1) * (acc_s[...] + bias_ref[0:1, :] * zf_s[...])
        zf_s[...] = z1
        zb_s[...] = z1.astype(BF16)

    @pl.when((c == HY_CHUNKS - 1) & (o == 1))
    def _():
        o_ref[...] = (short_conv(2) * (acc_s[...] + bias_ref[1:2, :] * zf_s[...])).astype(o_ref.dtype)


def hyena_mixer(proj, conv_w, conv_b, hy_bias, hr, hi, fwd, inv):
    b, s, n = proj.shape
    hw = HYENA_WIDTH
    kc = s // HY_CHUNKS
    return pl.pallas_call(
        functools.partial(_hyena_kernel, seq=s),
        out_shape=jax.ShapeDtypeStruct((b, s, hw), BF16),
        grid=(b, 2, HY_CHUNKS),
        in_specs=[
            pl.BlockSpec((None, s, 3 * hw), lambda bi, o, c: (bi, 0, COL_HY // (3 * hw))),
            pl.BlockSpec((3, 3 * hw), lambda bi, o, c: (0, 0)),
            pl.BlockSpec((1, 3 * hw), lambda bi, o, c: (0, 0)),
            pl.BlockSpec((2, hw), lambda bi, o, c: (0, 0)),
            pl.BlockSpec((None, kc, hw), lambda bi, o, c: (o, c, 0)),
            pl.BlockSpec((None, kc, hw), lambda bi, o, c: (o, c, 0)),
            pl.BlockSpec((None, 2 * kc, s), lambda bi, o, c: (c, 0, 0)),
            pl.BlockSpec((None, s, 2 * kc), lambda bi, o, c: (c, 0, 0)),
        ],
        out_specs=pl.BlockSpec((None, s, hw), lambda bi, o, c: (bi, 0, 0)),
        scratch_shapes=[pltpu.VMEM((s, hw), BF16), pltpu.VMEM((s, hw), F32), pltpu.VMEM((s, hw), F32)],
        compiler_params=_cparams(("parallel", "arbitrary", "arbitrary")),
        name="hyena_conv",
    )(proj, conv_w, conv_b.reshape(1, -1), hy_bias, hr, hi, fwd, inv)


def _layer_norm(y, g, b):
    mu = jnp.mean(y, axis=-1, keepdims=True)
    yc = y - mu
    var = jnp.mean(yc * yc, axis=-1, keepdims=True)
    return yc * lax.rsqrt(var + LN_EPS) * g + b


def _merge_kernel(*refs):
    att = refs[:12]
    (yb_ref, yc_ref, gl_ref, x_ref, g1_ref, sc2_ref, sh2_ref, lng_ref, lnb_ref, wa_ref, wb_ref, wc_ref, wo_ref,
     xo_ref, u2_ref) = refs[12:]
    d = D_MODEL
    halves = []
    for half in range(2):
        la, lb, lc = (att[4 * g + 2 + half][...] for g in range(3))
        m = jnp.maximum(jnp.maximum(la, lb), lc)
        ea, eb, ec = jnp.exp(la - m), jnp.exp(lb - m), jnp.exp(lc - m)
        inv = 1.0 / (ea + eb + ec)
        halves.append((ea * inv) * att[half][...] + (eb * inv) * att[4 + half][...] + (ec * inv) * att[8 + half][...])
    ya = jnp.concatenate(halves, axis=-1)
    za = jnp.dot(ya.astype(BF16), wa_ref[...], preferred_element_type=F32)
    zb = jnp.dot(yb_ref[...], wb_ref[...], preferred_element_type=F32)
    zc = jnp.dot(yc_ref[...], wc_ref[...], preferred_element_type=F32)
    merged = (jax.nn.sigmoid(gl_ref[:, 0:d].astype(F32)) * za
              + jax.nn.sigmoid(gl_ref[:, d:2 * d].astype(F32)) * zb
              + jax.nn.sigmoid(gl_ref[:, 2 * d:3 * d].astype(F32)) * zc)
    mix = jnp.dot(merged.astype(BF16), wo_ref[...], preferred_element_type=F32)
    xn = _layer_norm(ALPHA * x_ref[...] + g1_ref[...] * mix, lng_ref[...], lnb_ref[...])
    xo_ref[...] = xn
    u2_ref[...] = (xn * (1.0 + sc2_ref[...]) + sh2_ref[...]).T.astype(BF16)


def merge_mixers(att, yb, yc, proj, x, g1, sc2, sh2, ln_g, ln_b, wa, wb, wc, wo):
    b, s, d = x.shape
    tm = 512
    row = lambda w: pl.BlockSpec((None, tm, w), lambda bi, i: (bi, i, 0))
    per_b = pl.BlockSpec((None, 1, d), lambda bi, i: (bi, 0, 0))
    const = lambda r, c: pl.BlockSpec((r, c), lambda bi, i: (0, 0))
    return pl.pallas_call(
        _merge_kernel,
        out_shape=(jax.ShapeDtypeStruct((b, s, d), F32), jax.ShapeDtypeStruct((d, b * s), BF16)),
        grid=(b, s // tm),
        in_specs=[row(128)] * 12 + [row(512), row(512), row(3 * d), row(d), per_b, per_b, per_b,
                                    const(1, d), const(1, d), const(256, d), const(512, d), const(512, d),
                                    const(d, d)],
        out_specs=(row(d), pl.BlockSpec((d, tm), lambda bi, i: (0, bi * (s // tm) + i))),
        compiler_params=_cparams(("parallel", "parallel")),
        name="merge_mixers",
    )(*att, yb, yc, proj, x, g1, sc2, sh2, ln_g.reshape(1, d), ln_b.reshape(1, d), wa, wb, wc, wo)


_PEER_PAIRS = [(i, j) for i in range(1, PEER_TOPK + 1) for j in range(1, PEER_TOPK + 1) if i * j <= PEER_TOPK]
PEER_NOT_TOP = 100.0


def _peer_route_kernel(u_ref, wq_ref, keys_ref, e1_ref, cut_ref, e2_ref, r2_ref, q_s, s_s, ab_s, st_s, *, tr):
    nh, nk, topk = PEER_HEADS, PEER_KEYS, PEER_TOPK
    qt = jnp.dot(wq_ref[...], u_ref[...], preferred_element_type=F32)
    q_s[...] = qt.astype(BF16)
    for ph in range(2 * nh):
        s_s[ph] = jnp.dot(keys_ref[ph], q_s[ph * 128:(ph + 1) * 128, :], preferred_element_type=F32)

    def extract(ph, carry):
        p = ph // nh
        h = ph % nh

        def rnd(r, prev):
            sv = s_s[ph]
            cur = jnp.max(jnp.where(sv < prev, sv, -jnp.inf), axis=0, keepdims=True)
            ab_s[p, r, pl.ds(h, 1), :] = cur
            return cur

        lax.fori_loop(0, topk, rnd, jnp.full((1, tr), jnp.inf, F32))
        return carry

    lax.fori_loop(0, 2 * nh, extract, 0)

    for ch in range(tr // 128):
        ln = slice(ch * 128, (ch + 1) * 128)
        a = [ab_s[0, r, :, ln] for r in range(topk)]
        b = [ab_s[1, r, :, ln] for r in range(topk)]
        cand = [a[i - 1] + b[j - 1] for (i, j) in _PEER_PAIRS]
        tau = jnp.full((nh, 128), -jnp.inf, F32)
        for x, (ix, jx) in enumerate(_PEER_PAIRS):
            cnt = jnp.zeros((nh, 128), F32)
            for y, (iy, jy) in enumerate(_PEER_PAIRS):
                if iy <= ix and jy <= jx:
                    cnt = cnt + 1.0
                elif iy >= ix and jy >= jx:
                    continue
                else:
                    cnt = cnt + jnp.where(cand[y] >= cand[x], 1.0, 0.0)
            tau = jnp.maximum(tau, jnp.where(cnt >= float(topk), cand[x], -jnp.inf))
        top = cand[0]
        z = jnp.zeros((nh, 128), F32)
        for cx in cand:
            z = z + jnp.where(cx >= tau, jnp.exp(cx - top), 0.0)
        st_s[0, :, ln] = tau
        st_s[1, :, ln] = 1.0 / z

    for h in range(nh):
        rows = slice(h * nk, (h + 1) * nk)
        s1 = s_s[h]
        s2 = s_s[nh + h]
        hrow = slice(h, h + 1)
        tau = st_s[0, hrow, :]
        in1 = s1 >= ab_s[0, topk - 1, hrow, :]
        in2 = s2 >= ab_s[1, topk - 1, hrow, :]
        cut = jnp.zeros_like(s1)
        rank = jnp.ones_like(s2)
        for r in range(topk):
            br = ab_s[1, r, hrow, :]
            cut = cut + jnp.where(s1 + br >= tau, 1.0, 0.0)
            rank = rank + jnp.where(br > s2, 1.0, 0.0)
        e1_ref[rows, :] = jnp.where(in1, jnp.exp(s1 - ab_s[0, 0, hrow, :]), 0.0)
        cut_ref[rows, :] = jnp.where(in1, cut, 0.0)
        e2_ref[rows, :] = jnp.where(in2, jnp.exp(s2 - ab_s[1, 0, hrow, :]) * st_s[1, hrow, :], 0.0).astype(BF16)
        r2_ref[rows, :] = jnp.where(in2, rank, PEER_NOT_TOP).astype(BF16)


def peer_route(u2, wq_t, keys):
    d, t = u2.shape
    tr = 512
    nrow = PEER_HEADS * PEER_KEYS
    tab = lambda dt: jax.ShapeDtypeStruct((nrow, t), dt)
    col = pl.BlockSpec((nrow, tr), lambda i: (0, i))
    return pl.pallas_call(
        functools.partial(_peer_route_kernel, tr=tr),
        out_shape=(tab(F32), tab(F32), tab(BF16), tab(BF16)),
        grid=(t // tr,),
        in_specs=[
            pl.BlockSpec((d, tr), lambda i: (0, i)),
            pl.BlockSpec(wq_t.shape, lambda i: (0, 0)),
            pl.BlockSpec(keys.shape, lambda i: (0, 0, 0)),
        ],
        out_specs=(col, col, col, col),
        scratch_shapes=[
            pltpu.VMEM((2 * nrow, tr), BF16),
            pltpu.VMEM((2 * PEER_HEADS, PEER_KEYS, tr), F32),
            pltpu.VMEM((2, PEER_TOPK, PEER_HEADS, tr), F32),
            pltpu.VMEM((2, PEER_HEADS, tr), F32),
        ],
        compiler_params=_cparams(("parallel",)),
        name="peer_route",
    )(u2, wq_t, keys)


F8 = jnp.float8_e4m3fn
F8_MAX = 448.0
WG_SCALE = 16.0


def _peer_dense_kernel(sc_ref, x_ref, u_ref, vt_ref, e1_ref, cut_ref, e2_ref, r2_ref, o_ref, acc_s, wga_s, wgb_s,
                       *, te, n_e):
    j = pl.program_id(1)
    nk = PEER_KEYS
    tn = x_ref.shape[1]
    sub = 16

    @pl.when(j == 0)
    def _():
        acc_s[...] = jnp.zeros_like(acc_s)
        wgb_s[...] = jnp.zeros_like(wgb_s)

    def step(w_cur, w_prev):
        jj = jnp.minimum(j, n_e - 1)
        inv_a = sc_ref[0]
        ts = 512
        for sl in range(tn // ts):
            lanes = slice(sl * ts, (sl + 1) * ts)
            at = (jnp.dot(u_ref[...], x_ref[:, lanes], preferred_element_type=F32) * inv_a).astype(BF16)
            acc_s[:, lanes] += jnp.dot(vt_ref[...], w_prev[:, lanes], preferred_element_type=F32)
            g = (0.5 * WG_SCALE * at) * (1.0 + lax.erf(at * (2.0 ** -0.5)))
            for ii in range(te // nk):
                i = jj * (te // nk) + ii
                w = None
                for h in range(PEER_HEADS):
                    cb = jnp.broadcast_to(cut_ref[pl.ds(h * nk + i, 1), lanes], (sub, ts)).astype(BF16)[None]
                    e1 = jnp.broadcast_to(e1_ref[pl.ds(h * nk + i, 1), lanes], (sub, ts)).astype(BF16)[None]
                    r2 = r2_ref[h * nk:(h + 1) * nk, lanes].reshape(nk // sub, sub, ts)
                    e2 = e2_ref[h * nk:(h + 1) * nk, lanes].reshape(nk // sub, sub, ts)
                    sel = jnp.where(r2 <= cb, e2, jnp.zeros((), BF16)) * e1
                    w = sel if w is None else w + sel
                wg = jnp.clip(g[ii * nk:(ii + 1) * nk, :] * w.reshape(nk, ts), -F8_MAX, F8_MAX)
                w_cur[ii * nk:(ii + 1) * nk, lanes] = wg.astype(F8)

    @pl.when((j % 2 == 0) & (j < n_e))
    def _():
        step(wga_s, wgb_s)

    @pl.when(j % 2 == 1)
    def _():
        step(wgb_s, wga_s)

    @pl.when(j == n_e)
    def _():
        acc = acc_s[...] + jnp.dot(vt_ref[...], wgb_s[...], preferred_element_type=F32)
        o_ref[...] = (acc * sc_ref[1]).T


def peer_dense(scales, x8, u_tab, vt_tab, e1, cut, e2, r2):
    d, t = x8.shape
    ne = u_tab.shape[0]
    tn, te = 1024, 512
    n_e = ne // te
    nrow = PEER_HEADS * PEER_KEYS
    col = pl.BlockSpec((nrow, tn), lambda i, j: (0, i))
    return pl.pallas_call(
        functools.partial(_peer_dense_kernel, te=te, n_e=n_e),
        out_shape=jax.ShapeDtypeStruct((t, d), F32),
        grid=(t // tn, n_e + 1),
        in_specs=[
            pl.BlockSpec(memory_space=pltpu.SMEM),
            pl.BlockSpec((d, tn), lambda i, j: (0, i)),
            pl.BlockSpec((te, d), lambda i, j: (jnp.minimum(j, n_e - 1), 0)),
            pl.BlockSpec((d, te), lambda i, j: (0, jnp.maximum(j - 1, 0))),
            col, col, col, col,
        ],
        out_specs=pl.BlockSpec((tn, d), lambda i, j: (i, 0)),
        scratch_shapes=[pltpu.VMEM((d, tn), F32), pltpu.VMEM((te, tn), F8), pltpu.VMEM((te, tn), F8)],
        compiler_params=_cparams(("parallel", "arbitrary")),
        name="peer_dense",
    )(scales, x8, u_tab, vt_tab, e1, cut, e2, r2)


def _peer_out_kernel(x_ref, f_ref, g2_ref, lng_ref, lnb_ref, o_ref):
    o_ref[...] = _layer_norm(ALPHA * x_ref[...] + g2_ref[...] * f_ref[...], lng_ref[...], lnb_ref[...])


def peer_residual(x, ffn, g2, ln_g, ln_b):
    b, s, d = x.shape
    tm = 1024
    row = pl.BlockSpec((None, tm, d), lambda bi, i: (bi, i, 0))
    const = pl.BlockSpec((1, d), lambda bi, i: (0, 0))
    return pl.pallas_call(
        _peer_out_kernel,
        out_shape=jax.ShapeDtypeStruct((b, s, d), F32),
        grid=(b, s // tm),
        in_specs=[row, row, pl.BlockSpec((None, 1, d), lambda bi, i: (bi, 0, 0)), const, const],
        out_specs=row,
        compiler_params=_cparams(("parallel", "parallel")),
        name="peer_residual",
    )(x, ffn.reshape(b, s, d), g2, ln_g.reshape(1, d), ln_b.reshape(1, d))


def rope_tables(seq):
    pos = jnp.arange(seq, dtype=F32)
    inv = ROPE_THETA ** (-jnp.arange(0, HEAD_DIM, 2, dtype=F32) / HEAD_DIM)
    ang = pos[:, None] * inv[None, :]
    sign = jnp.where((jnp.arange(128) % 64) < 32, -1.0, 1.0).astype(F32)
    return jnp.tile(jnp.cos(ang), (1, 4)), jnp.tile(jnp.sin(ang), (1, 4)) * sign[None, :]


def _swa_head_perm():
    heads = [h for p in range(4) for h in (p, 4 + p)]
    return jnp.concatenate([jnp.arange(h * HEAD_DIM, (h + 1) * HEAD_DIM) for h in heads])


def relayout_w_in(w):
    qa, ka, va = w[:, 0:768], w[:, 768:1536], w[:, 1536:2304]
    hy = w[:, 2304:3840]
    qc, kvc = w[:, 3840:4352], w[:, 4352:4608]
    gates = w[:, 4608:7680]
    dil = [jnp.concatenate([t[:, 256 * g:256 * (g + 1)] for t in (qa, ka, va)], axis=1) for g in range(3)]
    main = jnp.concatenate([gates, hy, dil[0], qc[:, _swa_head_perm()], kvc], axis=1).astype(BF16)
    return main, jnp.concatenate(dil[1:], axis=1).astype(BF16)


def residue_major(tab, dil):
    s = tab.shape[0]
    return tab.reshape(s // dil, dil, 128).transpose(1, 0, 2).reshape(s, 128)


def token_mixer(x, ada, w_main, w_dil, conv_w, conv_b, hy_bias, attn_sink, hr, hi, fwd, inv, rope_tabs,
                wa, wb, wc, wo, ln_g, ln_b):
    b, s, _ = x.shape
    sh1, sc1, g1, sh2, sc2, _ = [a[:, None, :] for a in jnp.split(ada, 6, axis=-1)]
    proj = in_proj(x, sc1, sh1, w_main)
    qkv1, qkv2 = in_proj_dil(x, sc1, sh1, w_dil)
    att = (dilated_attention(proj.reshape(b, 1, s, MAIN_WIDTH), COL_DIL0 // QKV_W, *rope_tabs[0])
           + dilated_attention(qkv1, 0, *rope_tabs[1]) + dilated_attention(qkv2, 0, *rope_tabs[2]))
    yb = hyena_mixer(proj, conv_w, conv_b, hy_bias, hr, hi, fwd, inv)
    yc = swa_attention(proj, *rope_tabs[0], attn_sink)
    return merge_mixers(att, yb, yc, proj, x, g1, sc2, sh2, ln_g, ln_b, wa, wb, wc, wo)


def peer_ffn(u2, wq, keys, u_tab, v_tab):
    d = u2.shape[0]
    nh, nk = PEER_HEADS, PEER_KEYS
    wq_t = wq.reshape(d, nh, 2, nk).transpose(2, 1, 3, 0).reshape(2 * nh * nk, d).astype(BF16)
    keys_r = keys.transpose(1, 0, 2, 3).reshape(2 * nh, nk, keys.shape[-1]).astype(BF16)
    tabs = peer_route(u2, wq_t, keys_r)
    su, sv, sx = _pow2_scale(u_tab), _pow2_scale(v_tab), _pow2_scale(u2)
    scales = jnp.stack([1.0 / (su * sx), 1.0 / (sv * WG_SCALE)]).astype(F32)
    x8 = (u2.astype(F32) * sx).astype(F8)
    return peer_dense(scales, x8, (u_tab * su).astype(F8), (v_tab.T * sv).astype(F8), *tabs)


def _pow2_scale(a):
    m = jnp.maximum(jnp.max(jnp.abs(a)).astype(F32), 2.0 ** -100)
    return jnp.exp2(jnp.floor(jnp.log2(F8_MAX / m)))


def kernel(x, c, w_ada, b_ada, w_in, conv_w, conv_b, hy_w1, hy_b1, hy_w2, hy_b2, hy_w3, hy_freq, hy_log_decay,
           hy_bias, attn_sink, w_branch_a, w_branch_b, w_branch_c, w_out, ln_g, ln_b, peer_wq, peer_keys, peer_u,
           peer_v):
    b, s, d = x.shape
    depth = w_in.shape[0]
    hw = HYENA_WIDTH
    cos_t, sin_t = rope_tables(s)
    fwd, inv = dft_tables(s)
    ada = ada_ln(c, w_ada, b_ada)
    w3p = hy_w3.reshape(depth, -1, 2, 2, hw).transpose(0, 1, 3, 2, 4).reshape(depth, -1, 4 * hw)
    ldp = hy_log_decay.reshape(depth, 2, 2, hw).transpose(0, 2, 1, 3).reshape(depth, 4 * hw)
    hr, hi = hyena_filters(hy_w1, hy_b1, hy_w2, hy_b2, w3p, hy_freq, ldp, fwd, s)
    perm = _swa_head_perm()
    rope_tabs = [(residue_major(cos_t, dil), residue_major(sin_t, dil)) for _, dil in DIL_GROUPS]
    for l in range(depth):
        x1, u2 = token_mixer(x, ada[l], *relayout_w_in(w_in[l]), conv_w[l], conv_b[l], hy_bias[l], attn_sink[l],
                             hr[l], hi[l], fwd, inv, rope_tabs, w_branch_a[l].astype(BF16),
                             w_branch_b[l].astype(BF16), w_branch_c[l][perm].astype(BF16), w_out[l].astype(BF16),
                             ln_g[l, 0], ln_b[l, 0])
        ffn = peer_ffn(u2, peer_wq[l], peer_keys[l], peer_u[l], peer_v[l])
        x = peer_residual(x1, ffn, ada[l][:, None, 5 * d:6 * d], ln_g[l, 1], ln_b[l, 1])
    return x
```

```python
import functools
import math

import jax
import jax.numpy as jnp
from jax import lax
from jax.experimental import pallas as pl
from jax.experimental.pallas import tpu as pltpu

F32 = jnp.float32
BF16 = jnp.bfloat16

D_MODEL = 1024
HEAD_DIM = 64
DIL_GROUPS = ((128, 1), (512, 4), (2048, 16))
DIL_RADIUS = 64
HYENA_WIDTH = 512
HYENA_BANDS = 16
SWA_RADIUS = 128
SWA_Q_HEADS = 8
SWA_KV_HEADS = 2
PEER_HEADS = 8
PEER_KEYS = 128
PEER_TOPK = 16
DEPTH = 2
ALPHA = (2 * DEPTH) ** 0.25
LN_EPS = 1e-5
NEG_INF = -1e30
ROPE_THETA = 10000.0

COL_GATES = 0
COL_HY = 3072
COL_DIL0 = 4608
COL_SWA = 5376
MAIN_WIDTH = 6144
QKV_W = 768

VMEM_LIMIT = 56 * 1024 * 1024


def _cparams(sem, vmem=VMEM_LIMIT):
    return pltpu.CompilerParams(dimension_semantics=sem, vmem_limit_bytes=vmem)


def _ada_kernel(c_ref, w_ref, b_ref, o_ref):
    c = c_ref[...]
    cond = (c * jax.nn.sigmoid(c)).astype(BF16)
    o_ref[...] = jnp.dot(cond, w_ref[...].astype(BF16), preferred_element_type=F32) + b_ref[...]


def ada_ln(c, w_ada, b_ada):
    depth, d, n = w_ada.shape
    b = c.shape[0]
    tn = 1536
    return pl.pallas_call(
        _ada_kernel,
        out_shape=jax.ShapeDtypeStruct((depth, b, n), F32),
        grid=(depth, n // tn),
        in_specs=[
            pl.BlockSpec((b, d), lambda l, j: (0, 0)),
            pl.BlockSpec((None, d, tn), lambda l, j: (l, 0, j)),
            pl.BlockSpec((None, 1, tn), lambda l, j: (l, 0, j)),
        ],
        out_specs=pl.BlockSpec((None, b, tn), lambda l, j: (l, 0, j)),
        compiler_params=_cparams(("arbitrary", "arbitrary")),
        name="ada_ln",
    )(c, w_ada, b_ada.reshape(depth, 1, n))


def _inproj_kernel(x_ref, sc_ref, sh_ref, w_ref, o_ref, u_ref):
    @pl.when(pl.program_id(2) == 0)
    def _():
        u_ref[...] = (x_ref[...] * (1.0 + sc_ref[...]) + sh_ref[...]).astype(BF16)

    o_ref[...] = jnp.dot(u_ref[...], w_ref[...], preferred_element_type=F32).astype(o_ref.dtype)


def in_proj(x, sc, sh, w):
    b, s, d = x.shape
    n = w.shape[1]
    tm, tn = 1024, 1536
    return pl.pallas_call(
        _inproj_kernel,
        out_shape=jax.ShapeDtypeStruct((b, s, n), BF16),
        grid=(b, s // tm, n // tn),
        in_specs=[
            pl.BlockSpec((None, tm, d), lambda bi, i, j: (bi, i, 0)),
            pl.BlockSpec((None, 1, d), lambda bi, i, j: (bi, 0, 0)),
            pl.BlockSpec((None, 1, d), lambda bi, i, j: (bi, 0, 0)),
            pl.BlockSpec((d, tn), lambda bi, i, j: (0, j)),
        ],
        out_specs=pl.BlockSpec((None, tm, tn), lambda bi, i, j: (bi, i, j)),
        scratch_shapes=[pltpu.VMEM((tm, d), BF16)],
        compiler_params=_cparams(("parallel", "parallel", "arbitrary")),
        name="in_proj",
    )(x, sc, sh, w)


def _inproj_dil_kernel(x_ref, sc_ref, sh_ref, w_ref, o1_ref, o2_ref, res_s):
    tm = x_ref.shape[0]
    u = (x_ref[...] * (1.0 + sc_ref[...]) + sh_ref[...]).astype(BF16)
    res = jnp.dot(u, w_ref[...], preferred_element_type=F32)
    nslab = QKV_W // 128
    for c in range(2 * nslab):
        res_s[c] = res[:, c * 128:(c + 1) * 128]
    for g, o_ref in ((0, o1_ref), (1, o2_ref)):
        dil = o_ref.shape[0]
        for r in range(dil):
            for c in range(nslab):
                o_ref[r, :, c * 128:(c + 1) * 128] = res_s[g * nslab + c, pl.ds(r, tm // dil, stride=dil), :].astype(
                    o_ref.dtype)


def in_proj_dil(x, sc, sh, w):
    b, s, d = x.shape
    tm = 1024
    d1, d2 = DIL_GROUPS[1][1], DIL_GROUPS[2][1]
    return pl.pallas_call(
        _inproj_dil_kernel,
        out_shape=(jax.ShapeDtypeStruct((b, d1, s // d1, QKV_W), BF16),
                   jax.ShapeDtypeStruct((b, d2, s // d2, QKV_W), BF16)),
        grid=(b, s // tm),
        in_specs=[
            pl.BlockSpec((None, tm, d), lambda bi, i: (bi, i, 0)),
            pl.BlockSpec((None, 1, d), lambda bi, i: (bi, 0, 0)),
            pl.BlockSpec((None, 1, d), lambda bi, i: (bi, 0, 0)),
            pl.BlockSpec((d, 2 * QKV_W), lambda bi, i: (0, 0)),
        ],
        out_specs=(
            pl.BlockSpec((None, d1, tm // d1, QKV_W), lambda bi, i: (bi, 0, i, 0)),
            pl.BlockSpec((None, d2, tm // d2, QKV_W), lambda bi, i: (bi, 0, i, 0)),
        ),
        scratch_shapes=[pltpu.VMEM((2 * QKV_W // 128, tm, 128), F32)],
        compiler_params=_cparams(("parallel", "parallel")),
        name="in_proj_dil",
    )(x, sc, sh, w)


def _rope2(x, cos, sin_signed, lo):
    xr = jnp.where(lo, pltpu.roll(x, 96, 1), pltpu.roll(x, 32, 1))
    return x * cos + xr * sin_signed


def _lane_masks():
    lane = lax.broadcasted_iota(jnp.int32, (1, 128), 1)
    return (lane % 64) < 32, lane < 64


def _band_mask(q0, nq, nk, pad, radius, length):
    qpos = q0 + (lax.broadcasted_iota(jnp.int32, (nq, 1), 0) & 127)
    kpos = q0 - pad + lax.broadcasted_iota(jnp.int32, (1, nk), 1)
    kpos = jnp.where(kpos < 0, -4 * length, jnp.where(kpos >= length, -4 * length, kpos))
    return jnp.abs(qpos - kpos) <= radius


def _dil_attn_kernel(qkv_ref, cos_ref, sin_ref, olo_ref, ohi_ref, llo_ref, lhi_ref, q_s, k_s, v_s, *, dil, ls):
    lo, head0 = _lane_masks()
    heads = (head0, jnp.logical_not(head0))
    cos = cos_ref[...]
    sin = sin_ref[...]
    pad = DIL_RADIUS
    qb = 128
    nblk = ls // qb
    ngroup = 8
    zeros = jnp.zeros((dil, pad, 128), BF16)
    k_s[:, 0:pad, :] = zeros
    k_s[:, pad + ls:pad + ls + pad, :] = zeros
    v_s[:, 0:pad, :] = zeros
    v_s[:, pad + ls:pad + ls + pad, :] = zeros
    for hp, (o_ref, l_ref) in enumerate(((olo_ref, llo_ref), (ohi_ref, lhi_ref))):
        q = qkv_ref[:, :, hp * 128:(hp + 1) * 128].astype(F32).reshape(dil * ls, 128)
        k = qkv_ref[:, :, 256 + hp * 128:256 + (hp + 1) * 128].astype(F32).reshape(dil * ls, 128)
        q_s[...] = (_rope2(q, cos, sin, lo) * (HEAD_DIM ** -0.5)).astype(BF16)
        k_s[:, pad:pad + ls, :] = _rope2(k, cos, sin, lo).astype(BF16).reshape(dil, ls, 128)
        v_s[:, pad:pad + ls, :] = qkv_ref[:, :, 512 + hp * 128:512 + (hp + 1) * 128]

        def grp(gi, carry):
            tiles = []
            for t in range(ngroup):
                c = gi * ngroup + t
                r = c // nblk
                q0 = pl.multiple_of((c % nblk) * qb, qb)
                qblk = q_s[pl.ds(pl.multiple_of(c * qb, qb), qb), :]
                kw = k_s[r, pl.ds(q0, qb + 2 * pad), :]
                vw = v_s[r, pl.ds(q0, qb + 2 * pad), :]
                mask = _band_mask(q0, qb, qb + 2 * pad, pad, DIL_RADIUS, ls)
                ss = [lax.dot_general(jnp.where(hm, qblk, jnp.zeros_like(qblk)), kw, (((1,), (1,)), ((), ())),
                                      preferred_element_type=F32) for hm in heads]
                tiles.append((r, q0, vw, mask, ss))
            soft = []
            for r, q0, vw, mask, ss in tiles:
                for s in ss:
                    s = jnp.where(mask, s, NEG_INF)
                    m = jnp.max(s, axis=-1, keepdims=True)
                    p = jnp.exp(s - m)
                    soft.append((p.astype(BF16), jnp.sum(p, axis=-1, keepdims=True), m))
            for ti, (r, q0, vw, mask, ss) in enumerate(tiles):
                res = []
                for h in range(2):
                    p, den, m = soft[2 * ti + h]
                    res.append((jnp.dot(p, vw, preferred_element_type=F32) / den, m + jnp.log(den)))
                o = jnp.where(head0, res[0][0], res[1][0])
                lse = jnp.where(head0, res[0][1], res[1][1])
                rows = pl.ds(q0, qb) if dil == 1 else pl.ds(q0 * dil + r, qb, stride=dil)
                o_ref[rows, :] = o
                l_ref[rows, :] = lse
            return carry

        lax.fori_loop(0, dil * nblk // ngroup, grp, 0)


def dilated_attention(qkv, col_block, cos_d, sin_d):
    b, dil, ls, _ = qkv.shape
    s = dil * ls
    out_sds = jax.ShapeDtypeStruct((b, s, 128), F32)
    out_spec = pl.BlockSpec((None, s, 128), lambda bi: (bi, 0, 0))
    return pl.pallas_call(
        functools.partial(_dil_attn_kernel, dil=dil, ls=ls),
        out_shape=(out_sds,) * 4,
        grid=(b,),
        in_specs=[
            pl.BlockSpec((None, dil, ls, QKV_W), lambda bi: (bi, 0, 0, col_block)),
            pl.BlockSpec((s, 128), lambda bi: (0, 0)),
            pl.BlockSpec((s, 128), lambda bi: (0, 0)),
        ],
        out_specs=(out_spec,) * 4,
        scratch_shapes=[
            pltpu.VMEM((s, 128), BF16),
            pltpu.VMEM((dil, ls + 2 * DIL_RADIUS, 128), BF16),
            pltpu.VMEM((dil, ls + 2 * DIL_RADIUS, 128), BF16),
        ],
        compiler_params=_cparams(("parallel",)),
        name=f"dil_attn_d{dil}",
    )(qkv, cos_d, sin_d)


def _swa_kernel(qkv_ref, cos_ref, sin_ref, sink_ref, o_ref, q_s, k_s, v_s, *, s_len):
    lo, head0 = _lane_masks()
    heads = (head0, jnp.logical_not(head0))
    cos = cos_ref[...]
    sin = sin_ref[...]
    pad = SWA_RADIUS
    qb = 128
    grp = SWA_Q_HEADS // SWA_KV_HEADS
    zeros = jnp.zeros((pad, 128), BF16)
    k_s[0:pad, :] = zeros
    k_s[pad + s_len:pad + s_len + pad, :] = zeros
    v_s[0:pad, :] = zeros
    v_s[pad + s_len:pad + s_len + pad, :] = zeros
    k = qkv_ref[:, 512:640].astype(F32)
    k_s[pad:pad + s_len, :] = _rope2(k, cos, sin, lo).astype(BF16)
    v_s[pad:pad + s_len, :] = qkv_ref[:, 640:768]
    for qp in range(grp):
        q = qkv_ref[:, qp * 128:(qp + 1) * 128].astype(F32)
        q_s[qp] = (_rope2(q, cos, sin, lo) * (HEAD_DIM ** -0.5)).astype(BF16)
    sinks = [jnp.concatenate([jnp.broadcast_to(sink_ref[kv * grp + p:kv * grp + p + 1, 0:1], (qb, 1))
                              for p in range(grp)], axis=0) for kv in range(SWA_KV_HEADS)]

    def blk(i, carry):
        q0 = pl.multiple_of(i * qb, qb)
        kw = k_s[pl.ds(q0, qb + 2 * pad), :]
        vw = v_s[pl.ds(q0, qb + 2 * pad), :]
        qall = jnp.concatenate([q_s[p, pl.ds(q0, qb), :] for p in range(grp)], axis=0)
        mask = _band_mask(q0, grp * qb, qb + 2 * pad, pad, SWA_RADIUS, s_len)
        ss = [lax.dot_general(jnp.where(hm, qall, jnp.zeros_like(qall)), kw, (((1,), (1,)), ((), ())),
                              preferred_element_type=F32) for hm in heads]
        soft = []
        for s, sk in zip(ss, sinks):
            s = jnp.where(mask, s, NEG_INF)
            m = jnp.maximum(jnp.max(s, axis=-1, keepdims=True), sk)
            p = jnp.exp(s - m)
            soft.append((p.astype(BF16), jnp.sum(p, axis=-1, keepdims=True) + jnp.exp(sk - m)))
        res = [jnp.dot(p, vw, preferred_element_type=F32) / den for p, den in soft]
        o = jnp.where(head0, res[0], res[1]).astype(o_ref.dtype)
        for p in range(grp):
            o_ref[pl.ds(q0, qb), p * 128:(p + 1) * 128] = o[p * qb:(p + 1) * qb, :]
        return carry

    lax.fori_loop(0, s_len // qb, blk, 0, unroll=8)


def swa_attention(proj, cos_t, sin_t, sink):
    b, s, n = proj.shape
    return pl.pallas_call(
        functools.partial(_swa_kernel, s_len=s),
        out_shape=jax.ShapeDtypeStruct((b, s, 512), BF16),
        grid=(b,),
        in_specs=[
            pl.BlockSpec((None, s, QKV_W), lambda bi: (bi, 0, COL_SWA // QKV_W)),
            pl.BlockSpec((s, 128), lambda bi: (0, 0)),
            pl.BlockSpec((s, 128), lambda bi: (0, 0)),
            pl.BlockSpec((8, 128), lambda bi: (0, 0)),
        ],
        out_specs=pl.BlockSpec((None, s, 512), lambda bi: (bi, 0, 0)),
        scratch_shapes=[
            pltpu.VMEM((SWA_Q_HEADS // SWA_KV_HEADS, s, 128), BF16),
            pltpu.VMEM((s + 2 * SWA_RADIUS, 128), BF16),
            pltpu.VMEM((s + 2 * SWA_RADIUS, 128), BF16),
        ],
        compiler_params=_cparams(("parallel",)),
        name="swa_attn",
    )(proj, cos_t, sin_t, jnp.broadcast_to(sink.astype(F32)[:, None], (8, 128)))


HY_CHUNKS = 4


def dft_tables(seq):
    kc = seq // HY_CHUNKS
    k = jnp.arange(seq, dtype=jnp.int32)
    phase = ((2 * k[:, None] + 1) * k[None, :]) % (4 * seq)
    ang = phase.astype(F32) * (2.0 * math.pi / (4 * seq))
    c = jnp.cos(ang).reshape(HY_CHUNKS, kc, seq)
    s = jnp.sin(ang).reshape(HY_CHUNKS, kc, seq)
    fwd = jnp.concatenate([c, s], axis=1).astype(BF16)
    inv = (jnp.concatenate([c, -s], axis=1) * (1.0 / seq)).transpose(0, 2, 1).astype(BF16)
    return fwd, inv


def _hy_filter_kernel(w1_ref, b1_ref, w2_ref, b2_ref, w3_ref, fr_ref, ld_ref, bands_ref, f_ref,
                      hr_ref, hi_ref, hs_s, hd_s, *, seq):
    hw = HYENA_WIDTH
    hi_p = lax.Precision.HIGHEST

    @pl.when(pl.program_id(2) == 0)
    def _():
        idx = lax.broadcasted_iota(jnp.int32, (seq, 1), 0).astype(F32)
        t = idx / max(seq - 1, 1)
        w = 2.0 * math.pi * idx / seq
        ang = w * bands_ref[...]
        w1 = w1_ref[...]
        pre = (t * w1[0:1, :]
               + jnp.dot(jnp.cos(ang), w1[1:1 + HYENA_BANDS, :], precision=hi_p, preferred_element_type=F32)
               - jnp.dot(jnp.sin(ang), w1[1 + HYENA_BANDS:1 + 2 * HYENA_BANDS, :], precision=hi_p,
                         preferred_element_type=F32)
               + b1_ref[...])
        h = jnp.sin(fr_ref[0:1, :] * pre)
        h = jnp.sin(fr_ref[1:2, :] * (jnp.dot(h, w2_ref[...], precision=hi_p, preferred_element_type=F32)
                                      + b2_ref[...]))
        h = jnp.dot(h, w3_ref[...], precision=hi_p, preferred_element_type=F32)
        h = h * jnp.exp(-t * jnp.exp(ld_ref[...]))
        hf = h[:, :hw]
        hb = jnp.where(idx > 0.0, h[:, hw:], 0.0)
        inv = lax.rsqrt(jnp.sum(hf * hf + hb * hb, axis=0, keepdims=True) + 1e-12)
        hs_s[...] = ((hf + hb) * inv).astype(BF16)
        hd_s[...] = ((hf - hb) * inv).astype(BF16)

    kc = seq // HY_CHUNKS
    hr_ref[...] = jnp.dot(f_ref[0:kc, :], hs_s[...], preferred_element_type=F32)
    hi_ref[...] = -jnp.dot(f_ref[kc:2 * kc, :], hd_s[...], preferred_element_type=F32)


def hyena_filters(w1, b1, w2, b2, w3p, freq, ldp, fwd, seq):
    depth = w1.shape[0]
    kc = seq // HY_CHUNKS
    hw = HYENA_WIDTH
    bands = jnp.linspace(1e-4, HYENA_BANDS - 1, HYENA_BANDS, dtype=F32).reshape(1, HYENA_BANDS)
    full = lambda *shape: pl.BlockSpec((None,) + shape, lambda l, o, c: (l,) + (0,) * len(shape))
    out_sds = jax.ShapeDtypeStruct((depth, 2, seq, hw), F32)
    return pl.pallas_call(
        functools.partial(_hy_filter_kernel, seq=seq),
        out_shape=(out_sds, out_sds),
        grid=(depth, 2, HY_CHUNKS),
        in_specs=[
            full(*w1.shape[1:]), full(1, b1.shape[-1]), full(*w2.shape[1:]), full(1, b2.shape[-1]),
            pl.BlockSpec((None, w3p.shape[1], 2 * hw), lambda l, o, c: (l, 0, o)),
            full(*freq.shape[1:]),
            pl.BlockSpec((None, 1, 2 * hw), lambda l, o, c: (l, 0, o)),
            pl.BlockSpec((1, HYENA_BANDS), lambda l, o, c: (0, 0)),
            pl.BlockSpec((None, 2 * kc, seq), lambda l, o, c: (c, 0, 0)),
        ],
        out_specs=(
            pl.BlockSpec((None, None, kc, hw), lambda l, o, c: (l, o, c, 0)),
            pl.BlockSpec((None, None, kc, hw), lambda l, o, c: (l, o, c, 0)),
        ),
        scratch_shapes=[pltpu.VMEM((seq, hw), BF16), pltpu.VMEM((seq, hw), BF16)],
        compiler_params=_cparams(("arbitrary", "arbitrary", "arbitrary")),
        name="hyena_filters",
    )(w1, b1[:, None, :], w2, b2[:, None, :], w3p, freq, ldp[:, None, :], bands, fwd)


def _hyena_kernel(hy_ref, cw_ref, cb_ref, bias_ref, hr_ref, hi_ref, f_ref, g_ref, o_ref, zb_s, zf_s, acc_s, *, seq):
    hw = HYENA_WIDTH
    o = pl.program_id(1)
    c = pl.program_id(2)
    kc = seq // HY_CHUNKS

    def short_conv(part):
        x = hy_ref[:, part * hw:(part + 1) * hw].astype(F32)
        row = lax.broadcasted_iota(jnp.int32, (seq, 1), 0)
        xm = jnp.where(row == 0, 0.0, pltpu.roll(x, 1, 0))
        xp = jnp.where(row == seq - 1, 0.0, pltpu.roll(x, seq - 1, 0))
        w = cw_ref[:, part * hw:(part + 1) * hw]
        return cb_ref[:, part * hw:(part + 1) * hw] + xm * w[0:1, :] + x * w[1:2, :] + xp * w[2:3, :]

    @pl.when((o == 0) & (c == 0))
    def _():
        z0 = short_conv(0)
        zf_s[...] = z0
        zb_s[...] = z0.astype(BF16)

    zz = jnp.dot(f_ref[...], zb_s[...], preferred_element_type=F32)
    zc = zz[:kc]
    zs = zz[kc:]
    hr = hr_ref[...]
    hi = hi_ref[...]
    yr = zc * hr + zs * hi
    yi = zc * hi - zs * hr
    yy = jnp.concatenate([yr, yi], axis=0).astype(BF16)
    part = jnp.dot(g_ref[...], yy, preferred_element_type=F32)

    @pl.when(c == 0)
    def _():
        acc_s[...] = part

    @pl.when(c > 0)
    def _():
        acc_s[...] += part

    @pl.when((c == HY_CHUNKS - 1) & (o == 0))
    def _():
        z1 = short_conv(1) * (acc_s[...] + bias_ref[0:1, :] * zf_s[...])
        zf_s[...] = z1
        zb_s[...] = z1.astype(BF16)

    @pl.when((c == HY_CHUNKS - 1) & (o == 1))
    def _():
        o_ref[...] = (short_conv(2) * (acc_s[...] + bias_ref[1:2, :] * zf_s[...])).astype(o_ref.dtype)


def hyena_mixer(proj, conv_w, conv_b, hy_bias, hr, hi, fwd, inv):
    b, s, n = proj.shape
    hw = HYENA_WIDTH
    kc = s // HY_CHUNKS
    return pl.pallas_call(
        functools.partial(_hyena_kernel, seq=s),
        out_shape=jax.ShapeDtypeStruct((b, s, hw), BF16),
        grid=(b, 2, HY_CHUNKS),
        in_specs=[
            pl.BlockSpec((None, s, 3 * hw), lambda bi, o, c: (bi, 0, COL_HY // (3 * hw))),
            pl.BlockSpec((3, 3 * hw), lambda bi, o, c: (0, 0)),
            pl.BlockSpec((1, 3 * hw), lambda bi, o, c: (0, 0)),
            pl.BlockSpec((2, hw), lambda bi, o, c: (0, 0)),
            pl.BlockSpec((None, kc, hw), lambda bi, o, c: (o, c, 0)),
            pl.BlockSpec((None, kc, hw), lambda bi, o, c: (o, c, 0)),
            pl.BlockSpec((None, 2 * kc, s), lambda bi, o, c: (c, 0, 0)),
            pl.BlockSpec((None, s, 2 * kc), lambda bi, o, c: (c, 0, 0)),
        ],
        out_specs=pl.BlockSpec((None, s, hw), lambda bi, o, c: (bi, 0, 0)),
        scratch_shapes=[pltpu.VMEM((s, hw), BF16), pltpu.VMEM((s, hw), F32), pltpu.VMEM((s, hw), F32)],
        compiler_params=_cparams(("parallel", "arbitrary", "arbitrary")),
        name="hyena_conv",
    )(proj, conv_w, conv_b.reshape(1, -1), hy_bias, hr, hi, fwd, inv)


def _layer_norm(y, g, b):
    mu = jnp.mean(y, axis=-1, keepdims=True)
    yc = y - mu
    var = jnp.mean(yc * yc, axis=-1, keepdims=True)
    return yc * lax.rsqrt(var + LN_EPS) * g + b


def _merge_kernel(*refs):
    att = refs[:12]
    (yb_ref, yc_ref, gl_ref, x_ref, g1_ref, sc2_ref, sh2_ref, lng_ref, lnb_ref, wa_ref, wb_ref, wc_ref, wo_ref,
     xo_ref, u2_ref) = refs[12:]
    d = D_MODEL
    halves = []
    for half in range(2):
        la, lb, lc = (att[4 * g + 2 + half][...] for g in range(3))
        m = jnp.maximum(jnp.maximum(la, lb), lc)
        ea, eb, ec = jnp.exp(la - m), jnp.exp(lb - m), jnp.exp(lc - m)
        inv = 1.0 / (ea + eb + ec)
        halves.append((ea * inv) * att[half][...] + (eb * inv) * att[4 + half][...] + (ec * inv) * att[8 + half][...])
    ya = jnp.concatenate(halves, axis=-1)
    za = jnp.dot(ya.astype(BF16), wa_ref[...], preferred_element_type=F32)
    zb = jnp.dot(yb_ref[...], wb_ref[...], preferred_element_type=F32)
    zc = jnp.dot(yc_ref[...], wc_ref[...], preferred_element_type=F32)
    merged = (jax.nn.sigmoid(gl_ref[:, 0:d].astype(F32)) * za
              + jax.nn.sigmoid(gl_ref[:, d:2 * d].astype(F32)) * zb
              + jax.nn.sigmoid(gl_ref[:, 2 * d:3 * d].astype(F32)) * zc)
    mix = jnp.dot(merged.astype(BF16), wo_ref[...], preferred_element_type=F32)
    xn = _layer_norm(ALPHA * x_ref[...] + g1_ref[...] * mix, lng_ref[...], lnb_ref[...])
    xo_ref[...] = xn
    u2_ref[...] = (xn * (1.0 + sc2_ref[...]) + sh2_ref[...]).T.astype(BF16)


def merge_mixers(att, yb, yc, proj, x, g1, sc2, sh2, ln_g, ln_b, wa, wb, wc, wo):
    b, s, d = x.shape
    tm = 512
    row = lambda w: pl.BlockSpec((None, tm, w), lambda bi, i: (bi, i, 0))
    per_b = pl.BlockSpec((None, 1, d), lambda bi, i: (bi, 0, 0))
    const = lambda r, c: pl.BlockSpec((r, c), lambda bi, i: (0, 0))
    return pl.pallas_call(
        _merge_kernel,
        out_shape=(jax.ShapeDtypeStruct((b, s, d), F32), jax.ShapeDtypeStruct((d, b * s), BF16)),
        grid=(b, s // tm),
        in_specs=[row(128)] * 12 + [row(512), row(512), row(3 * d), row(d), per_b, per_b, per_b,
                                    const(1, d), const(1, d), const(256, d), const(512, d), const(512, d),
                                    const(d, d)],
        out_specs=(row(d), pl.BlockSpec((d, tm), lambda bi, i: (0, bi * (s // tm) + i))),
        compiler_params=_cparams(("parallel", "parallel")),
        name="merge_mixers",
    )(*att, yb, yc, proj, x, g1, sc2, sh2, ln_g.reshape(1, d), ln_b.reshape(1, d), wa, wb, wc, wo)


_PEER_PAIRS = [(i, j) for i in range(1, PEER_TOPK + 1) for j in range(1, PEER_TOPK + 1) if i * j <= PEER_TOPK]
PEER_NOT_TOP = 100.0


def _peer_route_kernel(u_ref, wq_ref, keys_ref, e1_ref, cut_ref, e2_ref, r2_ref, q_s, s_s, ab_s, st_s, *, tr):
    nh, nk, topk = PEER_HEADS, PEER_KEYS, PEER_TOPK
    qt = jnp.dot(wq_ref[...], u_ref[...], preferred_element_type=F32)
    q_s[...] = qt.astype(BF16)
    for ph in range(2 * nh):
        s_s[ph] = jnp.dot(keys_ref[ph], q_s[ph * 128:(ph + 1) * 128, :], preferred_element_type=F32)

    def extract(h, carry):
        def rnd(r, prev):
            cur = []
            for p in range(2):
                sv = s_s[p * nh + h]
                cur.append(jnp.max(jnp.where(sv < prev[p], sv, -jnp.inf), axis=0, keepdims=True))
                ab_s[p, r, pl.ds(h, 1), :] = cur[p]
            return tuple(cur)

        top = jnp.full((1, tr), jnp.inf, F32)
        lax.fori_loop(0, topk, rnd, (top, top))
        return carry

    lax.fori_loop(0, nh, extract, 0)

    for ch in range(tr // 128):
        ln = slice(ch * 128, (ch + 1) * 128)
        a = [ab_s[0, r, :, ln] for r in range(topk)]
        b = [ab_s[1, r, :, ln] for r in range(topk)]
        cand = [a[i - 1] + b[j - 1] for (i, j) in _PEER_PAIRS]
        tau = jnp.full((nh, 128), -jnp.inf, F32)
        for x, (ix, jx) in enumerate(_PEER_PAIRS):
            cnt = jnp.zeros((nh, 128), F32)
            for y, (iy, jy) in enumerate(_PEER_PAIRS):
                if iy <= ix and jy <= jx:
                    cnt = cnt + 1.0
                elif iy >= ix and jy >= jx:
                    continue
                else:
                    cnt = cnt + jnp.where(cand[y] >= cand[x], 1.0, 0.0)
            tau = jnp.maximum(tau, jnp.where(cnt >= float(topk), cand[x], -jnp.inf))
        top = cand[0]
        z = jnp.zeros((nh, 128), F32)
        for cx in cand:
            z = z + jnp.where(cx >= tau, jnp.exp(cx - top), 0.0)
        st_s[0, :, ln] = tau
        st_s[1, :, ln] = 1.0 / z

    for h in range(nh):
        rows = slice(h * nk, (h + 1) * nk)
        s1 = s_s[h]
        s2 = s_s[nh + h]
        hrow = slice(h, h + 1)
        tau = st_s[0, hrow, :]
        in1 = s1 >= ab_s[0, topk - 1, hrow, :]
        in2 = s2 >= ab_s[1, topk - 1, hrow, :]
        cut = jnp.zeros_like(s1)
        rank = jnp.ones_like(s2)
        for r in range(topk):
            br = ab_s[1, r, hrow, :]
            cut = cut + jnp.where(s1 + br >= tau, 1.0, 0.0)
            rank = rank + jnp.where(br > s2, 1.0, 0.0)
        e1_ref[rows, :] = jnp.where(in1, jnp.exp(s1 - ab_s[0, 0, hrow, :]), 0.0)
        cut_ref[rows, :] = jnp.where(in1, cut, 0.0)
        e2_ref[rows, :] = jnp.where(in2, jnp.exp(s2 - ab_s[1, 0, hrow, :]) * st_s[1, hrow, :], 0.0).astype(BF16)
        r2_ref[rows, :] = jnp.where(in2, rank, PEER_NOT_TOP).astype(BF16)


def peer_route(u2, wq_t, keys):
    d, t = u2.shape
    tr = 512
    nrow = PEER_HEADS * PEER_KEYS
    tab = lambda dt: jax.ShapeDtypeStruct((nrow, t), dt)
    col = pl.BlockSpec((nrow, tr), lambda i: (0, i))
    return pl.pallas_call(
        functools.partial(_peer_route_kernel, tr=tr),
        out_shape=(tab(F32), tab(F32), tab(BF16), tab(BF16)),
        grid=(t // tr,),
        in_specs=[
            pl.BlockSpec((d, tr), lambda i: (0, i)),
            pl.BlockSpec(wq_t.shape, lambda i: (0, 0)),
            pl.BlockSpec(keys.shape, lambda i: (0, 0, 0)),
        ],
        out_specs=(col, col, col, col),
        scratch_shapes=[
            pltpu.VMEM((2 * nrow, tr), BF16),
            pltpu.VMEM((2 * PEER_HEADS, PEER_KEYS, tr), F32),
            pltpu.VMEM((2, PEER_TOPK, PEER_HEADS, tr), F32),
            pltpu.VMEM((2, PEER_HEADS, tr), F32),
        ],
        compiler_params=_cparams(("parallel",)),
        name="peer_route",
    )(u2, wq_t, keys)


F8 = jnp.float8_e4m3fn
F8_MAX = 448.0
WG_SCALE = 16.0


def _peer_dense_kernel(sc_ref, x_ref, u_ref, vt_ref, e1_ref, cut_ref, e2_ref, r2_ref, xres_ref, g2_ref, lng_ref,
                       lnb_ref, o_ref, acc_s, wga_s, wgb_s, *, te, n_e):
    j = pl.program_id(1)
    nk = PEER_KEYS
    tn = x_ref.shape[1]
    sub = 16

    @pl.when(j == 0)
    def _():
        acc_s[...] = jnp.zeros_like(acc_s)
        wgb_s[...] = jnp.zeros_like(wgb_s)

    def step(w_cur, w_prev):
        jj = jnp.minimum(j, n_e - 1)
        inv_a = sc_ref[0]
        ts = 512
        for sl in range(tn // ts):
            lanes = slice(sl * ts, (sl + 1) * ts)
            for ii in range(te // nk):
                rows = slice(ii * nk, (ii + 1) * nk)
                at = (jnp.dot(u_ref[rows, :], x_ref[:, lanes], preferred_element_type=F32) * inv_a).astype(BF16)
                if ii == 1:
                    acc_s[:, lanes] += jnp.dot(vt_ref[...], w_prev[:, lanes], preferred_element_type=F32)
                g = (0.5 * WG_SCALE * at) * (1.0 + lax.erf(at * (2.0 ** -0.5)))
                i = jj * (te // nk) + ii
                w = None
                for h in range(PEER_HEADS):
                    cb = jnp.broadcast_to(cut_ref[pl.ds(h * nk + i, 1), lanes], (sub, ts)).astype(BF16)[None]
                    e1 = jnp.broadcast_to(e1_ref[pl.ds(h * nk + i, 1), lanes], (sub, ts)).astype(BF16)[None]
                    r2 = r2_ref[h * nk:(h + 1) * nk, lanes].reshape(nk // sub, sub, ts)
                    e2 = e2_ref[h * nk:(h + 1) * nk, lanes].reshape(nk // sub, sub, ts)
                    sel = jnp.where(r2 <= cb, e2, jnp.zeros((), BF16)) * e1
                    w = sel if w is None else w + sel
                wg = jnp.clip(g * w.reshape(nk, ts), -F8_MAX, F8_MAX)
                w_cur[rows, lanes] = wg.astype(F8)

    @pl.when((j % 2 == 0) & (j < n_e))
    def _():
        step(wga_s, wgb_s)

    @pl.when(j % 2 == 1)
    def _():
        step(wgb_s, wga_s)

    @pl.when(j == n_e)
    def _():
        acc = acc_s[...] + jnp.dot(vt_ref[...], wgb_s[...], preferred_element_type=F32)
        ffn = (acc * sc_ref[1]).T
        o_ref[...] = _layer_norm(ALPHA * xres_ref[...] + g2_ref[...] * ffn, lng_ref[...], lnb_ref[...])


def peer_dense(scales, x8, u_tab, vt_tab, e1, cut, e2, r2, x_res, g2, ln_g, ln_b):
    d, t = x8.shape
    b, s, _ = x_res.shape
    ne = u_tab.shape[0]
    tn, te = 1024, 512
    n_e = ne // te
    nrow = PEER_HEADS * PEER_KEYS
    col = pl.BlockSpec((nrow, tn), lambda i, j: (0, i))
    row = pl.BlockSpec((None, tn, d), lambda i, j: (i // (s // tn), i % (s // tn), 0))
    const = pl.BlockSpec((1, d), lambda i, j: (0, 0))
    return pl.pallas_call(
        functools.partial(_peer_dense_kernel, te=te, n_e=n_e),
        out_shape=jax.ShapeDtypeStruct((b, s, d), F32),
        grid=(t // tn, n_e + 1),
        in_specs=[
            pl.BlockSpec(memory_space=pltpu.SMEM),
            pl.BlockSpec((d, tn), lambda i, j: (0, i)),
            pl.BlockSpec((te, d), lambda i, j: (jnp.minimum(j, n_e - 1), 0)),
            pl.BlockSpec((d, te), lambda i, j: (0, jnp.maximum(j - 1, 0))),
            col, col, col, col,
            row, pl.BlockSpec((None, 1, d), lambda i, j: (i // (s // tn), 0, 0)), const, const,
        ],
        out_specs=row,
        scratch_shapes=[pltpu.VMEM((d, tn), F32), pltpu.VMEM((te, tn), F8), pltpu.VMEM((te, tn), F8)],
        compiler_params=_cparams(("parallel", "arbitrary")),
        name="peer_dense",
    )(scales, x8, u_tab, vt_tab, e1, cut, e2, r2, x_res, g2, ln_g.reshape(1, d), ln_b.reshape(1, d))


def rope_tables(seq):
    pos = jnp.arange(seq, dtype=F32)
    inv = ROPE_THETA ** (-jnp.arange(0, HEAD_DIM, 2, dtype=F32) / HEAD_DIM)
    ang = pos[:, None] * inv[None, :]
    sign = jnp.where((jnp.arange(128) % 64) < 32, -1.0, 1.0).astype(F32)
    return jnp.tile(jnp.cos(ang), (1, 4)), jnp.tile(jnp.sin(ang), (1, 4)) * sign[None, :]


def _swa_head_perm():
    heads = [h for p in range(4) for h in (p, 4 + p)]
    return jnp.concatenate([jnp.arange(h * HEAD_DIM, (h + 1) * HEAD_DIM) for h in heads])


def relayout_w_in(w):
    qa, ka, va = w[:, 0:768], w[:, 768:1536], w[:, 1536:2304]
    hy = w[:, 2304:3840]
    qc, kvc = w[:, 3840:4352], w[:, 4352:4608]
    gates = w[:, 4608:7680]
    dil = [jnp.concatenate([t[:, 256 * g:256 * (g + 1)] for t in (qa, ka, va)], axis=1) for g in range(3)]
    main = jnp.concatenate([gates, hy, dil[0], qc[:, _swa_head_perm()], kvc], axis=1).astype(BF16)
    return main, jnp.concatenate(dil[1:], axis=1).astype(BF16)


def residue_major(tab, dil):
    s = tab.shape[0]
    return tab.reshape(s // dil, dil, 128).transpose(1, 0, 2).reshape(s, 128)


def token_mixer(x, ada, w_main, w_dil, conv_w, conv_b, hy_bias, attn_sink, hr, hi, fwd, inv, rope_tabs,
                wa, wb, wc, wo, ln_g, ln_b):
    b, s, _ = x.shape
    sh1, sc1, g1, sh2, sc2, _ = [a[:, None, :] for a in jnp.split(ada, 6, axis=-1)]
    proj = in_proj(x, sc1, sh1, w_main)
    qkv1, qkv2 = in_proj_dil(x, sc1, sh1, w_dil)
    att = (dilated_attention(proj.reshape(b, 1, s, MAIN_WIDTH), COL_DIL0 // QKV_W, *rope_tabs[0])
           + dilated_attention(qkv1, 0, *rope_tabs[1]) + dilated_attention(qkv2, 0, *rope_tabs[2]))
    yb = hyena_mixer(proj, conv_w, conv_b, hy_bias, hr, hi, fwd, inv)
    yc = swa_attention(proj, *rope_tabs[0], attn_sink)
    return merge_mixers(att, yb, yc, proj, x, g1, sc2, sh2, ln_g, ln_b, wa, wb, wc, wo)


def peer_sublayer(u2, wq, keys, u_tab, v_tab, x_res, g2, ln_g, ln_b):
    d = u2.shape[0]
    nh, nk = PEER_HEADS, PEER_KEYS
    wq_t = wq.reshape(d, nh, 2, nk).transpose(2, 1, 3, 0).reshape(2 * nh * nk, d).astype(BF16)
    keys_r = keys.transpose(1, 0, 2, 3).reshape(2 * nh, nk, keys.shape[-1]).astype(BF16)
    tabs = peer_route(u2, wq_t, keys_r)
    su, sv, sx = _pow2_scale(u_tab), _pow2_scale(v_tab), _pow2_scale(u2)
    scales = jnp.stack([1.0 / (su * sx), 1.0 / (sv * WG_SCALE)]).astype(F32)
    x8 = (u2.astype(F32) * sx).astype(F8)
    return peer_dense(scales, x8, (u_tab * su).astype(F8), (v_tab.T * sv).astype(F8), *tabs, x_res, g2, ln_g, ln_b)


def _pow2_scale(a):
    m = jnp.maximum(jnp.max(jnp.abs(a)).astype(F32), 2.0 ** -100)
    return jnp.exp2(jnp.floor(jnp.log2(F8_MAX / m)))


def kernel(x, c, w_ada, b_ada, w_in, conv_w, conv_b, hy_w1, hy_b1, hy_w2, hy_b2, hy_w3, hy_freq, hy_log_decay,
           hy_bias, attn_sink, w_branch_a, w_branch_b, w_branch_c, w_out, ln_g, ln_b, peer_wq, peer_keys, peer_u,
           peer_v):
    b, s, d = x.shape
    depth = w_in.shape[0]
    hw = HYENA_WIDTH
    cos_t, sin_t = rope_tables(s)
    fwd, inv = dft_tables(s)
    ada = ada_ln(c, w_ada, b_ada)
    w3p = hy_w3.reshape(depth, -1, 2, 2, hw).transpose(0, 1, 3, 2, 4).reshape(depth, -1, 4 * hw)
    ldp = hy_log_decay.reshape(depth, 2, 2, hw).transpose(0, 2, 1, 3).reshape(depth, 4 * hw)
    hr, hi = hyena_filters(hy_w1, hy_b1, hy_w2, hy_b2, w3p, hy_freq, ldp, fwd, s)
    perm = _swa_head_perm()
    rope_tabs = [(residue_major(cos_t, dil), residue_major(sin_t, dil)) for _, dil in DIL_GROUPS]
    for l in range(depth):
        x1, u2 = token_mixer(x, ada[l], *relayout_w_in(w_in[l]), conv_w[l], conv_b[l], hy_bias[l], attn_sink[l],
                             hr[l], hi[l], fwd, inv, rope_tabs, w_branch_a[l].astype(BF16),
                             w_branch_b[l].astype(BF16), w_branch_c[l][perm].astype(BF16), w_out[l].astype(BF16),
                             ln_g[l, 0], ln_b[l, 0])
        x = peer_sublayer(u2, peer_wq[l], peer_keys[l], peer_u[l], peer_v[l], x1, ada[l][:, None, 5 * d:6 * d],
                          ln_g[l, 1], ln_b[l, 1])
    return x
```

```python
import functools
import math

import jax
import jax.numpy as jnp
from jax import lax
from jax.experimental import pallas as pl
from jax.experimental.pallas import tpu as pltpu

F32 = jnp.float32
BF16 = jnp.bfloat16

D_MODEL = 1024
HEAD_DIM = 64
DIL_GROUPS = ((128, 1), (512, 4), (2048, 16))
DIL_RADIUS = 64
HYENA_WIDTH = 512
HYENA_BANDS = 16
SWA_RADIUS = 128
SWA_Q_HEADS = 8
SWA_KV_HEADS = 2
PEER_HEADS = 8
PEER_KEYS = 128
PEER_TOPK = 16
DEPTH = 2
ALPHA = (2 * DEPTH) ** 0.25
LN_EPS = 1e-5
NEG_INF = -1e30
ROPE_THETA = 10000.0

COL_GATES = 0
COL_HY = 3072
COL_DIL0 = 4608
COL_SWA = 5376
MAIN_WIDTH = 6144
QKV_W = 768

VMEM_LIMIT = 56 * 1024 * 1024


def _cparams(sem, vmem=VMEM_LIMIT):
    return pltpu.CompilerParams(dimension_semantics=sem, vmem_limit_bytes=vmem)


def _ada_kernel(c_ref, w_ref, b_ref, o_ref):
    c = c_ref[...]
    cond = (c * jax.nn.sigmoid(c)).astype(BF16)
    o_ref[...] = jnp.dot(cond, w_ref[...].astype(BF16), preferred_element_type=F32) + b_ref[...]


def ada_ln(c, w_ada, b_ada):
    depth, d, n = w_ada.shape
    b = c.shape[0]
    tn = 1536
    return pl.pallas_call(
        _ada_kernel,
        out_shape=jax.ShapeDtypeStruct((depth, b, n), F32),
        grid=(depth, n // tn),
        in_specs=[
            pl.BlockSpec((b, d), lambda l, j: (0, 0)),
            pl.BlockSpec((None, d, tn), lambda l, j: (l, 0, j)),
            pl.BlockSpec((None, 1, tn), lambda l, j: (l, 0, j)),
        ],
        out_specs=pl.BlockSpec((None, b, tn), lambda l, j: (l, 0, j)),
        compiler_params=_cparams(("arbitrary", "arbitrary")),
        name="ada_ln",
    )(c, w_ada, b_ada.reshape(depth, 1, n))


def _inproj_kernel(x_ref, sc_ref, sh_ref, w_ref, o_ref, u_ref):
    @pl.when(pl.program_id(2) == 0)
    def _():
        u_ref[...] = (x_ref[...] * (1.0 + sc_ref[...]) + sh_ref[...]).astype(BF16)

    o_ref[...] = jnp.dot(u_ref[...], w_ref[...], preferred_element_type=F32).astype(o_ref.dtype)


def in_proj(x, sc, sh, w):
    b, s, d = x.shape
    n = w.shape[1]
    tm, tn = 1024, 1536
    return pl.pallas_call(
        _inproj_kernel,
        out_shape=jax.ShapeDtypeStruct((b, s, n), BF16),
        grid=(b, s // tm, n // tn),
        in_specs=[
            pl.BlockSpec((None, tm, d), lambda bi, i, j: (bi, i, 0)),
            pl.BlockSpec((None, 1, d), lambda bi, i, j: (bi, 0, 0)),
            pl.BlockSpec((None, 1, d), lambda bi, i, j: (bi, 0, 0)),
            pl.BlockSpec((d, tn), lambda bi, i, j: (0, j)),
        ],
        out_specs=pl.BlockSpec((None, tm, tn), lambda bi, i, j: (bi, i, j)),
        scratch_shapes=[pltpu.VMEM((tm, d), BF16)],
        compiler_params=_cparams(("parallel", "parallel", "arbitrary")),
        name="in_proj",
    )(x, sc, sh, w)


def _inproj_dil_kernel(x_ref, sc_ref, sh_ref, w_ref, o1_ref, o2_ref, res_s):
    tm = x_ref.shape[0]
    u = (x_ref[...] * (1.0 + sc_ref[...]) + sh_ref[...]).astype(BF16)
    res = jnp.dot(u, w_ref[...], preferred_element_type=F32)
    nslab = QKV_W // 128
    for c in range(2 * nslab):
        res_s[c] = res[:, c * 128:(c + 1) * 128]
    for g, o_ref in ((0, o1_ref), (1, o2_ref)):
        dil = o_ref.shape[0]
        for r in range(dil):
            for c in range(nslab):
                o_ref[r, :, c * 128:(c + 1) * 128] = res_s[g * nslab + c, pl.ds(r, tm // dil, stride=dil), :].astype(
                    o_ref.dtype)


def in_proj_dil(x, sc, sh, w):
    b, s, d = x.shape
    tm = 1024
    d1, d2 = DIL_GROUPS[1][1], DIL_GROUPS[2][1]
    return pl.pallas_call(
        _inproj_dil_kernel,
        out_shape=(jax.ShapeDtypeStruct((b, d1, s // d1, QKV_W), BF16),
                   jax.ShapeDtypeStruct((b, d2, s // d2, QKV_W), BF16)),
        grid=(b, s // tm),
        in_specs=[
            pl.BlockSpec((None, tm, d), lambda bi, i: (bi, i, 0)),
            pl.BlockSpec((None, 1, d), lambda bi, i: (bi, 0, 0)),
            pl.BlockSpec((None, 1, d), lambda bi, i: (bi, 0, 0)),
            pl.BlockSpec((d, 2 * QKV_W), lambda bi, i: (0, 0)),
        ],
        out_specs=(
            pl.BlockSpec((None, d1, tm // d1, QKV_W), lambda bi, i: (bi, 0, i, 0)),
            pl.BlockSpec((None, d2, tm // d2, QKV_W), lambda bi, i: (bi, 0, i, 0)),
        ),
        scratch_shapes=[pltpu.VMEM((2 * QKV_W // 128, tm, 128), F32)],
        compiler_params=_cparams(("parallel", "parallel")),
        name="in_proj_dil",
    )(x, sc, sh, w)


def _rope2(x, cos, sin_signed, lo):
    xr = jnp.where(lo, pltpu.roll(x, 96, 1), pltpu.roll(x, 32, 1))
    return x * cos + xr * sin_signed


def _lane_masks():
    lane = lax.broadcasted_iota(jnp.int32, (1, 128), 1)
    return (lane % 64) < 32, lane < 64


def _band_mask(q0, nq, nk, pad, radius, length):
    qpos = q0 + (lax.broadcasted_iota(jnp.int32, (nq, 1), 0) & 127)
    kpos = q0 - pad + lax.broadcasted_iota(jnp.int32, (1, nk), 1)
    kpos = jnp.where(kpos < 0, -4 * length, jnp.where(kpos >= length, -4 * length, kpos))
    return jnp.abs(qpos - kpos) <= radius


def _dil_attn_kernel(qkv_ref, cos_ref, sin_ref, olo_ref, ohi_ref, llo_ref, lhi_ref, q_s, k_s, v_s, *, dil, ls):
    lo, head0 = _lane_masks()
    heads = (head0, jnp.logical_not(head0))
    cos = cos_ref[...]
    sin = sin_ref[...]
    pad = DIL_RADIUS
    qb = 128
    nblk = ls // qb
    ngroup = 8
    zeros = jnp.zeros((dil, pad, 128), BF16)
    k_s[:, 0:pad, :] = zeros
    k_s[:, pad + ls:pad + ls + pad, :] = zeros
    v_s[:, 0:pad, :] = zeros
    v_s[:, pad + ls:pad + ls + pad, :] = zeros
    for hp, (o_ref, l_ref) in enumerate(((olo_ref, llo_ref), (ohi_ref, lhi_ref))):
        q = qkv_ref[:, :, hp * 128:(hp + 1) * 128].astype(F32).reshape(dil * ls, 128)
        k = qkv_ref[:, :, 256 + hp * 128:256 + (hp + 1) * 128].astype(F32).reshape(dil * ls, 128)
        q_s[...] = (_rope2(q, cos, sin, lo) * (HEAD_DIM ** -0.5)).astype(BF16)
        k_s[:, pad:pad + ls, :] = _rope2(k, cos, sin, lo).astype(BF16).reshape(dil, ls, 128)
        v_s[:, pad:pad + ls, :] = qkv_ref[:, :, 512 + hp * 128:512 + (hp + 1) * 128]

        def grp(gi, carry):
            tiles = []
            for t in range(ngroup):
                c = gi * ngroup + t
                r = c // nblk
                q0 = pl.multiple_of((c % nblk) * qb, qb)
                qblk = q_s[pl.ds(pl.multiple_of(c * qb, qb), qb), :]
                kw = k_s[r, pl.ds(q0, qb + 2 * pad), :]
                vw = v_s[r, pl.ds(q0, qb + 2 * pad), :]
                mask = _band_mask(q0, qb, qb + 2 * pad, pad, DIL_RADIUS, ls)
                ss = [lax.dot_general(jnp.where(hm, qblk, jnp.zeros_like(qblk)), kw, (((1,), (1,)), ((), ())),
                                      preferred_element_type=F32) for hm in heads]
                tiles.append((r, q0, vw, mask, ss))
            soft = []
            for r, q0, vw, mask, ss in tiles:
                for s in ss:
                    s = jnp.where(mask, s, NEG_INF)
                    m = jnp.max(s, axis=-1, keepdims=True)
                    p = jnp.exp(s - m)
                    soft.append((p.astype(BF16), jnp.sum(p, axis=-1, keepdims=True), m))
            for ti, (r, q0, vw, mask, ss) in enumerate(tiles):
                res = []
                for h in range(2):
                    p, den, m = soft[2 * ti + h]
                    res.append((jnp.dot(p, vw, preferred_element_type=F32) / den, m + jnp.log(den)))
                o = jnp.where(head0, res[0][0], res[1][0])
                lse = jnp.where(head0, res[0][1], res[1][1])
                rows = pl.ds(q0, qb) if dil == 1 else pl.ds(q0 * dil + r, qb, stride=dil)
                o_ref[rows, :] = o
                l_ref[rows, :] = lse
            return carry

        lax.fori_loop(0, dil * nblk // ngroup, grp, 0)


def dilated_attention(qkv, col_block, cos_d, sin_d):
    b, dil, ls, _ = qkv.shape
    s = dil * ls
    out_sds = jax.ShapeDtypeStruct((b, s, 128), F32)
    out_spec = pl.BlockSpec((None, s, 128), lambda bi: (bi, 0, 0))
    return pl.pallas_call(
        functools.partial(_dil_attn_kernel, dil=dil, ls=ls),
        out_shape=(out_sds,) * 4,
        grid=(b,),
        in_specs=[
            pl.BlockSpec((None, dil, ls, QKV_W), lambda bi: (bi, 0, 0, col_block)),
            pl.BlockSpec((s, 128), lambda bi: (0, 0)),
            pl.BlockSpec((s, 128), lambda bi: (0, 0)),
        ],
        out_specs=(out_spec,) * 4,
        scratch_shapes=[
            pltpu.VMEM((s, 128), BF16),
            pltpu.VMEM((dil, ls + 2 * DIL_RADIUS, 128), BF16),
            pltpu.VMEM((dil, ls + 2 * DIL_RADIUS, 128), BF16),
        ],
        compiler_params=_cparams(("parallel",)),
        name=f"dil_attn_d{dil}",
    )(qkv, cos_d, sin_d)


def _swa_kernel(qkv_ref, cos_ref, sin_ref, sink_ref, o_ref, q_s, k_s, v_s, *, s_len):
    lo, head0 = _lane_masks()
    heads = (head0, jnp.logical_not(head0))
    cos = cos_ref[...]
    sin = sin_ref[...]
    pad = SWA_RADIUS
    qb = 128
    grp = SWA_Q_HEADS // SWA_KV_HEADS
    zeros = jnp.zeros((pad, 128), BF16)
    k_s[0:pad, :] = zeros
    k_s[pad + s_len:pad + s_len + pad, :] = zeros
    v_s[0:pad, :] = zeros
    v_s[pad + s_len:pad + s_len + pad, :] = zeros
    k = qkv_ref[:, 512:640].astype(F32)
    k_s[pad:pad + s_len, :] = _rope2(k, cos, sin, lo).astype(BF16)
    v_s[pad:pad + s_len, :] = qkv_ref[:, 640:768]
    for qp in range(grp):
        q = qkv_ref[:, qp * 128:(qp + 1) * 128].astype(F32)
        q_s[qp] = (_rope2(q, cos, sin, lo) * (HEAD_DIM ** -0.5)).astype(BF16)
    sinks = [jnp.concatenate([jnp.broadcast_to(sink_ref[kv * grp + p:kv * grp + p + 1, 0:1], (qb, 1))
                              for p in range(grp)], axis=0) for kv in range(SWA_KV_HEADS)]

    def blk(i, carry):
        q0 = pl.multiple_of(i * qb, qb)
        kw = k_s[pl.ds(q0, qb + 2 * pad), :]
        vw = v_s[pl.ds(q0, qb + 2 * pad), :]
        qall = jnp.concatenate([q_s[p, pl.ds(q0, qb), :] for p in range(grp)], axis=0)
        mask = _band_mask(q0, grp * qb, qb + 2 * pad, pad, SWA_RADIUS, s_len)
        ss = [lax.dot_general(jnp.where(hm, qall, jnp.zeros_like(qall)), kw, (((1,), (1,)), ((), ())),
                              preferred_element_type=F32) for hm in heads]
        soft = []
        for s, sk in zip(ss, sinks):
            s = jnp.where(mask, s, NEG_INF)
            m = jnp.maximum(jnp.max(s, axis=-1, keepdims=True), sk)
            p = jnp.exp(s - m)
            soft.append((p.astype(BF16), jnp.sum(p, axis=-1, keepdims=True) + jnp.exp(sk - m)))
        res = [jnp.dot(p, vw, preferred_element_type=F32) / den for p, den in soft]
        o = jnp.where(head0, res[0], res[1]).astype(o_ref.dtype)
        for p in range(grp):
            o_ref[pl.ds(q0, qb), p * 128:(p + 1) * 128] = o[p * qb:(p + 1) * qb, :]
        return carry

    lax.fori_loop(0, s_len // qb, blk, 0, unroll=8)


def swa_attention(proj, cos_t, sin_t, sink):
    b, s, n = proj.shape
    return pl.pallas_call(
        functools.partial(_swa_kernel, s_len=s),
        out_shape=jax.ShapeDtypeStruct((b, s, 512), BF16),
        grid=(b,),
        in_specs=[
            pl.BlockSpec((None, s, QKV_W), lambda bi: (bi, 0, COL_SWA // QKV_W)),
            pl.BlockSpec((s, 128), lambda bi: (0, 0)),
            pl.BlockSpec((s, 128), lambda bi: (0, 0)),
            pl.BlockSpec((8, 128), lambda bi: (0, 0)),
        ],
        out_specs=pl.BlockSpec((None, s, 512), lambda bi: (bi, 0, 0)),
        scratch_shapes=[
            pltpu.VMEM((SWA_Q_HEADS // SWA_KV_HEADS, s, 128), BF16),
            pltpu.VMEM((s + 2 * SWA_RADIUS, 128), BF16),
            pltpu.VMEM((s + 2 * SWA_RADIUS, 128), BF16),
        ],
        compiler_params=_cparams(("parallel",)),
        name="swa_attn",
    )(proj, cos_t, sin_t, jnp.broadcast_to(sink.astype(F32)[:, None], (8, 128)))


HY_CHUNKS = 4


def dft_table(seq):
    kc = seq // HY_CHUNKS
    k = jnp.arange(seq, dtype=jnp.int32)
    phase = ((2 * k[:, None] + 1) * k[None, :]) % (4 * seq)
    ang = phase.astype(F32) * (2.0 * math.pi / (4 * seq))
    c = jnp.cos(ang).reshape(HY_CHUNKS, kc, seq)
    s = jnp.sin(ang).reshape(HY_CHUNKS, kc, seq)
    return jnp.concatenate([c, s], axis=1).astype(BF16)


def _hy_filter_kernel(w1_ref, b1_ref, w2_ref, b2_ref, w3_ref, fr_ref, ld_ref, bands_ref, f_ref,
                      hr_ref, hi_ref, hs_s, hd_s, *, seq):
    hw = HYENA_WIDTH
    hi_p = lax.Precision.HIGHEST

    @pl.when(pl.program_id(2) == 0)
    def _():
        idx = lax.broadcasted_iota(jnp.int32, (seq, 1), 0).astype(F32)
        t = idx / max(seq - 1, 1)
        w = 2.0 * math.pi * idx / seq
        ang = w * bands_ref[...]
        w1 = w1_ref[...]
        pre = (t * w1[0:1, :]
               + jnp.dot(jnp.cos(ang), w1[1:1 + HYENA_BANDS, :], precision=hi_p, preferred_element_type=F32)
               - jnp.dot(jnp.sin(ang), w1[1 + HYENA_BANDS:1 + 2 * HYENA_BANDS, :], precision=hi_p,
                         preferred_element_type=F32)
               + b1_ref[...])
        h = jnp.sin(fr_ref[0:1, :] * pre)
        h = jnp.sin(fr_ref[1:2, :] * (jnp.dot(h, w2_ref[...], precision=hi_p, preferred_element_type=F32)
                                      + b2_ref[...]))
        h = jnp.dot(h, w3_ref[...], precision=hi_p, preferred_element_type=F32)
        h = h * jnp.exp(-t * jnp.exp(ld_ref[...]))
        hf = h[:, :hw]
        hb = jnp.where(idx > 0.0, h[:, hw:], 0.0)
        inv = lax.rsqrt(jnp.sum(hf * hf + hb * hb, axis=0, keepdims=True) + 1e-12)
        hs_s[...] = ((hf + hb) * inv).astype(BF16)
        hd_s[...] = ((hf - hb) * inv).astype(BF16)

    kc = seq // HY_CHUNKS
    hr_ref[...] = jnp.dot(f_ref[0:kc, :], hs_s[...], preferred_element_type=F32)
    hi_ref[...] = -jnp.dot(f_ref[kc:2 * kc, :], hd_s[...], preferred_element_type=F32)


def hyena_filters(w1, b1, w2, b2, w3p, freq, ldp, fwd, seq):
    depth = w1.shape[0]
    kc = seq // HY_CHUNKS
    hw = HYENA_WIDTH
    bands = jnp.linspace(1e-4, HYENA_BANDS - 1, HYENA_BANDS, dtype=F32).reshape(1, HYENA_BANDS)
    full = lambda *shape: pl.BlockSpec((None,) + shape, lambda l, o, c: (l,) + (0,) * len(shape))
    out_sds = jax.ShapeDtypeStruct((depth, 2, seq, hw), F32)
    return pl.pallas_call(
        functools.partial(_hy_filter_kernel, seq=seq),
        out_shape=(out_sds, out_sds),
        grid=(depth, 2, HY_CHUNKS),
        in_specs=[
            full(*w1.shape[1:]), full(1, b1.shape[-1]), full(*w2.shape[1:]), full(1, b2.shape[-1]),
            pl.BlockSpec((None, w3p.shape[1], 2 * hw), lambda l, o, c: (l, 0, o)),
            full(*freq.shape[1:]),
            pl.BlockSpec((None, 1, 2 * hw), lambda l, o, c: (l, 0, o)),
            pl.BlockSpec((1, HYENA_BANDS), lambda l, o, c: (0, 0)),
            pl.BlockSpec((None, 2 * kc, seq), lambda l, o, c: (c, 0, 0)),
        ],
        out_specs=(
            pl.BlockSpec((None, None, kc, hw), lambda l, o, c: (l, o, c, 0)),
            pl.BlockSpec((None, None, kc, hw), lambda l, o, c: (l, o, c, 0)),
        ),
        scratch_shapes=[pltpu.VMEM((seq, hw), BF16), pltpu.VMEM((seq, hw), BF16)],
        compiler_params=_cparams(("arbitrary", "arbitrary", "arbitrary")),
        name="hyena_filters",
    )(w1, b1[:, None, :], w2, b2[:, None, :], w3p, freq, ldp[:, None, :], bands, fwd)


def dft_tables_r2(seq):
    half = seq // 2
    kc = half // HY_CHUNKS
    k = jnp.arange(half, dtype=jnp.int32)[:, None]
    n = jnp.arange(half, dtype=jnp.int32)[None, :]
    out = []
    for off in (0, 1):
        ang = (((2 * k + 1) * (2 * n + off)) % (4 * seq)).astype(F32) * (2.0 * math.pi / (4 * seq))
        c = jnp.cos(ang).reshape(HY_CHUNKS, kc, half)
        s = jnp.sin(ang).reshape(HY_CHUNKS, kc, half)
        out.append(jnp.concatenate([c, s], axis=1).astype(BF16))
        out.append((jnp.concatenate([c, -s], axis=1) * (1.0 / seq)).transpose(0, 2, 1).astype(BF16))
    return out[0], out[2], out[1], out[3]


def _hyena_kernel(hy_ref, cw_ref, cb_ref, bias_ref, hr_ref, hi_ref, hrp_ref, hip_ref, fe_ref, fo_ref, ge_ref, go_ref,
                  o_ref, zb_s, zf_s, acc_s, slab_s, *, seq):
    hw = HYENA_WIDTH
    half = seq // 2
    o = pl.program_id(1)
    c = pl.program_id(2)
    kc = half // HY_CHUNKS
    nslab = hw // 128

    def short_conv(part):
        x = hy_ref[:, part * hw:(part + 1) * hw].astype(F32)
        row = lax.broadcasted_iota(jnp.int32, (seq, 1), 0)
        xm = jnp.where(row == 0, 0.0, pltpu.roll(x, 1, 0))
        xp = jnp.where(row == seq - 1, 0.0, pltpu.roll(x, seq - 1, 0))
        w = cw_ref[:, part * hw:(part + 1) * hw]
        return cb_ref[:, part * hw:(part + 1) * hw] + xm * w[0:1, :] + x * w[1:2, :] + xp * w[2:3, :]

    def split(x):
        for sl in range(nslab):
            slab_s[sl] = x[:, sl * 128:(sl + 1) * 128]
        return tuple(jnp.concatenate([slab_s[sl, pl.ds(par, half, stride=2), :] for sl in range(nslab)], axis=-1)
                     for par in (0, 1))

    def interleave(xe, xo):
        for sl in range(nslab):
            slab_s[sl, pl.ds(0, half, stride=2), :] = xe[:, sl * 128:(sl + 1) * 128]
            slab_s[sl, pl.ds(1, half, stride=2), :] = xo[:, sl * 128:(sl + 1) * 128]
        return jnp.concatenate([slab_s[sl] for sl in range(nslab)], axis=-1)

    @pl.when((o == 0) & (c == 0))
    def _():
        for par, z in enumerate(split(short_conv(0))):
            zf_s[par] = z
            zb_s[par] = z.astype(BF16)

    ze = jnp.dot(fe_ref[...], zb_s[0], preferred_element_type=F32)
    zo = jnp.dot(fo_ref[...], zb_s[1], preferred_element_type=F32)
    zc, zs = ze[:kc] + zo[:kc], ze[kc:] + zo[kc:]
    zcp, zsp = ze[:kc] - zo[:kc], zo[kc:] - ze[kc:]
    hr, hi, hrp, hip = hr_ref[...], hi_ref[...], hrp_ref[...], hip_ref[...]
    yr, yi = zc * hr + zs * hi, zc * hi - zs * hr
    yrp, yip = zcp * hrp + zsp * hip, zcp * hip - zsp * hrp
    pe = jnp.dot(ge_ref[...], jnp.concatenate([yr + yrp, yi - yip], axis=0).astype(BF16),
                 preferred_element_type=F32)
    po = jnp.dot(go_ref[...], jnp.concatenate([yr - yrp, yi + yip], axis=0).astype(BF16),
                 preferred_element_type=F32)

    @pl.when(c == 0)
    def _():
        acc_s[0] = pe
        acc_s[1] = po

    @pl.when(c > 0)
    def _():
        acc_s[0] += pe
        acc_s[1] += po

    @pl.when((c == HY_CHUNKS - 1) & (o == 0))
    def _():
        for par, p1 in enumerate(split(short_conv(1))):
            z1 = p1 * (acc_s[par] + bias_ref[0:1, :] * zf_s[par])
            zf_s[par] = z1
            zb_s[par] = z1.astype(BF16)

    @pl.when((c == HY_CHUNKS - 1) & (o == 1))
    def _():
        halves = [p2 * (acc_s[par] + bias_ref[1:2, :] * zf_s[par]) for par, p2 in enumerate(split(short_conv(2)))]
        o_ref[...] = interleave(*halves).astype(o_ref.dtype)


def hyena_mixer(proj, conv_w, conv_b, hy_bias, hr, hi, tabs):
    b, s, n = proj.shape
    hw = HYENA_WIDTH
    half = s // 2
    kc = half // HY_CHUNKS
    nck = HY_CHUNKS
    spec = pl.BlockSpec((None, kc, hw), lambda bi, o, c: (o, c, 0))
    spec_p = pl.BlockSpec((None, kc, hw), lambda bi, o, c: (o, c, 0))
    fwd_spec = pl.BlockSpec((None, 2 * kc, half), lambda bi, o, c: (c, 0, 0))
    inv_spec = pl.BlockSpec((None, half, 2 * kc), lambda bi, o, c: (c, 0, 0))
    fe, fo, ge, go = tabs
    return pl.pallas_call(
        functools.partial(_hyena_kernel, seq=s),
        out_shape=jax.ShapeDtypeStruct((b, s, hw), BF16),
        grid=(b, 2, nck),
        in_specs=[
            pl.BlockSpec((None, s, 3 * hw), lambda bi, o, c: (bi, 0, COL_HY // (3 * hw))),
            pl.BlockSpec((3, 3 * hw), lambda bi, o, c: (0, 0)),
            pl.BlockSpec((1, 3 * hw), lambda bi, o, c: (0, 0)),
            pl.BlockSpec((2, hw), lambda bi, o, c: (0, 0)),
            spec, spec, spec_p, spec_p, fwd_spec, fwd_spec, inv_spec, inv_spec,
        ],
        out_specs=pl.BlockSpec((None, s, hw), lambda bi, o, c: (bi, 0, 0)),
        scratch_shapes=[pltpu.VMEM((2, half, hw), BF16), pltpu.VMEM((2, half, hw), F32),
                        pltpu.VMEM((2, half, hw), F32), pltpu.VMEM((hw // 128, s, 128), F32)],
        compiler_params=_cparams(("parallel", "arbitrary", "arbitrary")),
        name="hyena_conv",
    )(proj, conv_w, conv_b.reshape(1, -1), hy_bias, hr, hi, hr[:, ::-1], hi[:, ::-1], fe, fo, ge, go)


def _layer_norm(y, g, b):
    mu = jnp.mean(y, axis=-1, keepdims=True)
    yc = y - mu
    var = jnp.mean(yc * yc, axis=-1, keepdims=True)
    return yc * lax.rsqrt(var + LN_EPS) * g + b


def _merge_kernel(*refs):
    att = refs[:12]
    (yb_ref, yc_ref, gl_ref, x_ref, g1_ref, sc2_ref, sh2_ref, lng_ref, lnb_ref, wa_ref, wb_ref, wc_ref, wo_ref,
     xo_ref, u2_ref) = refs[12:]
    d = D_MODEL
    halves = []
    for half in range(2):
        la, lb, lc = (att[4 * g + 2 + half][...] for g in range(3))
        m = jnp.maximum(jnp.maximum(la, lb), lc)
        ea, eb, ec = jnp.exp(la - m), jnp.exp(lb - m), jnp.exp(lc - m)
        inv = 1.0 / (ea + eb + ec)
        halves.append((ea * inv) * att[half][...] + (eb * inv) * att[4 + half][...] + (ec * inv) * att[8 + half][...])
    ya = jnp.concatenate(halves, axis=-1)
    za = jnp.dot(ya.astype(BF16), wa_ref[...], preferred_element_type=F32)
    zb = jnp.dot(yb_ref[...], wb_ref[...], preferred_element_type=F32)
    zc = jnp.dot(yc_ref[...], wc_ref[...], preferred_element_type=F32)
    merged = (jax.nn.sigmoid(gl_ref[:, 0:d].astype(F32)) * za
              + jax.nn.sigmoid(gl_ref[:, d:2 * d].astype(F32)) * zb
              + jax.nn.sigmoid(gl_ref[:, 2 * d:3 * d].astype(F32)) * zc)
    mix = jnp.dot(merged.astype(BF16), wo_ref[...], preferred_element_type=F32)
    xn = _layer_norm(ALPHA * x_ref[...] + g1_ref[...] * mix, lng_ref[...], lnb_ref[...])
    xo_ref[...] = xn
    u2_ref[...] = (xn * (1.0 + sc2_ref[...]) + sh2_ref[...]).T.astype(BF16)


def merge_mixers(att, yb, yc, proj, x, g1, sc2, sh2, ln_g, ln_b, wa, wb, wc, wo):
    b, s, d = x.shape
    tm = 512
    row = lambda w: pl.BlockSpec((None, tm, w), lambda bi, i: (bi, i, 0))
    per_b = pl.BlockSpec((None, 1, d), lambda bi, i: (bi, 0, 0))
    const = lambda r, c: pl.BlockSpec((r, c), lambda bi, i: (0, 0))
    return pl.pallas_call(
        _merge_kernel,
        out_shape=(jax.ShapeDtypeStruct((b, s, d), F32), jax.ShapeDtypeStruct((d, b * s), BF16)),
        grid=(b, s // tm),
        in_specs=[row(128)] * 12 + [row(512), row(512), row(3 * d), row(d), per_b, per_b, per_b,
                                    const(1, d), const(1, d), const(256, d), const(512, d), const(512, d),
                                    const(d, d)],
        out_specs=(row(d), pl.BlockSpec((d, tm), lambda bi, i: (0, bi * (s // tm) + i))),
        compiler_params=_cparams(("parallel", "parallel")),
        name="merge_mixers",
    )(*att, yb, yc, proj, x, g1, sc2, sh2, ln_g.reshape(1, d), ln_b.reshape(1, d), wa, wb, wc, wo)


_PEER_PAIRS = [(i, j) for i in range(1, PEER_TOPK + 1) for j in range(1, PEER_TOPK + 1) if i * j <= PEER_TOPK]
PEER_NOT_TOP = 100.0


def _peer_route_kernel(u_ref, wq_ref, keys_ref, e1_ref, cut_ref, e2_ref, r2_ref, q_s, s_s, ab_s, st_s, *, tr):
    nh, nk, topk = PEER_HEADS, PEER_KEYS, PEER_TOPK
    qt = jnp.dot(wq_ref[...], u_ref[...], preferred_element_type=F32)
    q_s[...] = qt.astype(BF16)
    for ph in range(2 * nh):
        s_s[ph] = jnp.dot(keys_ref[ph], q_s[ph * 128:(ph + 1) * 128, :], preferred_element_type=F32)

    def extract(h, carry):
        def rnd(r, prev):
            cur = []
            for p in range(2):
                sv = s_s[p * nh + h]
                cur.append(jnp.max(jnp.where(sv < prev[p], sv, -jnp.inf), axis=0, keepdims=True))
                ab_s[p, r, pl.ds(h, 1), :] = cur[p]
            return tuple(cur)

        top = jnp.full((1, tr), jnp.inf, F32)
        lax.fori_loop(0, topk, rnd, (top, top))
        return carry

    lax.fori_loop(0, nh, extract, 0)

    for ch in range(tr // 128):
        ln = slice(ch * 128, (ch + 1) * 128)
        a = [ab_s[0, r, :, ln] for r in range(topk)]
        b = [ab_s[1, r, :, ln] for r in range(topk)]
        cand = [a[i - 1] + b[j - 1] for (i, j) in _PEER_PAIRS]
        tau = jnp.full((nh, 128), -jnp.inf, F32)
        for x, (ix, jx) in enumerate(_PEER_PAIRS):
            cnt = jnp.zeros((nh, 128), F32)
            for y, (iy, jy) in enumerate(_PEER_PAIRS):
                if iy <= ix and jy <= jx:
                    cnt = cnt + 1.0
                elif iy >= ix and jy >= jx:
                    continue
                else:
                    cnt = cnt + jnp.where(cand[y] >= cand[x], 1.0, 0.0)
            tau = jnp.maximum(tau, jnp.where(cnt >= float(topk), cand[x], -jnp.inf))
        top = cand[0]
        z = jnp.zeros((nh, 128), F32)
        for cx in cand:
            z = z + jnp.where(cx >= tau, jnp.exp(cx - top), 0.0)
        st_s[0, :, ln] = tau
        st_s[1, :, ln] = 1.0 / z

    for h in range(nh):
        rows = slice(h * nk, (h + 1) * nk)
        s1 = s_s[h]
        s2 = s_s[nh + h]
        hrow = slice(h, h + 1)
        tau = st_s[0, hrow, :]
        in1 = s1 >= ab_s[0, topk - 1, hrow, :]
        in2 = s2 >= ab_s[1, topk - 1, hrow, :]
        cut = jnp.zeros_like(s1)
        rank = jnp.ones_like(s2)
        for r in range(topk):
            br = ab_s[1, r, hrow, :]
            cut = cut + jnp.where(s1 + br >= tau, 1.0, 0.0)
            rank = rank + jnp.where(br > s2, 1.0, 0.0)
        e1_ref[rows, :] = jnp.where(in1, jnp.exp(s1 - ab_s[0, 0, hrow, :]), 0.0)
        cut_ref[rows, :] = jnp.where(in1, cut, 0.0)
        e2_ref[rows, :] = jnp.where(in2, jnp.exp(s2 - ab_s[1, 0, hrow, :]) * st_s[1, hrow, :], 0.0).astype(BF16)
        r2_ref[rows, :] = jnp.where(in2, rank, PEER_NOT_TOP).astype(BF16)


def peer_route(u2, wq_t, keys):
    d, t = u2.shape
    tr = 512
    nrow = PEER_HEADS * PEER_KEYS
    tab = lambda dt: jax.ShapeDtypeStruct((nrow, t), dt)
    col = pl.BlockSpec((nrow, tr), lambda i: (0, i))
    return pl.pallas_call(
        functools.partial(_peer_route_kernel, tr=tr),
        out_shape=(tab(F32), tab(F32), tab(BF16), tab(BF16)),
        grid=(t // tr,),
        in_specs=[
            pl.BlockSpec((d, tr), lambda i: (0, i)),
            pl.BlockSpec(wq_t.shape, lambda i: (0, 0)),
            pl.BlockSpec(keys.shape, lambda i: (0, 0, 0)),
        ],
        out_specs=(col, col, col, col),
        scratch_shapes=[
            pltpu.VMEM((2 * nrow, tr), BF16),
            pltpu.VMEM((2 * PEER_HEADS, PEER_KEYS, tr), F32),
            pltpu.VMEM((2, PEER_TOPK, PEER_HEADS, tr), F32),
            pltpu.VMEM((2, PEER_HEADS, tr), F32),
        ],
        compiler_params=_cparams(("parallel",)),
        name="peer_route",
    )(u2, wq_t, keys)


F8 = jnp.float8_e4m3fn
F8_MAX = 448.0
WG_SCALE = 16.0


def _peer_dense_kernel(sc_ref, x_ref, u_ref, vt_ref, e1_ref, cut_ref, e2_ref, r2_ref, xres_ref, g2_ref, lng_ref,
                       lnb_ref, o_ref, acc_s, wga_s, wgb_s, *, te, n_e):
    j = pl.program_id(1)
    nk = PEER_KEYS
    tn = x_ref.shape[1]
    sub = 16

    @pl.when(j == 0)
    def _():
        acc_s[...] = jnp.zeros_like(acc_s)
        wgb_s[...] = jnp.zeros_like(wgb_s)

    def step(w_cur, w_prev):
        inv_a = sc_ref[0]
        ts = 512
        for sl in range(tn // ts):
            lanes = slice(sl * ts, (sl + 1) * ts)
            for ii in range(te // nk):
                rows = slice(ii * nk, (ii + 1) * nk)
                at = (jnp.dot(u_ref[rows, :], x_ref[:, lanes], preferred_element_type=F32) * inv_a).astype(BF16)
                if ii == 1:
                    acc_s[:, lanes] += jnp.dot(vt_ref[...], w_prev[:, lanes], preferred_element_type=F32)
                g = (0.5 * WG_SCALE * at) * (1.0 + lax.erf(at * (2.0 ** -0.5)))
                w = None
                for h in range(PEER_HEADS):
                    cb = jnp.broadcast_to(cut_ref[h, ii:ii + 1, lanes], (sub, ts)).astype(BF16)[None]
                    e1 = jnp.broadcast_to(e1_ref[h, ii:ii + 1, lanes], (sub, ts)).astype(BF16)[None]
                    r2 = r2_ref[h * nk:(h + 1) * nk, lanes].reshape(nk // sub, sub, ts)
                    e2 = e2_ref[h * nk:(h + 1) * nk, lanes].reshape(nk // sub, sub, ts)
                    sel = jnp.where(r2 <= cb, e2, jnp.zeros((), BF16)) * e1
                    w = sel if w is None else w + sel
                wg = jnp.clip(g * w.reshape(nk, ts), -F8_MAX, F8_MAX)
                w_cur[rows, lanes] = wg.astype(F8)

    @pl.when((j % 2 == 0) & (j < n_e))
    def _():
        step(wga_s, wgb_s)

    @pl.when(j % 2 == 1)
    def _():
        step(wgb_s, wga_s)

    @pl.when(j == n_e)
    def _():
        acc = acc_s[...] + jnp.dot(vt_ref[...], wgb_s[...], preferred_element_type=F32)
        ffn = (acc * sc_ref[1]).T
        o_ref[...] = _layer_norm(ALPHA * xres_ref[...] + g2_ref[...] * ffn, lng_ref[...], lnb_ref[...])


def peer_dense(scales, x8, u_tab, vt_tab, e1, cut, e2, r2, x_res, g2, ln_g, ln_b):
    d, t = x8.shape
    b, s, _ = x_res.shape
    ne = u_tab.shape[0]
    tn, te = 1024, 1024
    n_e = ne // te
    nrow = PEER_HEADS * PEER_KEYS
    col = pl.BlockSpec((nrow, tn), lambda i, j: (0, i))
    step_rows = pl.BlockSpec((PEER_HEADS, te // PEER_KEYS, tn), lambda i, j: (0, jnp.minimum(j, n_e - 1), i))
    e1 = e1.reshape(PEER_HEADS, PEER_KEYS, t)
    cut = cut.reshape(PEER_HEADS, PEER_KEYS, t)
    row = pl.BlockSpec((None, tn, d), lambda i, j: (i // (s // tn), i % (s // tn), 0))
    const = pl.BlockSpec((1, d), lambda i, j: (0, 0))
    return pl.pallas_call(
        functools.partial(_peer_dense_kernel, te=te, n_e=n_e),
        out_shape=jax.ShapeDtypeStruct((b, s, d), F32),
        grid=(t // tn, n_e + 1),
        in_specs=[
            pl.BlockSpec(memory_space=pltpu.SMEM),
            pl.BlockSpec((d, tn), lambda i, j: (0, i)),
            pl.BlockSpec((te, d), lambda i, j: (jnp.minimum(j, n_e - 1), 0)),
            pl.BlockSpec((d, te), lambda i, j: (0, jnp.maximum(j - 1, 0))),
            step_rows, step_rows, col, col,
            row, pl.BlockSpec((None, 1, d), lambda i, j: (i // (s // tn), 0, 0)), const, const,
        ],
        out_specs=row,
        scratch_shapes=[pltpu.VMEM((d, tn), F32), pltpu.VMEM((te, tn), F8), pltpu.VMEM((te, tn), F8)],
        compiler_params=_cparams(("parallel", "arbitrary")),
        name="peer_dense",
    )(scales, x8, u_tab, vt_tab, e1, cut, e2, r2, x_res, g2, ln_g.reshape(1, d), ln_b.reshape(1, d))


def rope_tables(seq):
    pos = jnp.arange(seq, dtype=F32)
    inv = ROPE_THETA ** (-jnp.arange(0, HEAD_DIM, 2, dtype=F32) / HEAD_DIM)
    ang = pos[:, None] * inv[None, :]
    sign = jnp.where((jnp.arange(128) % 64) < 32, -1.0, 1.0).astype(F32)
    return jnp.tile(jnp.cos(ang), (1, 4)), jnp.tile(jnp.sin(ang), (1, 4)) * sign[None, :]


def _swa_head_perm():
    heads = [h for p in range(4) for h in (p, 4 + p)]
    return jnp.concatenate([jnp.arange(h * HEAD_DIM, (h + 1) * HEAD_DIM) for h in heads])


def relayout_w_in(w):
    qa, ka, va = w[:, 0:768], w[:, 768:1536], w[:, 1536:2304]
    hy = w[:, 2304:3840]
    qc, kvc = w[:, 3840:4352], w[:, 4352:4608]
    gates = w[:, 4608:7680]
    dil = [jnp.concatenate([t[:, 256 * g:256 * (g + 1)] for t in (qa, ka, va)], axis=1) for g in range(3)]
    main = jnp.concatenate([gates, hy, dil[0], qc[:, _swa_head_perm()], kvc], axis=1).astype(BF16)
    return main, jnp.concatenate(dil[1:], axis=1).astype(BF16)


def residue_major(tab, dil):
    s = tab.shape[0]
    return tab.reshape(s // dil, dil, 128).transpose(1, 0, 2).reshape(s, 128)


def token_mixer(x, ada, w_main, w_dil, conv_w, conv_b, hy_bias, attn_sink, hr, hi, dft_tabs, rope_tabs,
                wa, wb, wc, wo, ln_g, ln_b):
    b, s, _ = x.shape
    sh1, sc1, g1, sh2, sc2, _ = [a[:, None, :] for a in jnp.split(ada, 6, axis=-1)]
    proj = in_proj(x, sc1, sh1, w_main)
    qkv1, qkv2 = in_proj_dil(x, sc1, sh1, w_dil)
    att = (dilated_attention(proj.reshape(b, 1, s, MAIN_WIDTH), COL_DIL0 // QKV_W, *rope_tabs[0])
           + dilated_attention(qkv1, 0, *rope_tabs[1]) + dilated_attention(qkv2, 0, *rope_tabs[2]))
    yb = hyena_mixer(proj, conv_w, conv_b, hy_bias, hr, hi, dft_tabs)
    yc = swa_attention(proj, *rope_tabs[0], attn_sink)
    return merge_mixers(att, yb, yc, proj, x, g1, sc2, sh2, ln_g, ln_b, wa, wb, wc, wo)


def peer_sublayer(u2, wq, keys, u_tab, v_tab, x_res, g2, ln_g, ln_b):
    d = u2.shape[0]
    nh, nk = PEER_HEADS, PEER_KEYS
    wq_t = wq.reshape(d, nh, 2, nk).transpose(2, 1, 3, 0).reshape(2 * nh * nk, d).astype(BF16)
    keys_r = keys.transpose(1, 0, 2, 3).reshape(2 * nh, nk, keys.shape[-1]).astype(BF16)
    tabs = peer_route(u2, wq_t, keys_r)
    su, sv, sx = _pow2_scale(u_tab), _pow2_scale(v_tab), _pow2_scale(u2)
    scales = jnp.stack([1.0 / (su * sx), 1.0 / (sv * WG_SCALE)]).astype(F32)
    x8 = (u2.astype(F32) * sx).astype(F8)
    return peer_dense(scales, x8, (u_tab * su).astype(F8), (v_tab.T * sv).astype(F8), *tabs, x_res, g2, ln_g, ln_b)


def _pow2_scale(a):
    m = jnp.maximum(jnp.max(jnp.abs(a)).astype(F32), 2.0 ** -100)
    return jnp.exp2(jnp.floor(jnp.log2(F8_MAX / m)))


def kernel(x, c, w_ada, b_ada, w_in, conv_w, conv_b, hy_w1, hy_b1, hy_w2, hy_b2, hy_w3, hy_freq, hy_log_decay,
           hy_bias, attn_sink, w_branch_a, w_branch_b, w_branch_c, w_out, ln_g, ln_b, peer_wq, peer_keys, peer_u,
           peer_v):
    b, s, d = x.shape
    depth = w_in.shape[0]
    hw = HYENA_WIDTH
    cos_t, sin_t = rope_tables(s)
    fwd = dft_table(s)
    dft_tabs = dft_tables_r2(s)
    ada = ada_ln(c, w_ada, b_ada)
    w3p = hy_w3.reshape(depth, -1, 2, 2, hw).transpose(0, 1, 3, 2, 4).reshape(depth, -1, 4 * hw)
    ldp = hy_log_decay.reshape(depth, 2, 2, hw).transpose(0, 2, 1, 3).reshape(depth, 4 * hw)
    hr, hi = hyena_filters(hy_w1, hy_b1, hy_w2, hy_b2, w3p, hy_freq, ldp, fwd, s)
    perm = _swa_head_perm()
    rope_tabs = [(residue_major(cos_t, dil), residue_major(sin_t, dil)) for _, dil in DIL_GROUPS]
    for l in range(depth):
        x1, u2 = token_mixer(x, ada[l], *relayout_w_in(w_in[l]), conv_w[l], conv_b[l], hy_bias[l], attn_sink[l],
                             hr[l], hi[l], dft_tabs, rope_tabs, w_branch_a[l].astype(BF16),
                             w_branch_b[l].astype(BF16), w_branch_c[l][perm].astype(BF16), w_out[l].astype(BF16),
                             ln_g[l, 0], ln_b[l, 0])
        x = peer_sublayer(u2, peer_wq[l], peer_keys[l], peer_u[l], peer_v[l], x1, ada[l][:, None, 5 * d:6 * d],
                          ln_g[l, 1], ln_b[l, 1])
    return x
```

```python
import functools
import math

import jax
import jax.numpy as jnp
from jax import lax
from jax.experimental import pallas as pl
from jax.experimental.pallas import tpu as pltpu

F32 = jnp.float32
BF16 = jnp.bfloat16

D_MODEL = 1024
HEAD_DIM = 64
DIL_GROUPS = ((128, 1), (512, 4), (2048, 16))
DIL_RADIUS = 64
HYENA_WIDTH = 512
HYENA_BANDS = 16
SWA_RADIUS = 128
SWA_Q_HEADS = 8
SWA_KV_HEADS = 2
PEER_HEADS = 8
PEER_KEYS = 128
PEER_TOPK = 16
DEPTH = 2
ALPHA = (2 * DEPTH) ** 0.25
LN_EPS = 1e-5
NEG_INF = -1e30
ROPE_THETA = 10000.0

COL_GATES = 0
COL_HY = 3072
COL_DIL0 = 4608
COL_SWA = 5376
MAIN_WIDTH = 6144
QKV_W = 768

VMEM_LIMIT = 56 * 1024 * 1024


def _cparams(sem, vmem=VMEM_LIMIT):
    return pltpu.CompilerParams(dimension_semantics=sem, vmem_limit_bytes=vmem)


def _ada_kernel(c_ref, w_ref, b_ref, o_ref):
    c = c_ref[...]
    cond = (c * jax.nn.sigmoid(c)).astype(BF16)
    o_ref[...] = jnp.dot(cond, w_ref[...].astype(BF16), preferred_element_type=F32) + b_ref[...]


def ada_ln(c, w_ada, b_ada):
    depth, d, n = w_ada.shape
    b = c.shape[0]
    tn = 1536
    return pl.pallas_call(
        _ada_kernel,
        out_shape=jax.ShapeDtypeStruct((depth, b, n), F32),
        grid=(depth, n // tn),
        in_specs=[
            pl.BlockSpec((b, d), lambda l, j: (0, 0)),
            pl.BlockSpec((None, d, tn), lambda l, j: (l, 0, j)),
            pl.BlockSpec((None, 1, tn), lambda l, j: (l, 0, j)),
        ],
        out_specs=pl.BlockSpec((None, b, tn), lambda l, j: (l, 0, j)),
        compiler_params=_cparams(("arbitrary", "arbitrary")),
        name="ada_ln",
    )(c, w_ada, b_ada.reshape(depth, 1, n))


def _inproj_kernel(x_ref, sc_ref, sh_ref, w_ref, o_ref, u_ref):
    @pl.when(pl.program_id(2) == 0)
    def _():
        u_ref[...] = (x_ref[...] * (1.0 + sc_ref[...]) + sh_ref[...]).astype(BF16)

    o_ref[...] = jnp.dot(u_ref[...], w_ref[...], preferred_element_type=F32).astype(o_ref.dtype)


def in_proj(x, sc, sh, w):
    b, s, d = x.shape
    n = w.shape[1]
    tm, tn = 1024, 1536
    return pl.pallas_call(
        _inproj_kernel,
        out_shape=jax.ShapeDtypeStruct((b, s, n), BF16),
        grid=(b, s // tm, n // tn),
        in_specs=[
            pl.BlockSpec((None, tm, d), lambda bi, i, j: (bi, i, 0)),
            pl.BlockSpec((None, 1, d), lambda bi, i, j: (bi, 0, 0)),
            pl.BlockSpec((None, 1, d), lambda bi, i, j: (bi, 0, 0)),
            pl.BlockSpec((d, tn), lambda bi, i, j: (0, j)),
        ],
        out_specs=pl.BlockSpec((None, tm, tn), lambda bi, i, j: (bi, i, j)),
        scratch_shapes=[pltpu.VMEM((tm, d), BF16)],
        compiler_params=_cparams(("parallel", "parallel", "arbitrary")),
        name="in_proj",
    )(x, sc, sh, w)


def _inproj_dil_kernel(x_ref, sc_ref, sh_ref, w_ref, o1_ref, o2_ref, res_s):
    tm = x_ref.shape[0]
    u = (x_ref[...] * (1.0 + sc_ref[...]) + sh_ref[...]).astype(BF16)
    res = jnp.dot(u, w_ref[...], preferred_element_type=F32)
    nslab = QKV_W // 128
    for c in range(2 * nslab):
        res_s[c] = res[:, c * 128:(c + 1) * 128]
    for g, o_ref in ((0, o1_ref), (1, o2_ref)):
        dil = o_ref.shape[0]
        for r in range(dil):
            for c in range(nslab):
                o_ref[r, :, c * 128:(c + 1) * 128] = res_s[g * nslab + c, pl.ds(r, tm // dil, stride=dil), :].astype(
                    o_ref.dtype)


def in_proj_dil(x, sc, sh, w):
    b, s, d = x.shape
    tm = 1024
    d1, d2 = DIL_GROUPS[1][1], DIL_GROUPS[2][1]
    return pl.pallas_call(
        _inproj_dil_kernel,
        out_shape=(jax.ShapeDtypeStruct((b, d1, s // d1, QKV_W), BF16),
                   jax.ShapeDtypeStruct((b, d2, s // d2, QKV_W), BF16)),
        grid=(b, s // tm),
        in_specs=[
            pl.BlockSpec((None, tm, d), lambda bi, i: (bi, i, 0)),
            pl.BlockSpec((None, 1, d), lambda bi, i: (bi, 0, 0)),
            pl.BlockSpec((None, 1, d), lambda bi, i: (bi, 0, 0)),
            pl.BlockSpec((d, 2 * QKV_W), lambda bi, i: (0, 0)),
        ],
        out_specs=(
            pl.BlockSpec((None, d1, tm // d1, QKV_W), lambda bi, i: (bi, 0, i, 0)),
            pl.BlockSpec((None, d2, tm // d2, QKV_W), lambda bi, i: (bi, 0, i, 0)),
        ),
        scratch_shapes=[pltpu.VMEM((2 * QKV_W // 128, tm, 128), F32)],
        compiler_params=_cparams(("parallel", "parallel")),
        name="in_proj_dil",
    )(x, sc, sh, w)


def _rope2(x, cos, sin_signed, lo):
    xr = jnp.where(lo, pltpu.roll(x, 96, 1), pltpu.roll(x, 32, 1))
    return x * cos + xr * sin_signed


def _lane_masks():
    lane = lax.broadcasted_iota(jnp.int32, (1, 128), 1)
    return (lane % 64) < 32, lane < 64


def _band_mask(q0, nq, nk, pad, radius, length):
    qpos = q0 + (lax.broadcasted_iota(jnp.int32, (nq, 1), 0) & 127)
    kpos = q0 - pad + lax.broadcasted_iota(jnp.int32, (1, nk), 1)
    kpos = jnp.where(kpos < 0, -4 * length, jnp.where(kpos >= length, -4 * length, kpos))
    return jnp.abs(qpos - kpos) <= radius


def _dil_attn_kernel(qkv_ref, cos_ref, sin_ref, olo_ref, ohi_ref, llo_ref, lhi_ref, q_s, k_s, v_s, *, dil, ls):
    lo, head0 = _lane_masks()
    heads = (head0, jnp.logical_not(head0))
    cos = cos_ref[...]
    sin = sin_ref[...]
    pad = DIL_RADIUS
    qb = 128
    nblk = ls // qb
    ngroup = 8
    zeros = jnp.zeros((dil, pad, 128), BF16)
    k_s[:, 0:pad, :] = zeros
    k_s[:, pad + ls:pad + ls + pad, :] = zeros
    v_s[:, 0:pad, :] = zeros
    v_s[:, pad + ls:pad + ls + pad, :] = zeros
    for hp, (o_ref, l_ref) in enumerate(((olo_ref, llo_ref), (ohi_ref, lhi_ref))):
        q = qkv_ref[:, :, hp * 128:(hp + 1) * 128].astype(F32).reshape(dil * ls, 128)
        k = qkv_ref[:, :, 256 + hp * 128:256 + (hp + 1) * 128].astype(F32).reshape(dil * ls, 128)
        q_s[...] = (_rope2(q, cos, sin, lo) * (HEAD_DIM ** -0.5)).astype(BF16)
        k_s[:, pad:pad + ls, :] = _rope2(k, cos, sin, lo).astype(BF16).reshape(dil, ls, 128)
        v_s[:, pad:pad + ls, :] = qkv_ref[:, :, 512 + hp * 128:512 + (hp + 1) * 128]

        def grp(gi, carry):
            tiles = []
            for t in range(ngroup):
                c = gi * ngroup + t
                r = c // nblk
                q0 = pl.multiple_of((c % nblk) * qb, qb)
                qblk = q_s[pl.ds(pl.multiple_of(c * qb, qb), qb), :]
                kw = k_s[r, pl.ds(q0, qb + 2 * pad), :]
                vw = v_s[r, pl.ds(q0, qb + 2 * pad), :]
                mask = _band_mask(q0, qb, qb + 2 * pad, pad, DIL_RADIUS, ls)
                ss = [lax.dot_general(jnp.where(hm, qblk, jnp.zeros_like(qblk)), kw, (((1,), (1,)), ((), ())),
                                      preferred_element_type=F32) for hm in heads]
                tiles.append((r, q0, vw, mask, ss))
            soft = []
            for r, q0, vw, mask, ss in tiles:
                for s in ss:
                    s = jnp.where(mask, s, NEG_INF)
                    m = jnp.max(s, axis=-1, keepdims=True)
                    p = jnp.exp(s - m)
                    soft.append((p.astype(BF16), jnp.sum(p, axis=-1, keepdims=True), m))
            for ti, (r, q0, vw, mask, ss) in enumerate(tiles):
                res = []
                for h in range(2):
                    p, den, m = soft[2 * ti + h]
                    res.append((jnp.dot(p, vw, preferred_element_type=F32) / den, m + jnp.log(den)))
                o = jnp.where(head0, res[0][0], res[1][0])
                lse = jnp.where(head0, res[0][1], res[1][1])
                rows = pl.ds(q0, qb) if dil == 1 else pl.ds(q0 * dil + r, qb, stride=dil)
                o_ref[rows, :] = o
                l_ref[rows, :] = lse
            return carry

        lax.fori_loop(0, dil * nblk // ngroup, grp, 0)


def dilated_attention(qkv, col_block, cos_d, sin_d):
    b, dil, ls, _ = qkv.shape
    s = dil * ls
    out_sds = jax.ShapeDtypeStruct((b, s, 128), F32)
    out_spec = pl.BlockSpec((None, s, 128), lambda bi: (bi, 0, 0))
    return pl.pallas_call(
        functools.partial(_dil_attn_kernel, dil=dil, ls=ls),
        out_shape=(out_sds,) * 4,
        grid=(b,),
        in_specs=[
            pl.BlockSpec((None, dil, ls, QKV_W), lambda bi: (bi, 0, 0, col_block)),
            pl.BlockSpec((s, 128), lambda bi: (0, 0)),
            pl.BlockSpec((s, 128), lambda bi: (0, 0)),
        ],
        out_specs=(out_spec,) * 4,
        scratch_shapes=[
            pltpu.VMEM((s, 128), BF16),
            pltpu.VMEM((dil, ls + 2 * DIL_RADIUS, 128), BF16),
            pltpu.VMEM((dil, ls + 2 * DIL_RADIUS, 128), BF16),
        ],
        compiler_params=_cparams(("parallel",)),
        name=f"dil_attn_d{dil}",
    )(qkv, cos_d, sin_d)


def _swa_kernel(qkv_ref, cos_ref, sin_ref, sink_ref, o_ref, q_s, k_s, v_s, *, s_len):
    lo, head0 = _lane_masks()
    heads = (head0, jnp.logical_not(head0))
    cos = cos_ref[...]
    sin = sin_ref[...]
    pad = SWA_RADIUS
    qb = 128
    grp = SWA_Q_HEADS // SWA_KV_HEADS
    zeros = jnp.zeros((pad, 128), BF16)
    k_s[0:pad, :] = zeros
    k_s[pad + s_len:pad + s_len + pad, :] = zeros
    v_s[0:pad, :] = zeros
    v_s[pad + s_len:pad + s_len + pad, :] = zeros
    k = qkv_ref[:, 512:640].astype(F32)
    k_s[pad:pad + s_len, :] = _rope2(k, cos, sin, lo).astype(BF16)
    v_s[pad:pad + s_len, :] = qkv_ref[:, 640:768]
    for qp in range(grp):
        q = qkv_ref[:, qp * 128:(qp + 1) * 128].astype(F32)
        q_s[qp] = (_rope2(q, cos, sin, lo) * (HEAD_DIM ** -0.5)).astype(BF16)
    sinks = [jnp.concatenate([jnp.broadcast_to(sink_ref[kv * grp + p:kv * grp + p + 1, 0:1], (qb, 1))
                              for p in range(grp)], axis=0) for kv in range(SWA_KV_HEADS)]

    def blk(i, carry):
        q0 = pl.multiple_of(i * qb, qb)
        kw = k_s[pl.ds(q0, qb + 2 * pad), :]
        vw = v_s[pl.ds(q0, qb + 2 * pad), :]
        qall = jnp.concatenate([q_s[p, pl.ds(q0, qb), :] for p in range(grp)], axis=0)
        mask = _band_mask(q0, grp * qb, qb + 2 * pad, pad, SWA_RADIUS, s_len)
        ss = [lax.dot_general(jnp.where(hm, qall, jnp.zeros_like(qall)), kw, (((1,), (1,)), ((), ())),
                              preferred_element_type=F32) for hm in heads]
        soft = []
        for s, sk in zip(ss, sinks):
            s = jnp.where(mask, s, NEG_INF)
            m = jnp.maximum(jnp.max(s, axis=-1, keepdims=True), sk)
            p = jnp.exp(s - m)
            soft.append((p.astype(BF16), jnp.sum(p, axis=-1, keepdims=True) + jnp.exp(sk - m)))
        res = [jnp.dot(p, vw, preferred_element_type=F32) / den for p, den in soft]
        o = jnp.where(head0, res[0], res[1]).astype(o_ref.dtype)
        for p in range(grp):
            o_ref[pl.ds(q0, qb), p * 128:(p + 1) * 128] = o[p * qb:(p + 1) * qb, :]
        return carry

    lax.fori_loop(0, s_len // qb, blk, 0, unroll=8)


def swa_attention(proj, cos_t, sin_t, sink):
    b, s, n = proj.shape
    return pl.pallas_call(
        functools.partial(_swa_kernel, s_len=s),
        out_shape=jax.ShapeDtypeStruct((b, s, 512), BF16),
        grid=(b,),
        in_specs=[
            pl.BlockSpec((None, s, QKV_W), lambda bi: (bi, 0, COL_SWA // QKV_W)),
            pl.BlockSpec((s, 128), lambda bi: (0, 0)),
            pl.BlockSpec((s, 128), lambda bi: (0, 0)),
            pl.BlockSpec((8, 128), lambda bi: (0, 0)),
        ],
        out_specs=pl.BlockSpec((None, s, 512), lambda bi: (bi, 0, 0)),
        scratch_shapes=[
            pltpu.VMEM((SWA_Q_HEADS // SWA_KV_HEADS, s, 128), BF16),
            pltpu.VMEM((s + 2 * SWA_RADIUS, 128), BF16),
            pltpu.VMEM((s + 2 * SWA_RADIUS, 128), BF16),
        ],
        compiler_params=_cparams(("parallel",)),
        name="swa_attn",
    )(proj, cos_t, sin_t, jnp.broadcast_to(sink.astype(F32)[:, None], (8, 128)))


HY_CHUNKS = 4


def dft_table(seq):
    kc = seq // HY_CHUNKS
    s_idx = jnp.arange(seq, dtype=jnp.int32)
    k = jnp.where(s_idx < seq // 2, s_idx, seq + seq // 2 - 1 - s_idx)
    phase = ((2 * k[:, None] + 1) * s_idx[None, :]) % (4 * seq)
    ang = phase.astype(F32) * (2.0 * math.pi / (4 * seq))
    c = jnp.cos(ang).reshape(HY_CHUNKS, kc, seq)
    s = jnp.sin(ang).reshape(HY_CHUNKS, kc, seq)
    return jnp.concatenate([c, s], axis=1).astype(BF16)


def _hy_filter_kernel(w1_ref, b1_ref, w2_ref, b2_ref, w3_ref, fr_ref, ld_ref, bands_ref, f_ref,
                      hr_ref, hi_ref, hs_s, hd_s, *, seq):
    hw = HYENA_WIDTH
    hi_p = lax.Precision.HIGHEST

    @pl.when(pl.program_id(2) == 0)
    def _():
        idx = lax.broadcasted_iota(jnp.int32, (seq, 1), 0).astype(F32)
        t = idx / max(seq - 1, 1)
        w = 2.0 * math.pi * idx / seq
        ang = w * bands_ref[...]
        w1 = w1_ref[...]
        pre = (t * w1[0:1, :]
               + jnp.dot(jnp.cos(ang), w1[1:1 + HYENA_BANDS, :], precision=hi_p, preferred_element_type=F32)
               - jnp.dot(jnp.sin(ang), w1[1 + HYENA_BANDS:1 + 2 * HYENA_BANDS, :], precision=hi_p,
                         preferred_element_type=F32)
               + b1_ref[...])
        h = jnp.sin(fr_ref[0:1, :] * pre)
        h = jnp.sin(fr_ref[1:2, :] * (jnp.dot(h, w2_ref[...], precision=hi_p, preferred_element_type=F32)
                                      + b2_ref[...]))
        h = jnp.dot(h, w3_ref[...], precision=hi_p, preferred_element_type=F32)
        h = h * jnp.exp(-t * jnp.exp(ld_ref[...]))
        hf = h[:, :hw]
        hb = jnp.where(idx > 0.0, h[:, hw:], 0.0)
        inv = lax.rsqrt(jnp.sum(hf * hf + hb * hb, axis=0, keepdims=True) + 1e-12)
        hs_s[...] = ((hf + hb) * inv).astype(BF16)
        hd_s[...] = ((hf - hb) * inv).astype(BF16)

    kc = seq // HY_CHUNKS
    hr_ref[...] = jnp.dot(f_ref[0:kc, :], hs_s[...], preferred_element_type=F32)
    hi_ref[...] = -jnp.dot(f_ref[kc:2 * kc, :], hd_s[...], preferred_element_type=F32)


def hyena_filters(w1, b1, w2, b2, w3p, freq, ldp, fwd, seq):
    depth = w1.shape[0]
    kc = seq // HY_CHUNKS
    hw = HYENA_WIDTH
    bands = jnp.linspace(1e-4, HYENA_BANDS - 1, HYENA_BANDS, dtype=F32).reshape(1, HYENA_BANDS)
    full = lambda *shape: pl.BlockSpec((None,) + shape, lambda l, o, c: (l,) + (0,) * len(shape))
    out_sds = jax.ShapeDtypeStruct((depth, 2, seq, hw), F32)
    return pl.pallas_call(
        functools.partial(_hy_filter_kernel, seq=seq),
        out_shape=(out_sds, out_sds),
        grid=(depth, 2, HY_CHUNKS),
        in_specs=[
            full(*w1.shape[1:]), full(1, b1.shape[-1]), full(*w2.shape[1:]), full(1, b2.shape[-1]),
            pl.BlockSpec((None, w3p.shape[1], 2 * hw), lambda l, o, c: (l, 0, o)),
            full(*freq.shape[1:]),
            pl.BlockSpec((None, 1, 2 * hw), lambda l, o, c: (l, 0, o)),
            pl.BlockSpec((1, HYENA_BANDS), lambda l, o, c: (0, 0)),
            pl.BlockSpec((None, 2 * kc, seq), lambda l, o, c: (c, 0, 0)),
        ],
        out_specs=(
            pl.BlockSpec((None, None, kc, hw), lambda l, o, c: (l, o, c, 0)),
            pl.BlockSpec((None, None, kc, hw), lambda l, o, c: (l, o, c, 0)),
        ),
        scratch_shapes=[pltpu.VMEM((seq, hw), BF16), pltpu.VMEM((seq, hw), BF16)],
        compiler_params=_cparams(("arbitrary", "arbitrary", "arbitrary")),
        name="hyena_filters",
    )(w1, b1[:, None, :], w2, b2[:, None, :], w3p, freq, ldp[:, None, :], bands, fwd)


def dft_tables_r2(seq):
    half = seq // 2
    kc = half // HY_CHUNKS
    k = jnp.arange(half, dtype=jnp.int32)[:, None]
    n = jnp.arange(half, dtype=jnp.int32)[None, :]
    out = []
    for off in (0, 1):
        ang = (((2 * k + 1) * (2 * n + off)) % (4 * seq)).astype(F32) * (2.0 * math.pi / (4 * seq))
        c = jnp.cos(ang).reshape(HY_CHUNKS, kc, half)
        s = jnp.sin(ang).reshape(HY_CHUNKS, kc, half)
        out.append(jnp.concatenate([c, s], axis=1).astype(BF16))
        out.append((jnp.concatenate([c, -s], axis=1) * (1.0 / seq)).transpose(0, 2, 1).astype(BF16))
    return out[0], out[2], out[1], out[3]


def _hyena_kernel(hy_ref, cw_ref, cb_ref, bias_ref, hr_ref, hi_ref, hrp_ref, hip_ref, fe_ref, fo_ref, ge_ref, go_ref,
                  o_ref, zb_s, zf_s, acc_s, slab_s, *, seq):
    hw = HYENA_WIDTH
    half = seq // 2
    o = pl.program_id(1)
    c = pl.program_id(2)
    kc = half // HY_CHUNKS
    nslab = hw // 128

    def short_conv(part):
        x = hy_ref[:, part * hw:(part + 1) * hw].astype(F32)
        row = lax.broadcasted_iota(jnp.int32, (seq, 1), 0)
        xm = jnp.where(row == 0, 0.0, pltpu.roll(x, 1, 0))
        xp = jnp.where(row == seq - 1, 0.0, pltpu.roll(x, seq - 1, 0))
        w = cw_ref[:, part * hw:(part + 1) * hw]
        return cb_ref[:, part * hw:(part + 1) * hw] + xm * w[0:1, :] + x * w[1:2, :] + xp * w[2:3, :]

    def split(x):
        for sl in range(nslab):
            slab_s[sl] = x[:, sl * 128:(sl + 1) * 128]
        return tuple(jnp.concatenate([slab_s[sl, pl.ds(par, half, stride=2), :] for sl in range(nslab)], axis=-1)
                     for par in (0, 1))

    def interleave(xe, xo):
        for sl in range(nslab):
            slab_s[sl, pl.ds(0, half, stride=2), :] = xe[:, sl * 128:(sl + 1) * 128]
            slab_s[sl, pl.ds(1, half, stride=2), :] = xo[:, sl * 128:(sl + 1) * 128]
        return jnp.concatenate([slab_s[sl] for sl in range(nslab)], axis=-1)

    @pl.when((o == 0) & (c == 0))
    def _():
        for par, z in enumerate(split(short_conv(0))):
            zf_s[par] = z
            zb_s[par] = z.astype(BF16)

    ze = jnp.dot(fe_ref[...], zb_s[0], preferred_element_type=F32)
    zo = jnp.dot(fo_ref[...], zb_s[1], preferred_element_type=F32)
    zc, zs = ze[:kc] + zo[:kc], ze[kc:] + zo[kc:]
    zcp, zsp = ze[:kc] - zo[:kc], zo[kc:] - ze[kc:]
    hr, hi, hrp, hip = hr_ref[...], hi_ref[...], hrp_ref[...], hip_ref[...]
    yr, yi = zc * hr + zs * hi, zc * hi - zs * hr
    yrp, yip = zcp * hrp + zsp * hip, zcp * hip - zsp * hrp
    pe = jnp.dot(ge_ref[...], jnp.concatenate([yr + yrp, yi - yip], axis=0).astype(BF16),
                 preferred_element_type=F32)
    po = jnp.dot(go_ref[...], jnp.concatenate([yr - yrp, yi + yip], axis=0).astype(BF16),
                 preferred_element_type=F32)

    @pl.when(c == 0)
    def _():
        acc_s[0] = pe
        acc_s[1] = po

    @pl.when(c > 0)
    def _():
        acc_s[0] += pe
        acc_s[1] += po

    @pl.when((c == HY_CHUNKS - 1) & (o == 0))
    def _():
        for par, p1 in enumerate(split(short_conv(1))):
            z1 = p1 * (acc_s[par] + bias_ref[0:1, :] * zf_s[par])
            zf_s[par] = z1
            zb_s[par] = z1.astype(BF16)

    @pl.when((c == HY_CHUNKS - 1) & (o == 1))
    def _():
        halves = [p2 * (acc_s[par] + bias_ref[1:2, :] * zf_s[par]) for par, p2 in enumerate(split(short_conv(2)))]
        o_ref[...] = interleave(*halves).astype(o_ref.dtype)


def hyena_mixer(proj, conv_w, conv_b, hy_bias, hr, hi, tabs):
    b, s, n = proj.shape
    hw = HYENA_WIDTH
    half = s // 2
    kc = half // HY_CHUNKS
    nck = HY_CHUNKS
    spec = pl.BlockSpec((None, kc, hw), lambda bi, o, c: (o, c, 0))
    spec_p = pl.BlockSpec((None, kc, hw), lambda bi, o, c: (o, nck + c, 0))
    fwd_spec = pl.BlockSpec((None, 2 * kc, half), lambda bi, o, c: (c, 0, 0))
    inv_spec = pl.BlockSpec((None, half, 2 * kc), lambda bi, o, c: (c, 0, 0))
    fe, fo, ge, go = tabs
    return pl.pallas_call(
        functools.partial(_hyena_kernel, seq=s),
        out_shape=jax.ShapeDtypeStruct((b, s, hw), BF16),
        grid=(b, 2, nck),
        in_specs=[
            pl.BlockSpec((None, s, 3 * hw), lambda bi, o, c: (bi, 0, COL_HY // (3 * hw))),
            pl.BlockSpec((3, 3 * hw), lambda bi, o, c: (0, 0)),
            pl.BlockSpec((1, 3 * hw), lambda bi, o, c: (0, 0)),
            pl.BlockSpec((2, hw), lambda bi, o, c: (0, 0)),
            spec, spec, spec_p, spec_p, fwd_spec, fwd_spec, inv_spec, inv_spec,
        ],
        out_specs=pl.BlockSpec((None, s, hw), lambda bi, o, c: (bi, 0, 0)),
        scratch_shapes=[pltpu.VMEM((2, half, hw), BF16), pltpu.VMEM((2, half, hw), F32),
                        pltpu.VMEM((2, half, hw), F32), pltpu.VMEM((hw // 128, s, 128), F32)],
        compiler_params=_cparams(("parallel", "arbitrary", "arbitrary")),
        name="hyena_conv",
    )(proj, conv_w, conv_b.reshape(1, -1), hy_bias, hr, hi, hr, hi, fe, fo, ge, go)


def _layer_norm(y, g, b):
    mu = jnp.mean(y, axis=-1, keepdims=True)
    yc = y - mu
    var = jnp.mean(yc * yc, axis=-1, keepdims=True)
    return yc * lax.rsqrt(var + LN_EPS) * g + b


def _merge_kernel(*refs):
    att = refs[:12]
    (yb_ref, yc_ref, gl_ref, x_ref, g1_ref, sc2_ref, sh2_ref, lng_ref, lnb_ref, wa_ref, wb_ref, wc_ref, wo_ref,
     xo_ref, u2_ref) = refs[12:]
    d = D_MODEL
    halves = []
    for half in range(2):
        la, lb, lc = (att[4 * g + 2 + half][...] for g in range(3))
        m = jnp.maximum(jnp.maximum(la, lb), lc)
        ea, eb, ec = jnp.exp(la - m), jnp.exp(lb - m), jnp.exp(lc - m)
        inv = 1.0 / (ea + eb + ec)
        halves.append((ea * inv) * att[half][...] + (eb * inv) * att[4 + half][...] + (ec * inv) * att[8 + half][...])
    ya = jnp.concatenate(halves, axis=-1)
    za = jnp.dot(ya.astype(BF16), wa_ref[...], preferred_element_type=F32)
    zb = jnp.dot(yb_ref[...], wb_ref[...], preferred_element_type=F32)
    zc = jnp.dot(yc_ref[...], wc_ref[...], preferred_element_type=F32)
    merged = (jax.nn.sigmoid(gl_ref[:, 0:d].astype(F32)) * za
              + jax.nn.sigmoid(gl_ref[:, d:2 * d].astype(F32)) * zb
              + jax.nn.sigmoid(gl_ref[:, 2 * d:3 * d].astype(F32)) * zc)
    mix = jnp.dot(merged.astype(BF16), wo_ref[...], preferred_element_type=F32)
    xn = _layer_norm(ALPHA * x_ref[...] + g1_ref[...] * mix, lng_ref[...], lnb_ref[...])
    xo_ref[...] = xn
    u2_ref[...] = (xn * (1.0 + sc2_ref[...]) + sh2_ref[...]).T.astype(BF16)


def merge_mixers(att, yb, yc, proj, x, g1, sc2, sh2, ln_g, ln_b, wa, wb, wc, wo):
    b, s, d = x.shape
    tm = 512
    row = lambda w: pl.BlockSpec((None, tm, w), lambda bi, i: (bi, i, 0))
    per_b = pl.BlockSpec((None, 1, d), lambda bi, i: (bi, 0, 0))
    const = lambda r, c: pl.BlockSpec((r, c), lambda bi, i: (0, 0))
    return pl.pallas_call(
        _merge_kernel,
        out_shape=(jax.ShapeDtypeStruct((b, s, d), F32), jax.ShapeDtypeStruct((d, b * s), BF16)),
        grid=(b, s // tm),
        in_specs=[row(128)] * 12 + [row(512), row(512), row(3 * d), row(d), per_b, per_b, per_b,
                                    const(1, d), const(1, d), const(256, d), const(512, d), const(512, d),
                                    const(d, d)],
        out_specs=(row(d), pl.BlockSpec((d, tm), lambda bi, i: (0, bi * (s // tm) + i))),
        compiler_params=_cparams(("parallel", "parallel")),
        name="merge_mixers",
    )(*att, yb, yc, proj, x, g1, sc2, sh2, ln_g.reshape(1, d), ln_b.reshape(1, d), wa, wb, wc, wo)


_PEER_PAIRS = [(i, j) for i in range(1, PEER_TOPK + 1) for j in range(1, PEER_TOPK + 1) if i * j <= PEER_TOPK]
PEER_NOT_TOP = 100.0


def _sorted_prefix_count(vals, pred):
    w = jnp.where
    t8 = pred(vals[7])
    t4 = pred(w(t8, vals[11], vals[3]))
    t2 = pred(w(t8, w(t4, vals[13], vals[9]), w(t4, vals[5], vals[1])))
    lo = w(t4, w(t2, vals[6], vals[4]), w(t2, vals[2], vals[0]))
    hi = w(t4, w(t2, vals[14], vals[12]), w(t2, vals[10], vals[8]))
    t1 = pred(w(t8, hi, lo))
    t16 = pred(vals[15])
    count = w(t8, 8.0, 0.0) + w(t4, 4.0, 0.0) + w(t2, 2.0, 0.0) + w(t1, 1.0, 0.0) + w(t16, 1.0, 0.0)
    return count, t16


def _peer_route_kernel(u_ref, wq_ref, keys_ref, e1_ref, cut_ref, e2_ref, r2_ref, q_s, s_s, ab_s, st_s, *, tr):
    nh, nk, topk = PEER_HEADS, PEER_KEYS, PEER_TOPK
    qt = jnp.dot(wq_ref[...], u_ref[...], preferred_element_type=F32)
    q_s[...] = qt.astype(BF16)
    for ph in range(2 * nh):
        s_s[ph] = jnp.dot(keys_ref[ph], q_s[ph * 128:(ph + 1) * 128, :], preferred_element_type=F32)

    def extract(h, carry):
        def rnd(r, prev):
            cur = []
            for p in range(2):
                sv = s_s[p * nh + h]
                cur.append(jnp.max(jnp.where(sv < prev[p], sv, -jnp.inf), axis=0, keepdims=True))
                ab_s[p, r, pl.ds(h, 1), :] = cur[p]
            return tuple(cur)

        top = jnp.full((1, tr), jnp.inf, F32)
        lax.fori_loop(0, topk, rnd, (top, top))
        return carry

    lax.fori_loop(0, nh, extract, 0)

    for ch in range(tr // 128):
        ln = slice(ch * 128, (ch + 1) * 128)
        a = [ab_s[0, r, :, ln] for r in range(topk)]
        b = [ab_s[1, r, :, ln] for r in range(topk)]
        cand = [a[i - 1] + b[j - 1] for (i, j) in _PEER_PAIRS]
        tau = jnp.full((nh, 128), -jnp.inf, F32)
        for x, (ix, jx) in enumerate(_PEER_PAIRS):
            cnt = jnp.zeros((nh, 128), F32)
            for y, (iy, jy) in enumerate(_PEER_PAIRS):
                if iy <= ix and jy <= jx:
                    cnt = cnt + 1.0
                elif iy >= ix and jy >= jx:
                    continue
                else:
                    cnt = cnt + jnp.where(cand[y] >= cand[x], 1.0, 0.0)
            tau = jnp.maximum(tau, jnp.where(cnt >= float(topk), cand[x], -jnp.inf))
        top = cand[0]
        z = jnp.zeros((nh, 128), F32)
        for cx in cand:
            z = z + jnp.where(cx >= tau, jnp.exp(cx - top), 0.0)
        st_s[0, :, ln] = tau
        st_s[1, :, ln] = 1.0 / z

    for h in range(nh):
        rows = slice(h * nk, (h + 1) * nk)
        s1 = s_s[h]
        s2 = s_s[nh + h]
        hrow = slice(h, h + 1)
        tau = st_s[0, hrow, :]
        in1 = s1 >= ab_s[0, topk - 1, hrow, :]
        bs = [ab_s[1, r, hrow, :] for r in range(topk)]
        cut, _ = _sorted_prefix_count(bs, lambda v: s1 + v >= tau)
        above, below_all = _sorted_prefix_count(bs, lambda v: v > s2)
        e1_ref[rows, :] = jnp.where(in1, jnp.exp(s1 - ab_s[0, 0, hrow, :]), 0.0)
        cut_ref[rows, :] = jnp.where(in1, cut, 0.0)
        e2_ref[rows, :] = jnp.where(below_all, 0.0, jnp.exp(s2 - bs[0]) * st_s[1, hrow, :]).astype(BF16)
        r2_ref[rows, :] = jnp.where(below_all, PEER_NOT_TOP, above + 1.0).astype(BF16)


def peer_route(u2, wq_t, keys):
    d, t = u2.shape
    tr = 512
    nrow = PEER_HEADS * PEER_KEYS
    tab = lambda dt: jax.ShapeDtypeStruct((nrow, t), dt)
    col = pl.BlockSpec((nrow, tr), lambda i: (0, i))
    return pl.pallas_call(
        functools.partial(_peer_route_kernel, tr=tr),
        out_shape=(tab(F32), tab(F32), tab(BF16), tab(BF16)),
        grid=(t // tr,),
        in_specs=[
            pl.BlockSpec((d, tr), lambda i: (0, i)),
            pl.BlockSpec(wq_t.shape, lambda i: (0, 0)),
            pl.BlockSpec(keys.shape, lambda i: (0, 0, 0)),
        ],
        out_specs=(col, col, col, col),
        scratch_shapes=[
            pltpu.VMEM((2 * nrow, tr), BF16),
            pltpu.VMEM((2 * PEER_HEADS, PEER_KEYS, tr), F32),
            pltpu.VMEM((2, PEER_TOPK, PEER_HEADS, tr), F32),
            pltpu.VMEM((2, PEER_HEADS, tr), F32),
        ],
        compiler_params=_cparams(("parallel",)),
        name="peer_route",
    )(u2, wq_t, keys)


F8 = jnp.float8_e4m3fn
F8_MAX = 448.0
WG_SCALE = 16.0


def _peer_dense_kernel(sc_ref, x_ref, u_ref, vt_ref, e1_ref, cut_ref, e2_ref, r2_ref, xres_ref, g2_ref, lng_ref,
                       lnb_ref, o_ref, acc_s, wga_s, wgb_s, *, te, n_e):
    j = pl.program_id(1)
    nk = PEER_KEYS
    tn = x_ref.shape[1]
    sub = 16

    @pl.when(j == 0)
    def _():
        acc_s[...] = jnp.zeros_like(acc_s)
        wgb_s[...] = jnp.zeros_like(wgb_s)

    def step(w_cur, w_prev):
        inv_a = sc_ref[0]
        ts = 512
        for sl in range(tn // ts):
            lanes = slice(sl * ts, (sl + 1) * ts)
            for ii in range(te // nk):
                rows = slice(ii * nk, (ii + 1) * nk)
                at = (jnp.dot(u_ref[rows, :], x_ref[:, lanes], preferred_element_type=F32) * inv_a).astype(BF16)
                if ii == 1:
                    acc_s[:, lanes] += jnp.dot(vt_ref[...], w_prev[:, lanes], preferred_element_type=F32)
                g = (0.5 * WG_SCALE * at) * (1.0 + lax.erf(at * (2.0 ** -0.5)))
                w = None
                for h in range(PEER_HEADS):
                    cb = jnp.broadcast_to(cut_ref[h, ii:ii + 1, lanes], (sub, ts)).astype(BF16)[None]
                    e1 = jnp.broadcast_to(e1_ref[h, ii:ii + 1, lanes], (sub, ts)).astype(BF16)[None]
                    r2 = r2_ref[h * nk:(h + 1) * nk, lanes].reshape(nk // sub, sub, ts)
                    e2 = e2_ref[h * nk:(h + 1) * nk, lanes].reshape(nk // sub, sub, ts)
                    sel = jnp.where(r2 <= cb, e2, jnp.zeros((), BF16)) * e1
                    w = sel if w is None else w + sel
                wg = jnp.clip(g * w.reshape(nk, ts), -F8_MAX, F8_MAX)
                w_cur[rows, lanes] = wg.astype(F8)

    @pl.when((j % 2 == 0) & (j < n_e))
    def _():
        step(wga_s, wgb_s)

    @pl.when(j % 2 == 1)
    def _():
        step(wgb_s, wga_s)

    @pl.when(j == n_e)
    def _():
        acc = acc_s[...] + jnp.dot(vt_ref[...], wgb_s[...], preferred_element_type=F32)
        ffn = (acc * sc_ref[1]).T
        o_ref[...] = _layer_norm(ALPHA * xres_ref[...] + g2_ref[...] * ffn, lng_ref[...], lnb_ref[...])


def peer_dense(scales, x8, u_tab, vt_tab, e1, cut, e2, r2, x_res, g2, ln_g, ln_b):
    d, t = x8.shape
    b, s, _ = x_res.shape
    ne = u_tab.shape[0]
    tn, te = 1024, 1024
    n_e = ne // te
    nrow = PEER_HEADS * PEER_KEYS
    col = pl.BlockSpec((nrow, tn), lambda i, j: (0, i))
    step_rows = pl.BlockSpec((PEER_HEADS, te // PEER_KEYS, tn), lambda i, j: (0, jnp.minimum(j, n_e - 1), i))
    e1 = e1.reshape(PEER_HEADS, PEER_KEYS, t)
    cut = cut.reshape(PEER_HEADS, PEER_KEYS, t)
    row = pl.BlockSpec((None, tn, d), lambda i, j: (i // (s // tn), i % (s // tn), 0))
    const = pl.BlockSpec((1, d), lambda i, j: (0, 0))
    return pl.pallas_call(
        functools.partial(_peer_dense_kernel, te=te, n_e=n_e),
        out_shape=jax.ShapeDtypeStruct((b, s, d), F32),
        grid=(t // tn, n_e + 1),
        in_specs=[
            pl.BlockSpec(memory_space=pltpu.SMEM),
            pl.BlockSpec((d, tn), lambda i, j: (0, i)),
            pl.BlockSpec((te, d), lambda i, j: (jnp.minimum(j, n_e - 1), 0)),
            pl.BlockSpec((d, te), lambda i, j: (0, jnp.maximum(j - 1, 0))),
            step_rows, step_rows, col, col,
            row, pl.BlockSpec((None, 1, d), lambda i, j: (i // (s // tn), 0, 0)), const, const,
        ],
        out_specs=row,
        scratch_shapes=[pltpu.VMEM((d, tn), F32), pltpu.VMEM((te, tn), F8), pltpu.VMEM((te, tn), F8)],
        compiler_params=_cparams(("parallel", "arbitrary")),
        name="peer_dense",
    )(scales, x8, u_tab, vt_tab, e1, cut, e2, r2, x_res, g2, ln_g.reshape(1, d), ln_b.reshape(1, d))


def rope_tables(seq):
    pos = jnp.arange(seq, dtype=F32)
    inv = ROPE_THETA ** (-jnp.arange(0, HEAD_DIM, 2, dtype=F32) / HEAD_DIM)
    ang = pos[:, None] * inv[None, :]
    sign = jnp.where((jnp.arange(128) % 64) < 32, -1.0, 1.0).astype(F32)
    return jnp.tile(jnp.cos(ang), (1, 4)), jnp.tile(jnp.sin(ang), (1, 4)) * sign[None, :]


def _swa_head_perm():
    heads = [h for p in range(4) for h in (p, 4 + p)]
    return jnp.concatenate([jnp.arange(h * HEAD_DIM, (h + 1) * HEAD_DIM) for h in heads])


def relayout_w_in(w):
    qa, ka, va = w[:, 0:768], w[:, 768:1536], w[:, 1536:2304]
    hy = w[:, 2304:3840]
    qc, kvc = w[:, 3840:4352], w[:, 4352:4608]
    gates = w[:, 4608:7680]
    dil = [jnp.concatenate([t[:, 256 * g:256 * (g + 1)] for t in (qa, ka, va)], axis=1) for g in range(3)]
    main = jnp.concatenate([gates, hy, dil[0], qc[:, _swa_head_perm()], kvc], axis=1).astype(BF16)
    return main, jnp.concatenate(dil[1:], axis=1).astype(BF16)


def residue_major(tab, dil):
    s = tab.shape[0]
    return tab.reshape(s // dil, dil, 128).transpose(1, 0, 2).reshape(s, 128)


def token_mixer(x, ada, w_main, w_dil, conv_w, conv_b, hy_bias, attn_sink, hr, hi, dft_tabs, rope_tabs,
                wa, wb, wc, wo, ln_g, ln_b):
    b, s, _ = x.shape
    sh1, sc1, g1, sh2, sc2, _ = [a[:, None, :] for a in jnp.split(ada, 6, axis=-1)]
    proj = in_proj(x, sc1, sh1, w_main)
    qkv1, qkv2 = in_proj_dil(x, sc1, sh1, w_dil)
    att = (dilated_attention(proj.reshape(b, 1, s, MAIN_WIDTH), COL_DIL0 // QKV_W, *rope_tabs[0])
           + dilated_attention(qkv1, 0, *rope_tabs[1]) + dilated_attention(qkv2, 0, *rope_tabs[2]))
    yb = hyena_mixer(proj, conv_w, conv_b, hy_bias, hr, hi, dft_tabs)
    yc = swa_attention(proj, *rope_tabs[0], attn_sink)
    return merge_mixers(att, yb, yc, proj, x, g1, sc2, sh2, ln_g, ln_b, wa, wb, wc, wo)


def peer_sublayer(u2, wq, keys, u_tab, v_tab, x_res, g2, ln_g, ln_b):
    d = u2.shape[0]
    nh, nk = PEER_HEADS, PEER_KEYS
    wq_t = wq.reshape(d, nh, 2, nk).transpose(2, 1, 3, 0).reshape(2 * nh * nk, d).astype(BF16)
    keys_r = keys.transpose(1, 0, 2, 3).reshape(2 * nh, nk, keys.shape[-1]).astype(BF16)
    tabs = peer_route(u2, wq_t, keys_r)
    su, sv, sx = _pow2_scale(u_tab), _pow2_scale(v_tab), _pow2_scale(u2)
    scales = jnp.stack([1.0 / (su * sx), 1.0 / (sv * WG_SCALE)]).astype(F32)
    x8 = (u2.astype(F32) * sx).astype(F8)
    return peer_dense(scales, x8, (u_tab * su).astype(F8), (v_tab.T * sv).astype(F8), *tabs, x_res, g2, ln_g, ln_b)


def _pow2_scale(a):
    m = jnp.maximum(jnp.max(jnp.abs(a)).astype(F32), 2.0 ** -100)
    return jnp.exp2(jnp.floor(jnp.log2(F8_MAX / m)))


def kernel(x, c, w_ada, b_ada, w_in, conv_w, conv_b, hy_w1, hy_b1, hy_w2, hy_b2, hy_w3, hy_freq, hy_log_decay,
           hy_bias, attn_sink, w_branch_a, w_branch_b, w_branch_c, w_out, ln_g, ln_b, peer_wq, peer_keys, peer_u,
           peer_v):
    b, s, d = x.shape
    depth = w_in.shape[0]
    hw = HYENA_WIDTH
    cos_t, sin_t = rope_tables(s)
    fwd = dft_table(s)
    dft_tabs = dft_tables_r2(s)
    ada = ada_ln(c, w_ada, b_ada)
    w3p = hy_w3.reshape(depth, -1, 2, 2, hw).transpose(0, 1, 3, 2, 4).reshape(depth, -1, 4 * hw)
    ldp = hy_log_decay.reshape(depth, 2, 2, hw).transpose(0, 2, 1, 3).reshape(depth, 4 * hw)
    hr, hi = hyena_filters(hy_w1, hy_b1, hy_w2, hy_b2, w3p, hy_freq, ldp, fwd, s)
    perm = _swa_head_perm()
    rope_tabs = [(residue_major(cos_t, dil), residue_major(sin_t, dil)) for _, dil in DIL_GROUPS]
    for l in range(depth):
        x1, u2 = token_mixer(x, ada[l], *relayout_w_in(w_in[l]), conv_w[l], conv_b[l], hy_bias[l], attn_sink[l],
                             hr[l], hi[l], dft_tabs, rope_tabs, w_branch_a[l].astype(BF16),
                             w_branch_b[l].astype(BF16), w_branch_c[l][perm].astype(BF16), w_out[l].astype(BF16),
                             ln_g[l, 0], ln_b[l, 0])
        x = peer_sublayer(u2, peer_wq[l], peer_keys[l], peer_u[l], peer_v[l], x1, ada[l][:, None, 5 * d:6 * d],
                          ln_g[l, 1], ln_b[l, 1])
    return x
```

```python
import functools
import math

import jax
import jax.numpy as jnp
from jax import lax
from jax.experimental import pallas as pl
from jax.experimental.pallas import tpu as pltpu

F32 = jnp.float32
BF16 = jnp.bfloat16

D_MODEL = 1024
HEAD_DIM = 64
DIL_GROUPS = ((128, 1), (512, 4), (2048, 16))
DIL_RADIUS = 64
HYENA_WIDTH = 512
HYENA_BANDS = 16
SWA_RADIUS = 128
SWA_Q_HEADS = 8
SWA_KV_HEADS = 2
PEER_HEADS = 8
PEER_KEYS = 128
PEER_TOPK = 16
DEPTH = 2
ALPHA = (2 * DEPTH) ** 0.25
LN_EPS = 1e-5
NEG_INF = -1e30
ROPE_THETA = 10000.0

COL_GATES = 0
COL_HY = 3072
COL_DIL0 = 4608
COL_SWA = 5376
MAIN_WIDTH = 6144
QKV_W = 768

VMEM_LIMIT = 56 * 1024 * 1024


def _cparams(sem, vmem=VMEM_LIMIT):
    return pltpu.CompilerParams(dimension_semantics=sem, vmem_limit_bytes=vmem)


def _ada_kernel(c_ref, w_ref, b_ref, o_ref):
    c = c_ref[...]
    cond = (c * jax.nn.sigmoid(c)).astype(BF16)
    o_ref[...] = jnp.dot(cond, w_ref[...].astype(BF16), preferred_element_type=F32) + b_ref[...]


def ada_ln(c, w_ada, b_ada):
    depth, d, n = w_ada.shape
    b = c.shape[0]
    tn = 1536
    return pl.pallas_call(
        _ada_kernel,
        out_shape=jax.ShapeDtypeStruct((depth, b, n), F32),
        grid=(depth, n // tn),
        in_specs=[
            pl.BlockSpec((b, d), lambda l, j: (0, 0)),
            pl.BlockSpec((None, d, tn), lambda l, j: (l, 0, j)),
            pl.BlockSpec((None, 1, tn), lambda l, j: (l, 0, j)),
        ],
        out_specs=pl.BlockSpec((None, b, tn), lambda l, j: (l, 0, j)),
        compiler_params=_cparams(("arbitrary", "arbitrary")),
        name="ada_ln",
    )(c, w_ada, b_ada.reshape(depth, 1, n))


def _inproj_kernel(x_ref, sc_ref, sh_ref, w_ref, o_ref, u_ref):
    @pl.when(pl.program_id(2) == 0)
    def _():
        u_ref[...] = (x_ref[...] * (1.0 + sc_ref[...]) + sh_ref[...]).astype(BF16)

    o_ref[...] = jnp.dot(u_ref[...], w_ref[...], preferred_element_type=F32).astype(o_ref.dtype)


def in_proj(x, sc, sh, w):
    b, s, d = x.shape
    n = w.shape[1]
    tm, tn = 1024, 1536
    return pl.pallas_call(
        _inproj_kernel,
        out_shape=jax.ShapeDtypeStruct((b, s, n), BF16),
        grid=(b, s // tm, n // tn),
        in_specs=[
            pl.BlockSpec((None, tm, d), lambda bi, i, j: (bi, i, 0)),
            pl.BlockSpec((None, 1, d), lambda bi, i, j: (bi, 0, 0)),
            pl.BlockSpec((None, 1, d), lambda bi, i, j: (bi, 0, 0)),
            pl.BlockSpec((d, tn), lambda bi, i, j: (0, j)),
        ],
        out_specs=pl.BlockSpec((None, tm, tn), lambda bi, i, j: (bi, i, j)),
        scratch_shapes=[pltpu.VMEM((tm, d), BF16)],
        compiler_params=_cparams(("parallel", "parallel", "arbitrary")),
        name="in_proj",
    )(x, sc, sh, w)


def _inproj_dil_kernel(x_ref, sc_ref, sh_ref, w_ref, o1_ref, o2_ref, res_s):
    tm = x_ref.shape[0]
    u = (x_ref[...] * (1.0 + sc_ref[...]) + sh_ref[...]).astype(BF16)
    res = jnp.dot(u, w_ref[...], preferred_element_type=F32)
    nslab = QKV_W // 128
    for c in range(2 * nslab):
        res_s[c] = res[:, c * 128:(c + 1) * 128]
    for g, o_ref in ((0, o1_ref), (1, o2_ref)):
        dil = o_ref.shape[0]
        for r in range(dil):
            for c in range(nslab):
                o_ref[r, :, c * 128:(c + 1) * 128] = res_s[g * nslab + c, pl.ds(r, tm // dil, stride=dil), :].astype(
                    o_ref.dtype)


def in_proj_dil(x, sc, sh, w):
    b, s, d = x.shape
    tm = 1024
    d1, d2 = DIL_GROUPS[1][1], DIL_GROUPS[2][1]
    return pl.pallas_call(
        _inproj_dil_kernel,
        out_shape=(jax.ShapeDtypeStruct((b, d1, s // d1, QKV_W), BF16),
                   jax.ShapeDtypeStruct((b, d2, s // d2, QKV_W), BF16)),
        grid=(b, s // tm),
        in_specs=[
            pl.BlockSpec((None, tm, d), lambda bi, i: (bi, i, 0)),
            pl.BlockSpec((None, 1, d), lambda bi, i: (bi, 0, 0)),
            pl.BlockSpec((None, 1, d), lambda bi, i: (bi, 0, 0)),
            pl.BlockSpec((d, 2 * QKV_W), lambda bi, i: (0, 0)),
        ],
        out_specs=(
            pl.BlockSpec((None, d1, tm // d1, QKV_W), lambda bi, i: (bi, 0, i, 0)),
            pl.BlockSpec((None, d2, tm // d2, QKV_W), lambda bi, i: (bi, 0, i, 0)),
        ),
        scratch_shapes=[pltpu.VMEM((2 * QKV_W // 128, tm, 128), F32)],
        compiler_params=_cparams(("parallel", "parallel")),
        name="in_proj_dil",
    )(x, sc, sh, w)


def _rope2(x, cos, sin_signed, lo):
    xr = jnp.where(lo, pltpu.roll(x, 96, 1), pltpu.roll(x, 32, 1))
    return x * cos + xr * sin_signed


def _lane_masks():
    lane = lax.broadcasted_iota(jnp.int32, (1, 128), 1)
    return (lane % 64) < 32, lane < 64


def _band_mask(q0, nq, nk, pad, radius, length):
    qpos = q0 + (lax.broadcasted_iota(jnp.int32, (nq, 1), 0) & 127)
    kpos = q0 - pad + lax.broadcasted_iota(jnp.int32, (1, nk), 1)
    kpos = jnp.where(kpos < 0, -4 * length, jnp.where(kpos >= length, -4 * length, kpos))
    return jnp.abs(qpos - kpos) <= radius


def _dil_attn_kernel(qkv_ref, cos_ref, sin_ref, olo_ref, ohi_ref, llo_ref, lhi_ref, q_s, k_s, v_s, *, dil, ls):
    lo, head0 = _lane_masks()
    heads = (head0, jnp.logical_not(head0))
    cos = cos_ref[...]
    sin = sin_ref[...]
    pad = DIL_RADIUS
    qb = 128
    nblk = ls // qb
    ngroup = 8
    zeros = jnp.zeros((dil, pad, 128), BF16)
    k_s[:, 0:pad, :] = zeros
    k_s[:, pad + ls:pad + ls + pad, :] = zeros
    v_s[:, 0:pad, :] = zeros
    v_s[:, pad + ls:pad + ls + pad, :] = zeros
    for hp, (o_ref, l_ref) in enumerate(((olo_ref, llo_ref), (ohi_ref, lhi_ref))):
        q = qkv_ref[:, :, hp * 128:(hp + 1) * 128].astype(F32).reshape(dil * ls, 128)
        k = qkv_ref[:, :, 256 + hp * 128:256 + (hp + 1) * 128].astype(F32).reshape(dil * ls, 128)
        q_s[...] = (_rope2(q, cos, sin, lo) * (HEAD_DIM ** -0.5)).astype(BF16)
        k_s[:, pad:pad + ls, :] = _rope2(k, cos, sin, lo).astype(BF16).reshape(dil, ls, 128)
        v_s[:, pad:pad + ls, :] = qkv_ref[:, :, 512 + hp * 128:512 + (hp + 1) * 128]

        def grp(gi, carry):
            tiles = []
            for t in range(ngroup):
                c = gi * ngroup + t
                r = c // nblk
                q0 = pl.multiple_of((c % nblk) * qb, qb)
                qblk = q_s[pl.ds(pl.multiple_of(c * qb, qb), qb), :]
                kw = k_s[r, pl.ds(q0, qb + 2 * pad), :]
                vw = v_s[r, pl.ds(q0, qb + 2 * pad), :]
                mask = _band_mask(q0, qb, qb + 2 * pad, pad, DIL_RADIUS, ls)
                ss = [lax.dot_general(jnp.where(hm, qblk, jnp.zeros_like(qblk)), kw, (((1,), (1,)), ((), ())),
                                      preferred_element_type=F32) for hm in heads]
                tiles.append((r, q0, vw, mask, ss))
            soft = []
            for r, q0, vw, mask, ss in tiles:
                for s in ss:
                    s = jnp.where(mask, s, NEG_INF)
                    m = jnp.max(s, axis=-1, keepdims=True)
                    p = jnp.exp(s - m)
                    soft.append((p.astype(BF16), jnp.sum(p, axis=-1, keepdims=True), m))
            for ti, (r, q0, vw, mask, ss) in enumerate(tiles):
                res = []
                for h in range(2):
                    p, den, m = soft[2 * ti + h]
                    res.append((jnp.dot(p, vw, preferred_element_type=F32) / den, m + jnp.log(den)))
                o = jnp.where(head0, res[0][0], res[1][0])
                lse = jnp.where(head0, res[0][1], res[1][1])
                rows = pl.ds(q0, qb) if dil == 1 else pl.ds(q0 * dil + r, qb, stride=dil)
                o_ref[rows, :] = o
                l_ref[rows, :] = lse
            return carry

        lax.fori_loop(0, dil * nblk // ngroup, grp, 0)


def dilated_attention(qkv, col_block, cos_d, sin_d):
    b, dil, ls, _ = qkv.shape
    s = dil * ls
    out_sds = jax.ShapeDtypeStruct((b, s, 128), F32)
    out_spec = pl.BlockSpec((None, s, 128), lambda bi: (bi, 0, 0))
    return pl.pallas_call(
        functools.partial(_dil_attn_kernel, dil=dil, ls=ls),
        out_shape=(out_sds,) * 4,
        grid=(b,),
        in_specs=[
            pl.BlockSpec((None, dil, ls, QKV_W), lambda bi: (bi, 0, 0, col_block)),
            pl.BlockSpec((s, 128), lambda bi: (0, 0)),
            pl.BlockSpec((s, 128), lambda bi: (0, 0)),
        ],
        out_specs=(out_spec,) * 4,
        scratch_shapes=[
            pltpu.VMEM((s, 128), BF16),
            pltpu.VMEM((dil, ls + 2 * DIL_RADIUS, 128), BF16),
            pltpu.VMEM((dil, ls + 2 * DIL_RADIUS, 128), BF16),
        ],
        compiler_params=_cparams(("parallel",)),
        name=f"dil_attn_d{dil}",
    )(qkv, cos_d, sin_d)


def _swa_kernel(qkv_ref, cos_ref, sin_ref, sink_ref, o_ref, q_s, k_s, v_s, *, s_len):
    lo, head0 = _lane_masks()
    heads = (head0, jnp.logical_not(head0))
    cos = cos_ref[...]
    sin = sin_ref[...]
    pad = SWA_RADIUS
    qb = 128
    grp = SWA_Q_HEADS // SWA_KV_HEADS
    zeros = jnp.zeros((pad, 128), BF16)
    k_s[0:pad, :] = zeros
    k_s[pad + s_len:pad + s_len + pad, :] = zeros
    v_s[0:pad, :] = zeros
    v_s[pad + s_len:pad + s_len + pad, :] = zeros
    k = qkv_ref[:, 512:640].astype(F32)
    k_s[pad:pad + s_len, :] = _rope2(k, cos, sin, lo).astype(BF16)
    v_s[pad:pad + s_len, :] = qkv_ref[:, 640:768]
    for qp in range(grp):
        q = qkv_ref[:, qp * 128:(qp + 1) * 128].astype(F32)
        q_s[qp] = (_rope2(q, cos, sin, lo) * (HEAD_DIM ** -0.5)).astype(BF16)
    sinks = [jnp.concatenate([jnp.broadcast_to(sink_ref[kv * grp + p:kv * grp + p + 1, 0:1], (qb, 1))
                              for p in range(grp)], axis=0) for kv in range(SWA_KV_HEADS)]

    def blk(i, carry):
        q0 = pl.multiple_of(i * qb, qb)
        kw = k_s[pl.ds(q0, qb + 2 * pad), :]
        vw = v_s[pl.ds(q0, qb + 2 * pad), :]
        qall = jnp.concatenate([q_s[p, pl.ds(q0, qb), :] for p in range(grp)], axis=0)
        mask = _band_mask(q0, grp * qb, qb + 2 * pad, pad, SWA_RADIUS, s_len)
        ss = [lax.dot_general(jnp.where(hm, qall, jnp.zeros_like(qall)), kw, (((1,), (1,)), ((), ())),
                              preferred_element_type=F32) for hm in heads]
        soft = []
        for s, sk in zip(ss, sinks):
            s = jnp.where(mask, s, NEG_INF)
            m = jnp.maximum(jnp.max(s, axis=-1, keepdims=True), sk)
            p = jnp.exp(s - m)
            soft.append((p.astype(BF16), jnp.sum(p, axis=-1, keepdims=True) + jnp.exp(sk - m)))
        res = [jnp.dot(p, vw, preferred_element_type=F32) / den for p, den in soft]
        o = jnp.where(head0, res[0], res[1]).astype(o_ref.dtype)
        for p in range(grp):
            o_ref[pl.ds(q0, qb), p * 128:(p + 1) * 128] = o[p * qb:(p + 1) * qb, :]
        return carry

    lax.fori_loop(0, s_len // qb, blk, 0, unroll=8)


def swa_attention(proj, cos_t, sin_t, sink):
    b, s, n = proj.shape
    return pl.pallas_call(
        functools.partial(_swa_kernel, s_len=s),
        out_shape=jax.ShapeDtypeStruct((b, s, 512), BF16),
        grid=(b,),
        in_specs=[
            pl.BlockSpec((None, s, QKV_W), lambda bi: (bi, 0, COL_SWA // QKV_W)),
            pl.BlockSpec((s, 128), lambda bi: (0, 0)),
            pl.BlockSpec((s, 128), lambda bi: (0, 0)),
            pl.BlockSpec((8, 128), lambda bi: (0, 0)),
        ],
        out_specs=pl.BlockSpec((None, s, 512), lambda bi: (bi, 0, 0)),
        scratch_shapes=[
            pltpu.VMEM((SWA_Q_HEADS // SWA_KV_HEADS, s, 128), BF16),
            pltpu.VMEM((s + 2 * SWA_RADIUS, 128), BF16),
            pltpu.VMEM((s + 2 * SWA_RADIUS, 128), BF16),
        ],
        compiler_params=_cparams(("parallel",)),
        name="swa_attn",
    )(proj, cos_t, sin_t, jnp.broadcast_to(sink.astype(F32)[:, None], (8, 128)))


HY_CHUNKS = 4


def dft_table(seq):
    kc = seq // HY_CHUNKS
    s_idx = jnp.arange(seq, dtype=jnp.int32)
    k = jnp.where(s_idx < seq // 2, s_idx, seq + seq // 2 - 1 - s_idx)
    phase = ((2 * k[:, None] + 1) * s_idx[None, :]) % (4 * seq)
    ang = phase.astype(F32) * (2.0 * math.pi / (4 * seq))
    c = jnp.cos(ang).reshape(HY_CHUNKS, kc, seq)
    s = jnp.sin(ang).reshape(HY_CHUNKS, kc, seq)
    return jnp.concatenate([c, s], axis=1).astype(BF16)


def _hy_filter_kernel(w1_ref, b1_ref, w2_ref, b2_ref, w3_ref, fr_ref, ld_ref, bands_ref, f_ref,
                      hr_ref, hi_ref, hs_s, hd_s, *, seq):
    hw = HYENA_WIDTH
    hi_p = lax.Precision.HIGHEST

    @pl.when(pl.program_id(2) == 0)
    def _():
        idx = lax.broadcasted_iota(jnp.int32, (seq, 1), 0).astype(F32)
        t = idx / max(seq - 1, 1)
        w = 2.0 * math.pi * idx / seq
        ang = w * bands_ref[...]
        w1 = w1_ref[...]
        pre = (t * w1[0:1, :]
               + jnp.dot(jnp.cos(ang), w1[1:1 + HYENA_BANDS, :], precision=hi_p, preferred_element_type=F32)
               - jnp.dot(jnp.sin(ang), w1[1 + HYENA_BANDS:1 + 2 * HYENA_BANDS, :], precision=hi_p,
                         preferred_element_type=F32)
               + b1_ref[...])
        h = jnp.sin(fr_ref[0:1, :] * pre)
        h = jnp.sin(fr_ref[1:2, :] * (jnp.dot(h, w2_ref[...], precision=hi_p, preferred_element_type=F32)
                                      + b2_ref[...]))
        h = jnp.dot(h, w3_ref[...], precision=hi_p, preferred_element_type=F32)
        h = h * jnp.exp(-t * jnp.exp(ld_ref[...]))
        hf = h[:, :hw]
        hb = jnp.where(idx > 0.0, h[:, hw:], 0.0)
        inv = lax.rsqrt(jnp.sum(hf * hf + hb * hb, axis=0, keepdims=True) + 1e-12)
        hs_s[...] = ((hf + hb) * inv).astype(BF16)
        hd_s[...] = ((hf - hb) * inv).astype(BF16)

    kc = seq // HY_CHUNKS
    hr_ref[...] = jnp.dot(f_ref[0:kc, :], hs_s[...], preferred_element_type=F32)
    hi_ref[...] = -jnp.dot(f_ref[kc:2 * kc, :], hd_s[...], preferred_element_type=F32)


def hyena_filters(w1, b1, w2, b2, w3p, freq, ldp, fwd, seq):
    depth = w1.shape[0]
    kc = seq // HY_CHUNKS
    hw = HYENA_WIDTH
    bands = jnp.linspace(1e-4, HYENA_BANDS - 1, HYENA_BANDS, dtype=F32).reshape(1, HYENA_BANDS)
    full = lambda *shape: pl.BlockSpec((None,) + shape, lambda l, o, c: (l,) + (0,) * len(shape))
    out_sds = jax.ShapeDtypeStruct((depth, 2, seq, hw), F32)
    return pl.pallas_call(
        functools.partial(_hy_filter_kernel, seq=seq),
        out_shape=(out_sds, out_sds),
        grid=(depth, 2, HY_CHUNKS),
        in_specs=[
            full(*w1.shape[1:]), full(1, b1.shape[-1]), full(*w2.shape[1:]), full(1, b2.shape[-1]),
            pl.BlockSpec((None, w3p.shape[1], 2 * hw), lambda l, o, c: (l, 0, o)),
            full(*freq.shape[1:]),
            pl.BlockSpec((None, 1, 2 * hw), lambda l, o, c: (l, 0, o)),
            pl.BlockSpec((1, HYENA_BANDS), lambda l, o, c: (0, 0)),
            pl.BlockSpec((None, 2 * kc, seq), lambda l, o, c: (c, 0, 0)),
        ],
        out_specs=(
            pl.BlockSpec((None, None, kc, hw), lambda l, o, c: (l, o, c, 0)),
            pl.BlockSpec((None, None, kc, hw), lambda l, o, c: (l, o, c, 0)),
        ),
        scratch_shapes=[pltpu.VMEM((seq, hw), BF16), pltpu.VMEM((seq, hw), BF16)],
        compiler_params=_cparams(("arbitrary", "arbitrary", "arbitrary")),
        name="hyena_filters",
    )(w1, b1[:, None, :], w2, b2[:, None, :], w3p, freq, ldp[:, None, :], bands, fwd)


def dft_tables_r2(seq):
    half = seq // 2
    kc = half // HY_CHUNKS
    k = jnp.arange(half, dtype=jnp.int32)[:, None]
    n = jnp.arange(half, dtype=jnp.int32)[None, :]
    out = []
    for off in (0, 1):
        ang = (((2 * k + 1) * (2 * n + off)) % (4 * seq)).astype(F32) * (2.0 * math.pi / (4 * seq))
        c = jnp.cos(ang).reshape(HY_CHUNKS, kc, half)
        s = jnp.sin(ang).reshape(HY_CHUNKS, kc, half)
        out.append(jnp.concatenate([c, s], axis=1).astype(BF16))
        out.append((jnp.concatenate([c, -s], axis=1) * (1.0 / seq)).transpose(0, 2, 1).astype(BF16))
    return out[0], out[2], out[1], out[3]


def _hyena_kernel(hy_ref, cw_ref, cb_ref, bias_ref, hr_ref, hi_ref, hrp_ref, hip_ref, fe_ref, fo_ref, ge_ref, go_ref,
                  o_ref, zb_s, zf_s, acc_s, slab_s, *, seq):
    hw = HYENA_WIDTH
    half = seq // 2
    o = pl.program_id(1)
    c = pl.program_id(2)
    kc = half // HY_CHUNKS
    nslab = hw // 128

    def short_conv(part):
        x = hy_ref[:, part * hw:(part + 1) * hw].astype(F32)
        row = lax.broadcasted_iota(jnp.int32, (seq, 1), 0)
        xm = jnp.where(row == 0, 0.0, pltpu.roll(x, 1, 0))
        xp = jnp.where(row == seq - 1, 0.0, pltpu.roll(x, seq - 1, 0))
        w = cw_ref[:, part * hw:(part + 1) * hw]
        return cb_ref[:, part * hw:(part + 1) * hw] + xm * w[0:1, :] + x * w[1:2, :] + xp * w[2:3, :]

    def split(x):
        for sl in range(nslab):
            slab_s[sl] = x[:, sl * 128:(sl + 1) * 128]
        return tuple(jnp.concatenate([slab_s[sl, pl.ds(par, half, stride=2), :] for sl in range(nslab)], axis=-1)
                     for par in (0, 1))

    def interleave(xe, xo):
        for sl in range(nslab):
            slab_s[sl, pl.ds(0, half, stride=2), :] = xe[:, sl * 128:(sl + 1) * 128]
            slab_s[sl, pl.ds(1, half, stride=2), :] = xo[:, sl * 128:(sl + 1) * 128]
        return jnp.concatenate([slab_s[sl] for sl in range(nslab)], axis=-1)

    @pl.when((o == 0) & (c == 0))
    def _():
        for par, z in enumerate(split(short_conv(0))):
            zf_s[par] = z
            zb_s[par] = z.astype(BF16)

    ze = jnp.dot(fe_ref[...], zb_s[0], preferred_element_type=F32)
    zo = jnp.dot(fo_ref[...], zb_s[1], preferred_element_type=F32)
    zc, zs = ze[:kc] + zo[:kc], ze[kc:] + zo[kc:]
    zcp, zsp = ze[:kc] - zo[:kc], zo[kc:] - ze[kc:]
    hr, hi, hrp, hip = hr_ref[...], hi_ref[...], hrp_ref[...], hip_ref[...]
    yr, yi = zc * hr + zs * hi, zc * hi - zs * hr
    yrp, yip = zcp * hrp + zsp * hip, zcp * hip - zsp * hrp
    pe = jnp.dot(ge_ref[...], jnp.concatenate([yr + yrp, yi - yip], axis=0).astype(BF16),
                 preferred_element_type=F32)
    po = jnp.dot(go_ref[...], jnp.concatenate([yr - yrp, yi + yip], axis=0).astype(BF16),
                 preferred_element_type=F32)

    @pl.when(c == 0)
    def _():
        acc_s[0] = pe
        acc_s[1] = po

    @pl.when(c > 0)
    def _():
        acc_s[0] += pe
        acc_s[1] += po

    @pl.when((c == HY_CHUNKS - 1) & (o == 0))
    def _():
        for par, p1 in enumerate(split(short_conv(1))):
            z1 = p1 * (acc_s[par] + bias_ref[0:1, :] * zf_s[par])
            zf_s[par] = z1
            zb_s[par] = z1.astype(BF16)

    @pl.when((c == HY_CHUNKS - 1) & (o == 1))
    def _():
        halves = [p2 * (acc_s[par] + bias_ref[1:2, :] * zf_s[par]) for par, p2 in enumerate(split(short_conv(2)))]
        o_ref[...] = interleave(*halves).astype(o_ref.dtype)


def hyena_mixer(proj, conv_w, conv_b, hy_bias, hr, hi, tabs):
    b, s, n = proj.shape
    hw = HYENA_WIDTH
    half = s // 2
    kc = half // HY_CHUNKS
    nck = HY_CHUNKS
    spec = pl.BlockSpec((None, kc, hw), lambda bi, o, c: (o, c, 0))
    spec_p = pl.BlockSpec((None, kc, hw), lambda bi, o, c: (o, nck + c, 0))
    fwd_spec = pl.BlockSpec((None, 2 * kc, half), lambda bi, o, c: (c, 0, 0))
    inv_spec = pl.BlockSpec((None, half, 2 * kc), lambda bi, o, c: (c, 0, 0))
    fe, fo, ge, go = tabs
    return pl.pallas_call(
        functools.partial(_hyena_kernel, seq=s),
        out_shape=jax.ShapeDtypeStruct((b, s, hw), BF16),
        grid=(b, 2, nck),
        in_specs=[
            pl.BlockSpec((None, s, 3 * hw), lambda bi, o, c: (bi, 0, COL_HY // (3 * hw))),
            pl.BlockSpec((3, 3 * hw), lambda bi, o, c: (0, 0)),
            pl.BlockSpec((1, 3 * hw), lambda bi, o, c: (0, 0)),
            pl.BlockSpec((2, hw), lambda bi, o, c: (0, 0)),
            spec, spec, spec_p, spec_p, fwd_spec, fwd_spec, inv_spec, inv_spec,
        ],
        out_specs=pl.BlockSpec((None, s, hw), lambda bi, o, c: (bi, 0, 0)),
        scratch_shapes=[pltpu.VMEM((2, half, hw), BF16), pltpu.VMEM((2, half, hw), F32),
                        pltpu.VMEM((2, half, hw), F32), pltpu.VMEM((hw // 128, s, 128), F32)],
        compiler_params=_cparams(("parallel", "arbitrary", "arbitrary")),
        name="hyena_conv",
    )(proj, conv_w, conv_b.reshape(1, -1), hy_bias, hr, hi, hr, hi, fe, fo, ge, go)


def _layer_norm(y, g, b):
    mu = jnp.mean(y, axis=-1, keepdims=True)
    yc = y - mu
    var = jnp.mean(yc * yc, axis=-1, keepdims=True)
    return yc * lax.rsqrt(var + LN_EPS) * g + b


def _merge_kernel(*refs):
    att = refs[:12]
    (yb_ref, yc_ref, gl_ref, x_ref, g1_ref, sc2_ref, sh2_ref, lng_ref, lnb_ref, wa_ref, wb_ref, wc_ref, wo_ref,
     xo_ref, u2_ref) = refs[12:]
    d = D_MODEL
    halves = []
    for half in range(2):
        la, lb, lc = (att[4 * g + 2 + half][...] for g in range(3))
        m = jnp.maximum(jnp.maximum(la, lb), lc)
        ea, eb, ec = jnp.exp(la - m), jnp.exp(lb - m), jnp.exp(lc - m)
        inv = 1.0 / (ea + eb + ec)
        halves.append((ea * inv) * att[half][...] + (eb * inv) * att[4 + half][...] + (ec * inv) * att[8 + half][...])
    ya = jnp.concatenate(halves, axis=-1)
    za = jnp.dot(ya.astype(BF16), wa_ref[...], preferred_element_type=F32)
    zb = jnp.dot(yb_ref[...], wb_ref[...], preferred_element_type=F32)
    zc = jnp.dot(yc_ref[...], wc_ref[...], preferred_element_type=F32)
    merged = (jax.nn.sigmoid(gl_ref[:, 0:d].astype(F32)) * za
              + jax.nn.sigmoid(gl_ref[:, d:2 * d].astype(F32)) * zb
              + jax.nn.sigmoid(gl_ref[:, 2 * d:3 * d].astype(F32)) * zc)
    mix = jnp.dot(merged.astype(BF16), wo_ref[...], preferred_element_type=F32)
    xn = _layer_norm(ALPHA * x_ref[...] + g1_ref[...] * mix, lng_ref[...], lnb_ref[...])
    xo_ref[...] = xn
    u2_ref[...] = (xn * (1.0 + sc2_ref[...]) + sh2_ref[...]).T.astype(BF16)


def merge_mixers(att, yb, yc, proj, x, g1, sc2, sh2, ln_g, ln_b, wa, wb, wc, wo):
    b, s, d = x.shape
    tm = 512
    row = lambda w: pl.BlockSpec((None, tm, w), lambda bi, i: (bi, i, 0))
    per_b = pl.BlockSpec((None, 1, d), lambda bi, i: (bi, 0, 0))
    const = lambda r, c: pl.BlockSpec((r, c), lambda bi, i: (0, 0))
    return pl.pallas_call(
        _merge_kernel,
        out_shape=(jax.ShapeDtypeStruct((b, s, d), F32), jax.ShapeDtypeStruct((d, b * s), BF16)),
        grid=(b, s // tm),
        in_specs=[row(128)] * 12 + [row(512), row(512), row(3 * d), row(d), per_b, per_b, per_b,
                                    const(1, d), const(1, d), const(256, d), const(512, d), const(512, d),
                                    const(d, d)],
        out_specs=(row(d), pl.BlockSpec((d, tm), lambda bi, i: (0, bi * (s // tm) + i))),
        compiler_params=_cparams(("parallel", "parallel")),
        name="merge_mixers",
    )(*att, yb, yc, proj, x, g1, sc2, sh2, ln_g.reshape(1, d), ln_b.reshape(1, d), wa, wb, wc, wo)


_PEER_PAIRS = [(i, j) for i in range(1, PEER_TOPK + 1) for j in range(1, PEER_TOPK + 1) if i * j <= PEER_TOPK]
PEER_NOT_TOP = 100.0


def _sorted_prefix_count(vals, pred):
    w = jnp.where
    t8 = pred(vals[7])
    t4 = pred(w(t8, vals[11], vals[3]))
    t2 = pred(w(t8, w(t4, vals[13], vals[9]), w(t4, vals[5], vals[1])))
    lo = w(t4, w(t2, vals[6], vals[4]), w(t2, vals[2], vals[0]))
    hi = w(t4, w(t2, vals[14], vals[12]), w(t2, vals[10], vals[8]))
    t1 = pred(w(t8, hi, lo))
    t16 = pred(vals[15])
    count = w(t8, 8.0, 0.0) + w(t4, 4.0, 0.0) + w(t2, 2.0, 0.0) + w(t1, 1.0, 0.0) + w(t16, 1.0, 0.0)
    return count, t16


def _peer_route_kernel(u_ref, wq_ref, keys_ref, e1_ref, cut_ref, e2_ref, r2_ref, q_s, s_s, ab_s, st_s, *, tr):
    nh, nk, topk = PEER_HEADS, PEER_KEYS, PEER_TOPK
    qt = jnp.dot(wq_ref[...], u_ref[...], preferred_element_type=F32)
    q_s[...] = qt.astype(BF16)
    for ph in range(2 * nh):
        s_s[ph] = jnp.dot(keys_ref[ph], q_s[ph * 128:(ph + 1) * 128, :], preferred_element_type=F32)

    def extract(h, carry):
        for p in range(2):
            sv = s_s[p * nh + h]
            v = [sv[g * 8:(g + 1) * 8, :] for g in range(topk)]
            k = 2
            while k <= topk:
                jdist = k // 2
                while jdist >= 1:
                    for a in range(topk):
                        b = a ^ jdist
                        if b > a:
                            hi, lo = jnp.maximum(v[a], v[b]), jnp.minimum(v[a], v[b])
                            v[a], v[b] = (hi, lo) if (a & k) == 0 else (lo, hi)
                    jdist //= 2
                k *= 2
            for shift in (4, 2, 1):
                v = [jnp.maximum(v[a], pltpu.roll(v[topk - 1 - a], shift, 0)) for a in range(topk)]
                jdist = topk // 2
                while jdist >= 1:
                    for a in range(topk):
                        b = a ^ jdist
                        if b > a:
                            v[a], v[b] = jnp.maximum(v[a], v[b]), jnp.minimum(v[a], v[b])
                    jdist //= 2
            for r in range(topk):
                ab_s[p, r, pl.ds(h, 1), :] = v[r][0:1, :]
        return carry

    lax.fori_loop(0, nh, extract, 0)

    for ch in range(tr // 128):
        ln = slice(ch * 128, (ch + 1) * 128)
        a = [ab_s[0, r, :, ln] for r in range(topk)]
        b = [ab_s[1, r, :, ln] for r in range(topk)]
        cand = [a[i - 1] + b[j - 1] for (i, j) in _PEER_PAIRS]
        tau = jnp.full((nh, 128), -jnp.inf, F32)
        for x, (ix, jx) in enumerate(_PEER_PAIRS):
            cnt = jnp.zeros((nh, 128), F32)
            for y, (iy, jy) in enumerate(_PEER_PAIRS):
                if iy <= ix and jy <= jx:
                    cnt = cnt + 1.0
                elif iy >= ix and jy >= jx:
                    continue
                else:
                    cnt = cnt + jnp.where(cand[y] >= cand[x], 1.0, 0.0)
            tau = jnp.maximum(tau, jnp.where(cnt >= float(topk), cand[x], -jnp.inf))
        top = cand[0]
        z = jnp.zeros((nh, 128), F32)
        for cx in cand:
            z = z + jnp.where(cx >= tau, jnp.exp(cx - top), 0.0)
        st_s[0, :, ln] = tau
        st_s[1, :, ln] = 1.0 / z

    for h in range(nh):
        rows = slice(h * nk, (h + 1) * nk)
        s1 = s_s[h]
        s2 = s_s[nh + h]
        hrow = slice(h, h + 1)
        tau = st_s[0, hrow, :]
        in1 = s1 >= ab_s[0, topk - 1, hrow, :]
        bs = [ab_s[1, r, hrow, :] for r in range(topk)]
        cut, _ = _sorted_prefix_count(bs, lambda v: s1 + v >= tau)
        above, below_all = _sorted_prefix_count(bs, lambda v: v > s2)
        e1_ref[rows, :] = jnp.where(in1, jnp.exp(s1 - ab_s[0, 0, hrow, :]), 0.0)
        cut_ref[rows, :] = jnp.where(in1, cut, 0.0)
        e2_ref[rows, :] = jnp.where(below_all, 0.0, jnp.exp(s2 - bs[0]) * st_s[1, hrow, :]).astype(BF16)
        r2_ref[rows, :] = jnp.where(below_all, PEER_NOT_TOP, above + 1.0).astype(BF16)


def peer_route(u2, wq_t, keys):
    d, t = u2.shape
    tr = 512
    nrow = PEER_HEADS * PEER_KEYS
    tab = lambda dt: jax.ShapeDtypeStruct((nrow, t), dt)
    col = pl.BlockSpec((nrow, tr), lambda i: (0, i))
    return pl.pallas_call(
        functools.partial(_peer_route_kernel, tr=tr),
        out_shape=(tab(F32), tab(F32), tab(BF16), tab(BF16)),
        grid=(t // tr,),
        in_specs=[
            pl.BlockSpec((d, tr), lambda i: (0, i)),
            pl.BlockSpec(wq_t.shape, lambda i: (0, 0)),
            pl.BlockSpec(keys.shape, lambda i: (0, 0, 0)),
        ],
        out_specs=(col, col, col, col),
        scratch_shapes=[
            pltpu.VMEM((2 * nrow, tr), BF16),
            pltpu.VMEM((2 * PEER_HEADS, PEER_KEYS, tr), F32),
            pltpu.VMEM((2, PEER_TOPK, PEER_HEADS, tr), F32),
            pltpu.VMEM((2, PEER_HEADS, tr), F32),
        ],
        compiler_params=_cparams(("parallel",)),
        name="peer_route",
    )(u2, wq_t, keys)


F8 = jnp.float8_e4m3fn
F8_MAX = 448.0
WG_SCALE = 16.0


def _peer_dense_kernel(sc_ref, x_ref, u_ref, vt_ref, e1_ref, cut_ref, e2_ref, r2_ref, xres_ref, g2_ref, lng_ref,
                       lnb_ref, o_ref, acc_s, wga_s, wgb_s, *, te, n_e):
    j = pl.program_id(1)
    nk = PEER_KEYS
    tn = x_ref.shape[1]
    sub = 16

    @pl.when(j == 0)
    def _():
        acc_s[...] = jnp.zeros_like(acc_s)
        wgb_s[...] = jnp.zeros_like(wgb_s)

    def step(w_cur, w_prev):
        inv_a = sc_ref[0]
        ts = 512
        c_lin = jnp.full((1, ts), 0.5 * WG_SCALE * inv_a, F32).astype(BF16)
        c_erf = jnp.full((1, ts), (2.0 ** -0.5) * inv_a, F32).astype(BF16)

        def row_bcast(ref, h, ii, lanes):
            chunks = [jnp.broadcast_to(ref[h, ii:ii + 1, c0:c0 + 128], (sub, 128)).astype(BF16)
                      for c0 in range(lanes.start, lanes.stop, 128)]
            return jnp.concatenate(chunks, axis=-1)[None]

        for sl in range(tn // ts):
            lanes = slice(sl * ts, (sl + 1) * ts)
            for ii in range(te // nk):
                rows = slice(ii * nk, (ii + 1) * nk)
                at = jnp.dot(u_ref[rows, :], x_ref[:, lanes], preferred_element_type=F32).astype(BF16)
                if ii == 1:
                    acc_s[:, lanes] += jnp.dot(vt_ref[...], w_prev[:, lanes], preferred_element_type=F32)
                g = (c_lin * at) * (1.0 + lax.erf(at * c_erf))
                w = None
                for h in range(PEER_HEADS):
                    cb = row_bcast(cut_ref, h, ii, lanes)
                    e1 = row_bcast(e1_ref, h, ii, lanes)
                    r2 = r2_ref[h * nk:(h + 1) * nk, lanes].reshape(nk // sub, sub, ts)
                    e2 = e2_ref[h * nk:(h + 1) * nk, lanes].reshape(nk // sub, sub, ts)
                    sel = jnp.where(r2 <= cb, e2, jnp.zeros((), BF16)) * e1
                    w = sel if w is None else w + sel
                wg = jnp.clip(g * w.reshape(nk, ts), -F8_MAX, F8_MAX)
                w_cur[rows, lanes] = wg.astype(F8)

    @pl.when((j % 2 == 0) & (j < n_e))
    def _():
        step(wga_s, wgb_s)

    @pl.when(j % 2 == 1)
    def _():
        step(wgb_s, wga_s)

    @pl.when(j == n_e)
    def _():
        acc = acc_s[...] + jnp.dot(vt_ref[...], wgb_s[...], preferred_element_type=F32)
        ffn = (acc * sc_ref[1]).T
        o_ref[...] = _layer_norm(ALPHA * xres_ref[...] + g2_ref[...] * ffn, lng_ref[...], lnb_ref[...])


def peer_dense(scales, x8, u_tab, vt_tab, e1, cut, e2, r2, x_res, g2, ln_g, ln_b):
    d, t = x8.shape
    b, s, _ = x_res.shape
    ne = u_tab.shape[0]
    tn, te = 1024, 1024
    n_e = ne // te
    nrow = PEER_HEADS * PEER_KEYS
    col = pl.BlockSpec((nrow, tn), lambda i, j: (0, i))
    step_rows = pl.BlockSpec((PEER_HEADS, te // PEER_KEYS, tn), lambda i, j: (0, jnp.minimum(j, n_e - 1), i))
    e1 = e1.reshape(PEER_HEADS, PEER_KEYS, t)
    cut = cut.reshape(PEER_HEADS, PEER_KEYS, t)
    row = pl.BlockSpec((None, tn, d), lambda i, j: (i // (s // tn), i % (s // tn), 0))
    const = pl.BlockSpec((1, d), lambda i, j: (0, 0))
    return pl.pallas_call(
        functools.partial(_peer_dense_kernel, te=te, n_e=n_e),
        out_shape=jax.ShapeDtypeStruct((b, s, d), F32),
        grid=(t // tn, n_e + 1),
        in_specs=[
            pl.BlockSpec(memory_space=pltpu.SMEM),
            pl.BlockSpec((d, tn), lambda i, j: (0, i)),
            pl.BlockSpec((te, d), lambda i, j: (jnp.minimum(j, n_e - 1), 0)),
            pl.BlockSpec((d, te), lambda i, j: (0, jnp.maximum(j - 1, 0))),
            step_rows, step_rows, col, col,
            row, pl.BlockSpec((None, 1, d), lambda i, j: (i // (s // tn), 0, 0)), const, const,
        ],
        out_specs=row,
        scratch_shapes=[pltpu.VMEM((d, tn), F32), pltpu.VMEM((te, tn), F8), pltpu.VMEM((te, tn), F8)],
        compiler_params=_cparams(("parallel", "arbitrary")),
        name="peer_dense",
    )(scales, x8, u_tab, vt_tab, e1, cut, e2, r2, x_res, g2, ln_g.reshape(1, d), ln_b.reshape(1, d))


def rope_tables(seq):
    pos = jnp.arange(seq, dtype=F32)
    inv = ROPE_THETA ** (-jnp.arange(0, HEAD_DIM, 2, dtype=F32) / HEAD_DIM)
    ang = pos[:, None] * inv[None, :]
    sign = jnp.where((jnp.arange(128) % 64) < 32, -1.0, 1.0).astype(F32)
    return jnp.tile(jnp.cos(ang), (1, 4)), jnp.tile(jnp.sin(ang), (1, 4)) * sign[None, :]


def _swa_head_perm():
    heads = [h for p in range(4) for h in (p, 4 + p)]
    return jnp.concatenate([jnp.arange(h * HEAD_DIM, (h + 1) * HEAD_DIM) for h in heads])


def relayout_w_in(w):
    qa, ka, va = w[:, 0:768], w[:, 768:1536], w[:, 1536:2304]
    hy = w[:, 2304:3840]
    qc, kvc = w[:, 3840:4352], w[:, 4352:4608]
    gates = w[:, 4608:7680]
    dil = [jnp.concatenate([t[:, 256 * g:256 * (g + 1)] for t in (qa, ka, va)], axis=1) for g in range(3)]
    main = jnp.concatenate([gates, hy, dil[0], qc[:, _swa_head_perm()], kvc], axis=1).astype(BF16)
    return main, jnp.concatenate(dil[1:], axis=1).astype(BF16)


def residue_major(tab, dil):
    s = tab.shape[0]
    return tab.reshape(s // dil, dil, 128).transpose(1, 0, 2).reshape(s, 128)


def token_mixer(x, ada, w_main, w_dil, conv_w, conv_b, hy_bias, attn_sink, hr, hi, dft_tabs, rope_tabs,
                wa, wb, wc, wo, ln_g, ln_b):
    b, s, _ = x.shape
    sh1, sc1, g1, sh2, sc2, _ = [a[:, None, :] for a in jnp.split(ada, 6, axis=-1)]
    proj = in_proj(x, sc1, sh1, w_main)
    qkv1, qkv2 = in_proj_dil(x, sc1, sh1, w_dil)
    att = (dilated_attention(proj.reshape(b, 1, s, MAIN_WIDTH), COL_DIL0 // QKV_W, *rope_tabs[0])
           + dilated_attention(qkv1, 0, *rope_tabs[1]) + dilated_attention(qkv2, 0, *rope_tabs[2]))
    yb = hyena_mixer(proj, conv_w, conv_b, hy_bias, hr, hi, dft_tabs)
    yc = swa_attention(proj, *rope_tabs[0], attn_sink)
    return merge_mixers(att, yb, yc, proj, x, g1, sc2, sh2, ln_g, ln_b, wa, wb, wc, wo)


def peer_sublayer(u2, wq, keys, u_tab, v_tab, x_res, g2, ln_g, ln_b):
    d = u2.shape[0]
    nh, nk = PEER_HEADS, PEER_KEYS
    wq_t = wq.reshape(d, nh, 2, nk).transpose(2, 1, 3, 0).reshape(2 * nh * nk, d).astype(BF16)
    keys_r = keys.transpose(1, 0, 2, 3).reshape(2 * nh, nk, keys.shape[-1]).astype(BF16)
    tabs = peer_route(u2, wq_t, keys_r)
    su, sv, sx = _pow2_scale(u_tab), _pow2_scale(v_tab), _pow2_scale(u2)
    scales = jnp.stack([1.0 / (su * sx), 1.0 / (sv * WG_SCALE)]).astype(F32)
    x8 = (u2.astype(F32) * sx).astype(F8)
    return peer_dense(scales, x8, (u_tab * su).astype(F8), (v_tab.T * sv).astype(F8), *tabs, x_res, g2, ln_g, ln_b)


def _pow2_scale(a):
    m = jnp.maximum(jnp.max(jnp.abs(a)).astype(F32), 2.0 ** -100)
    return jnp.exp2(jnp.floor(jnp.log2(F8_MAX / m)))


def kernel(x, c, w_ada, b_ada, w_in, conv_w, conv_b, hy_w1, hy_b1, hy_w2, hy_b2, hy_w3, hy_freq, hy_log_decay,
           hy_bias, attn_sink, w_branch_a, w_branch_b, w_branch_c, w_out, ln_g, ln_b, peer_wq, peer_keys, peer_u,
           peer_v):
    b, s, d = x.shape
    depth = w_in.shape[0]
    hw = HYENA_WIDTH
    cos_t, sin_t = rope_tables(s)
    fwd = dft_table(s)
    dft_tabs = dft_tables_r2(s)
    ada = ada_ln(c, w_ada, b_ada)
    w3p = hy_w3.reshape(depth, -1, 2, 2, hw).transpose(0, 1, 3, 2, 4).reshape(depth, -1, 4 * hw)
    ldp = hy_log_decay.reshape(depth, 2, 2, hw).transpose(0, 2, 1, 3).reshape(depth, 4 * hw)
    hr, hi = hyena_filters(hy_w1, hy_b1, hy_w2, hy_b2, w3p, hy_freq, ldp, fwd, s)
    perm = _swa_head_perm()
    rope_tabs = [(residue_major(cos_t, dil), residue_major(sin_t, dil)) for _, dil in DIL_GROUPS]
    for l in range(depth):
        x1, u2 = token_mixer(x, ada[l], *relayout_w_in(w_in[l]), conv_w[l], conv_b[l], hy_bias[l], attn_sink[l],
                             hr[l], hi[l], dft_tabs, rope_tabs, w_branch_a[l].astype(BF16),
                             w_branch_b[l].astype(BF16), w_branch_c[l][perm].astype(BF16), w_out[l].astype(BF16),
                             ln_g[l, 0], ln_b[l, 0])
        x = peer_sublayer(u2, peer_wq[l], peer_keys[l], peer_u[l], peer_v[l], x1, ada[l][:, None, 5 * d:6 * d],
                          ln_g[l, 1], ln_b[l, 1])
    return x
```

```python
import functools
import math

import jax
import jax.numpy as jnp
from jax import lax
from jax.experimental import pallas as pl
from jax.experimental.pallas import tpu as pltpu

F32 = jnp.float32
BF16 = jnp.bfloat16

D_MODEL = 1024
HEAD_DIM = 64
DIL_GROUPS = ((128, 1), (512, 4), (2048, 16))
DIL_RADIUS = 64
HYENA_WIDTH = 512
HYENA_BANDS = 16
SWA_RADIUS = 128
SWA_Q_HEADS = 8
SWA_KV_HEADS = 2
PEER_HEADS = 8
PEER_KEYS = 128
PEER_TOPK = 16
DEPTH = 2
ALPHA = (2 * DEPTH) ** 0.25
LN_EPS = 1e-5
NEG_INF = -1e30
ROPE_THETA = 10000.0

COL_GATES = 0
COL_HY = 3072
COL_DIL0 = 4608
COL_SWA = 5376
MAIN_WIDTH = 6144
QKV_W = 768

VMEM_LIMIT = 56 * 1024 * 1024


def _cparams(sem, vmem=VMEM_LIMIT):
    return pltpu.CompilerParams(dimension_semantics=sem, vmem_limit_bytes=vmem)


def _ada_kernel(c_ref, w_ref, b_ref, o_ref):
    c = c_ref[...]
    cond = (c * jax.nn.sigmoid(c)).astype(BF16)
    o_ref[...] = jnp.dot(cond, w_ref[...].astype(BF16), preferred_element_type=F32) + b_ref[...]


def ada_ln(c, w_ada, b_ada):
    depth, d, n = w_ada.shape
    b = c.shape[0]
    tn = 1536
    return pl.pallas_call(
        _ada_kernel,
        out_shape=jax.ShapeDtypeStruct((depth, b, n), F32),
        grid=(depth, n // tn),
        in_specs=[
            pl.BlockSpec((b, d), lambda l, j: (0, 0)),
            pl.BlockSpec((None, d, tn), lambda l, j: (l, 0, j)),
            pl.BlockSpec((None, 1, tn), lambda l, j: (l, 0, j)),
        ],
        out_specs=pl.BlockSpec((None, b, tn), lambda l, j: (l, 0, j)),
        compiler_params=_cparams(("arbitrary", "arbitrary")),
        name="ada_ln",
    )(c, w_ada, b_ada.reshape(depth, 1, n))


def _inproj_kernel(x_ref, sc_ref, sh_ref, w_ref, o_ref, u_ref):
    @pl.when(pl.program_id(2) == 0)
    def _():
        u_ref[...] = (x_ref[...] * (1.0 + sc_ref[...]) + sh_ref[...]).astype(BF16)

    o_ref[...] = jnp.dot(u_ref[...], w_ref[...], preferred_element_type=F32).astype(o_ref.dtype)


def in_proj(x, sc, sh, w):
    b, s, d = x.shape
    n = w.shape[1]
    tm, tn = 1024, 1536
    return pl.pallas_call(
        _inproj_kernel,
        out_shape=jax.ShapeDtypeStruct((b, s, n), BF16),
        grid=(b, s // tm, n // tn),
        in_specs=[
            pl.BlockSpec((None, tm, d), lambda bi, i, j: (bi, i, 0)),
            pl.BlockSpec((None, 1, d), lambda bi, i, j: (bi, 0, 0)),
            pl.BlockSpec((None, 1, d), lambda bi, i, j: (bi, 0, 0)),
            pl.BlockSpec((d, tn), lambda bi, i, j: (0, j)),
        ],
        out_specs=pl.BlockSpec((None, tm, tn), lambda bi, i, j: (bi, i, j)),
        scratch_shapes=[pltpu.VMEM((tm, d), BF16)],
        compiler_params=_cparams(("parallel", "parallel", "arbitrary")),
        name="in_proj",
    )(x, sc, sh, w)


def _inproj_dil_kernel(x_ref, sc_ref, sh_ref, w_ref, o1_ref, o2_ref, res_s):
    tm = x_ref.shape[0]
    u = (x_ref[...] * (1.0 + sc_ref[...]) + sh_ref[...]).astype(BF16)
    res = jnp.dot(u, w_ref[...], preferred_element_type=F32)
    nslab = QKV_W // 128
    for c in range(2 * nslab):
        res_s[c] = res[:, c * 128:(c + 1) * 128]
    for g, o_ref in ((0, o1_ref), (1, o2_ref)):
        dil = o_ref.shape[0]
        for r in range(dil):
            for c in range(nslab):
                o_ref[r, :, c * 128:(c + 1) * 128] = res_s[g * nslab + c, pl.ds(r, tm // dil, stride=dil), :].astype(
                    o_ref.dtype)


def in_proj_dil(x, sc, sh, w):
    b, s, d = x.shape
    tm = 1024
    d1, d2 = DIL_GROUPS[1][1], DIL_GROUPS[2][1]
    return pl.pallas_call(
        _inproj_dil_kernel,
        out_shape=(jax.ShapeDtypeStruct((b, d1, s // d1, QKV_W), BF16),
                   jax.ShapeDtypeStruct((b, d2, s // d2, QKV_W), BF16)),
        grid=(b, s // tm),
        in_specs=[
            pl.BlockSpec((None, tm, d), lambda bi, i: (bi, i, 0)),
            pl.BlockSpec((None, 1, d), lambda bi, i: (bi, 0, 0)),
            pl.BlockSpec((None, 1, d), lambda bi, i: (bi, 0, 0)),
            pl.BlockSpec((d, 2 * QKV_W), lambda bi, i: (0, 0)),
        ],
        out_specs=(
            pl.BlockSpec((None, d1, tm // d1, QKV_W), lambda bi, i: (bi, 0, i, 0)),
            pl.BlockSpec((None, d2, tm // d2, QKV_W), lambda bi, i: (bi, 0, i, 0)),
        ),
        scratch_shapes=[pltpu.VMEM((2 * QKV_W // 128, tm, 128), F32)],
        compiler_params=_cparams(("parallel", "parallel")),
        name="in_proj_dil",
    )(x, sc, sh, w)


def _rope2(x, cos, sin_signed, lo):
    xr = jnp.where(lo, pltpu.roll(x, 96, 1), pltpu.roll(x, 32, 1))
    return x * cos + xr * sin_signed


def _lane_masks():
    lane = lax.broadcasted_iota(jnp.int32, (1, 128), 1)
    return (lane % 64) < 32, lane < 64


def _band_mask(q0, nq, nk, pad, radius, length):
    qpos = q0 + (lax.broadcasted_iota(jnp.int32, (nq, 1), 0) & 127)
    kpos = q0 - pad + lax.broadcasted_iota(jnp.int32, (1, nk), 1)
    kpos = jnp.where(kpos < 0, -4 * length, jnp.where(kpos >= length, -4 * length, kpos))
    return jnp.abs(qpos - kpos) <= radius


def _dil_attn_kernel(qkv_ref, cos_ref, sin_ref, olo_ref, ohi_ref, llo_ref, lhi_ref, q_s, k_s, v_s, *, dil, ls):
    lo, head0 = _lane_masks()
    heads = (head0, jnp.logical_not(head0))
    cos = cos_ref[...]
    sin = sin_ref[...]
    pad = DIL_RADIUS
    qb = 128
    nblk = ls // qb
    ngroup = 8
    zeros = jnp.zeros((dil, pad, 128), BF16)
    k_s[:, 0:pad, :] = zeros
    k_s[:, pad + ls:pad + ls + pad, :] = zeros
    v_s[:, 0:pad, :] = zeros
    v_s[:, pad + ls:pad + ls + pad, :] = zeros
    for hp, (o_ref, l_ref) in enumerate(((olo_ref, llo_ref), (ohi_ref, lhi_ref))):
        q = qkv_ref[:, :, hp * 128:(hp + 1) * 128].astype(F32).reshape(dil * ls, 128)
        k = qkv_ref[:, :, 256 + hp * 128:256 + (hp + 1) * 128].astype(F32).reshape(dil * ls, 128)
        q_s[...] = (_rope2(q, cos, sin, lo) * (HEAD_DIM ** -0.5)).astype(BF16)
        k_s[:, pad:pad + ls, :] = _rope2(k, cos, sin, lo).astype(BF16).reshape(dil, ls, 128)
        v_s[:, pad:pad + ls, :] = qkv_ref[:, :, 512 + hp * 128:512 + (hp + 1) * 128]

        def grp(gi, carry):
            tiles = []
            for t in range(ngroup):
                c = gi * ngroup + t
                r = c // nblk
                q0 = pl.multiple_of((c % nblk) * qb, qb)
                qblk = q_s[pl.ds(pl.multiple_of(c * qb, qb), qb), :]
                kw = k_s[r, pl.ds(q0, qb + 2 * pad), :]
                vw = v_s[r, pl.ds(q0, qb + 2 * pad), :]
                mask = _band_mask(q0, qb, qb + 2 * pad, pad, DIL_RADIUS, ls)
                ss = [lax.dot_general(jnp.where(hm, qblk, jnp.zeros_like(qblk)), kw, (((1,), (1,)), ((), ())),
                                      preferred_element_type=F32) for hm in heads]
                tiles.append((r, q0, vw, mask, ss))
            soft = []
            for r, q0, vw, mask, ss in tiles:
                for s in ss:
                    s = jnp.where(mask, s, NEG_INF)
                    m = jnp.max(s, axis=-1, keepdims=True)
                    p = jnp.exp(s - m)
                    soft.append((p.astype(BF16), jnp.sum(p, axis=-1, keepdims=True), m))
            for ti, (r, q0, vw, mask, ss) in enumerate(tiles):
                res = []
                for h in range(2):
                    p, den, m = soft[2 * ti + h]
                    res.append((jnp.dot(p, vw, preferred_element_type=F32) / den, m + jnp.log(den)))
                o = jnp.where(head0, res[0][0], res[1][0])
                lse = jnp.where(head0, res[0][1], res[1][1])
                rows = pl.ds(q0, qb) if dil == 1 else pl.ds(q0 * dil + r, qb, stride=dil)
                o_ref[rows, :] = o
                l_ref[rows, :] = lse
            return carry

        lax.fori_loop(0, dil * nblk // ngroup, grp, 0)


def dilated_attention(qkv, col_block, cos_d, sin_d):
    b, dil, ls, _ = qkv.shape
    s = dil * ls
    out_sds = jax.ShapeDtypeStruct((b, s, 128), F32)
    out_spec = pl.BlockSpec((None, s, 128), lambda bi: (bi, 0, 0))
    return pl.pallas_call(
        functools.partial(_dil_attn_kernel, dil=dil, ls=ls),
        out_shape=(out_sds,) * 4,
        grid=(b,),
        in_specs=[
            pl.BlockSpec((None, dil, ls, QKV_W), lambda bi: (bi, 0, 0, col_block)),
            pl.BlockSpec((s, 128), lambda bi: (0, 0)),
            pl.BlockSpec((s, 128), lambda bi: (0, 0)),
        ],
        out_specs=(out_spec,) * 4,
        scratch_shapes=[
            pltpu.VMEM((s, 128), BF16),
            pltpu.VMEM((dil, ls + 2 * DIL_RADIUS, 128), BF16),
            pltpu.VMEM((dil, ls + 2 * DIL_RADIUS, 128), BF16),
        ],
        compiler_params=_cparams(("parallel",)),
        name=f"dil_attn_d{dil}",
    )(qkv, cos_d, sin_d)


def _swa_kernel(qkv_ref, cos_ref, sin_ref, sink_ref, o_ref, q_s, k_s, v_s, *, s_len):
    lo, head0 = _lane_masks()
    heads = (head0, jnp.logical_not(head0))
    cos = cos_ref[...]
    sin = sin_ref[...]
    pad = SWA_RADIUS
    qb = 128
    grp = SWA_Q_HEADS // SWA_KV_HEADS
    zeros = jnp.zeros((pad, 128), BF16)
    k_s[0:pad, :] = zeros
    k_s[pad + s_len:pad + s_len + pad, :] = zeros
    v_s[0:pad, :] = zeros
    v_s[pad + s_len:pad + s_len + pad, :] = zeros
    k = qkv_ref[:, 512:640].astype(F32)
    k_s[pad:pad + s_len, :] = _rope2(k, cos, sin, lo).astype(BF16)
    v_s[pad:pad + s_len, :] = qkv_ref[:, 640:768]
    for qp in range(grp):
        q = qkv_ref[:, qp * 128:(qp + 1) * 128].astype(F32)
        q_s[qp] = (_rope2(q, cos, sin, lo) * (HEAD_DIM ** -0.5)).astype(BF16)
    sinks = [jnp.concatenate([jnp.broadcast_to(sink_ref[kv * grp + p:kv * grp + p + 1, 0:1], (qb, 1))
                              for p in range(grp)], axis=0) for kv in range(SWA_KV_HEADS)]

    def blk(i, carry):
        q0 = pl.multiple_of(i * qb, qb)
        kw = k_s[pl.ds(q0, qb + 2 * pad), :]
        vw = v_s[pl.ds(q0, qb + 2 * pad), :]
        qall = jnp.concatenate([q_s[p, pl.ds(q0, qb), :] for p in range(grp)], axis=0)
        mask = _band_mask(q0, grp * qb, qb + 2 * pad, pad, SWA_RADIUS, s_len)
        ss = [lax.dot_general(jnp.where(hm, qall, jnp.zeros_like(qall)), kw, (((1,), (1,)), ((), ())),
                              preferred_element_type=F32) for hm in heads]
        soft = []
        for s, sk in zip(ss, sinks):
            s = jnp.where(mask, s, NEG_INF)
            m = jnp.maximum(jnp.max(s, axis=-1, keepdims=True), sk)
            p = jnp.exp(s - m)
            soft.append((p.astype(BF16), jnp.sum(p, axis=-1, keepdims=True) + jnp.exp(sk - m)))
        res = [jnp.dot(p, vw, preferred_element_type=F32) / den for p, den in soft]
        o = jnp.where(head0, res[0], res[1]).astype(o_ref.dtype)
        for p in range(grp):
            o_ref[pl.ds(q0, qb), p * 128:(p + 1) * 128] = o[p * qb:(p + 1) * qb, :]
        return carry

    lax.fori_loop(0, s_len // qb, blk, 0, unroll=8)


def swa_attention(proj, cos_t, sin_t, sink):
    b, s, n = proj.shape
    return pl.pallas_call(
        functools.partial(_swa_kernel, s_len=s),
        out_shape=jax.ShapeDtypeStruct((b, s, 512), BF16),
        grid=(b,),
        in_specs=[
            pl.BlockSpec((None, s, QKV_W), lambda bi: (bi, 0, COL_SWA // QKV_W)),
            pl.BlockSpec((s, 128), lambda bi: (0, 0)),
            pl.BlockSpec((s, 128), lambda bi: (0, 0)),
            pl.BlockSpec((8, 128), lambda bi: (0, 0)),
        ],
        out_specs=pl.BlockSpec((None, s, 512), lambda bi: (bi, 0, 0)),
        scratch_shapes=[
            pltpu.VMEM((SWA_Q_HEADS // SWA_KV_HEADS, s, 128), BF16),
            pltpu.VMEM((s + 2 * SWA_RADIUS, 128), BF16),
            pltpu.VMEM((s + 2 * SWA_RADIUS, 128), BF16),
        ],
        compiler_params=_cparams(("parallel",)),
        name="swa_attn",
    )(proj, cos_t, sin_t, jnp.broadcast_to(sink.astype(F32)[:, None], (8, 128)))


HY_CHUNKS = 4


def _hy_filter_kernel(w1_ref, b1_ref, w2_ref, b2_ref, w3_ref, fr_ref, ld_ref, bands_ref, fe_ref, fo_ref,
                      hr_ref, hi_ref, hh_s, slab_s, *, seq):
    hw = HYENA_WIDTH
    hi_p = lax.Precision.HIGHEST

    @pl.when(pl.program_id(2) == 0)
    def _():
        idx = lax.broadcasted_iota(jnp.int32, (seq, 1), 0).astype(F32)
        t = idx / max(seq - 1, 1)
        w = 2.0 * math.pi * idx / seq
        ang = w * bands_ref[...]
        w1 = w1_ref[...]
        pre = (t * w1[0:1, :]
               + jnp.dot(jnp.cos(ang), w1[1:1 + HYENA_BANDS, :], precision=hi_p, preferred_element_type=F32)
               - jnp.dot(jnp.sin(ang), w1[1 + HYENA_BANDS:1 + 2 * HYENA_BANDS, :], precision=hi_p,
                         preferred_element_type=F32)
               + b1_ref[...])
        h = jnp.sin(fr_ref[0:1, :] * pre)
        h = jnp.sin(fr_ref[1:2, :] * (jnp.dot(h, w2_ref[...], precision=hi_p, preferred_element_type=F32)
                                      + b2_ref[...]))
        h = jnp.dot(h, w3_ref[...], precision=hi_p, preferred_element_type=F32)
        h = h * jnp.exp(-t * jnp.exp(ld_ref[...]))
        hf = h[:, :hw]
        hb = jnp.where(idx > 0.0, h[:, hw:], 0.0)
        inv = lax.rsqrt(jnp.sum(hf * hf + hb * hb, axis=0, keepdims=True) + 1e-12)
        for q, comb in enumerate(((hf + hb) * inv, (hf - hb) * inv)):
            for sl in range(hw // 128):
                slab_s[sl] = comb[:, sl * 128:(sl + 1) * 128]
            for par in range(2):
                hh_s[2 * q + par] = jnp.concatenate(
                    [slab_s[sl, pl.ds(par, seq // 2, stride=2), :] for sl in range(hw // 128)], axis=-1).astype(BF16)

    kc = seq // 2 // HY_CHUNKS
    ec = jnp.dot(fe_ref[0:kc, :], hh_s[0], preferred_element_type=F32)
    oc = jnp.dot(fo_ref[0:kc, :], hh_s[1], preferred_element_type=F32)
    es = jnp.dot(fe_ref[kc:2 * kc, :], hh_s[2], preferred_element_type=F32)
    os_ = jnp.dot(fo_ref[kc:2 * kc, :], hh_s[3], preferred_element_type=F32)
    hr_ref[0] = ec + oc
    hr_ref[1] = ec - oc
    hi_ref[0] = -(es + os_)
    hi_ref[1] = es - os_


def hyena_filters(w1, b1, w2, b2, w3p, freq, ldp, fe, fo, seq):
    depth = w1.shape[0]
    half = seq // 2
    kc = half // HY_CHUNKS
    hw = HYENA_WIDTH
    bands = jnp.linspace(1e-4, HYENA_BANDS - 1, HYENA_BANDS, dtype=F32).reshape(1, HYENA_BANDS)
    full = lambda *shape: pl.BlockSpec((None,) + shape, lambda l, o, c: (l,) + (0,) * len(shape))
    out_sds = jax.ShapeDtypeStruct((depth, 2, 2, half, hw), F32)
    out_spec = pl.BlockSpec((None, None, 2, kc, hw), lambda l, o, c: (l, o, 0, c, 0))
    tab_spec = pl.BlockSpec((None, 2 * kc, half), lambda l, o, c: (c, 0, 0))
    hr, hi = pl.pallas_call(
        functools.partial(_hy_filter_kernel, seq=seq),
        out_shape=(out_sds, out_sds),
        grid=(depth, 2, HY_CHUNKS),
        in_specs=[
            full(*w1.shape[1:]), full(1, b1.shape[-1]), full(*w2.shape[1:]), full(1, b2.shape[-1]),
            pl.BlockSpec((None, w3p.shape[1], 2 * hw), lambda l, o, c: (l, 0, o)),
            full(*freq.shape[1:]),
            pl.BlockSpec((None, 1, 2 * hw), lambda l, o, c: (l, 0, o)),
            pl.BlockSpec((1, HYENA_BANDS), lambda l, o, c: (0, 0)),
            tab_spec, tab_spec,
        ],
        out_specs=(out_spec, out_spec),
        scratch_shapes=[pltpu.VMEM((4, half, hw), BF16), pltpu.VMEM((hw // 128, seq, 128), F32)],
        compiler_params=_cparams(("arbitrary", "arbitrary", "arbitrary")),
        name="hyena_filters",
    )(w1, b1[:, None, :], w2, b2[:, None, :], w3p, freq, ldp[:, None, :], bands, fe, fo)
    return hr.reshape(depth, 2, seq, hw), hi.reshape(depth, 2, seq, hw)


def dft_tables_r2(seq):
    half = seq // 2
    kc = half // HY_CHUNKS
    k = jnp.arange(half, dtype=jnp.int32)[:, None]
    n = jnp.arange(half, dtype=jnp.int32)[None, :]
    out = []
    for off in (0, 1):
        ang = (((2 * k + 1) * (2 * n + off)) % (4 * seq)).astype(F32) * (2.0 * math.pi / (4 * seq))
        c = jnp.cos(ang).reshape(HY_CHUNKS, kc, half)
        s = jnp.sin(ang).reshape(HY_CHUNKS, kc, half)
        out.append(jnp.concatenate([c, s], axis=1).astype(BF16))
        out.append((jnp.concatenate([c, -s], axis=1) * (1.0 / seq)).transpose(0, 2, 1).astype(BF16))
    return out[0], out[2], out[1], out[3]


def _hyena_kernel(hy_ref, cw_ref, cb_ref, bias_ref, hr_ref, hi_ref, hrp_ref, hip_ref, fe_ref, fo_ref, ge_ref, go_ref,
                  o_ref, zb_s, zf_s, acc_s, slab_s, *, seq):
    hw = HYENA_WIDTH
    half = seq // 2
    o = pl.program_id(1)
    c = pl.program_id(2)
    kc = half // HY_CHUNKS
    nslab = hw // 128

    def short_conv(part):
        x = hy_ref[:, part * hw:(part + 1) * hw].astype(F32)
        row = lax.broadcasted_iota(jnp.int32, (seq, 1), 0)
        xm = jnp.where(row == 0, 0.0, pltpu.roll(x, 1, 0))
        xp = jnp.where(row == seq - 1, 0.0, pltpu.roll(x, seq - 1, 0))
        w = cw_ref[:, part * hw:(part + 1) * hw]
        return cb_ref[:, part * hw:(part + 1) * hw] + xm * w[0:1, :] + x * w[1:2, :] + xp * w[2:3, :]

    def split(x):
        for sl in range(nslab):
            slab_s[sl] = x[:, sl * 128:(sl + 1) * 128]
        return tuple(jnp.concatenate([slab_s[sl, pl.ds(par, half, stride=2), :] for sl in range(nslab)], axis=-1)
                     for par in (0, 1))

    def interleave(xe, xo):
        for sl in range(nslab):
            slab_s[sl, pl.ds(0, half, stride=2), :] = xe[:, sl * 128:(sl + 1) * 128]
            slab_s[sl, pl.ds(1, half, stride=2), :] = xo[:, sl * 128:(sl + 1) * 128]
        return jnp.concatenate([slab_s[sl] for sl in range(nslab)], axis=-1)

    @pl.when((o == 0) & (c == 0))
    def _():
        for par, z in enumerate(split(short_conv(0))):
            zf_s[par] = z
            zb_s[par] = z.astype(BF16)

    ze = jnp.dot(fe_ref[...], zb_s[0], preferred_element_type=F32)
    zo = jnp.dot(fo_ref[...], zb_s[1], preferred_element_type=F32)
    zc, zs = ze[:kc] + zo[:kc], ze[kc:] + zo[kc:]
    zcp, zsp = ze[:kc] - zo[:kc], zo[kc:] - ze[kc:]
    hr, hi, hrp, hip = hr_ref[...], hi_ref[...], hrp_ref[...], hip_ref[...]
    yr, yi = zc * hr + zs * hi, zc * hi - zs * hr
    yrp, yip = zcp * hrp + zsp * hip, zcp * hip - zsp * hrp
    pe = jnp.dot(ge_ref[...], jnp.concatenate([yr + yrp, yi - yip], axis=0).astype(BF16),
                 preferred_element_type=F32)
    po = jnp.dot(go_ref[...], jnp.concatenate([yr - yrp, yi + yip], axis=0).astype(BF16),
                 preferred_element_type=F32)

    @pl.when(c == 0)
    def _():
        acc_s[0] = pe
        acc_s[1] = po

    @pl.when(c > 0)
    def _():
        acc_s[0] += pe
        acc_s[1] += po

    @pl.when((c == HY_CHUNKS - 1) & (o == 0))
    def _():
        for par, p1 in enumerate(split(short_conv(1))):
            z1 = p1 * (acc_s[par] + bias_ref[0:1, :] * zf_s[par])
            zf_s[par] = z1
            zb_s[par] = z1.astype(BF16)

    @pl.when((c == HY_CHUNKS - 1) & (o == 1))
    def _():
        halves = [p2 * (acc_s[par] + bias_ref[1:2, :] * zf_s[par]) for par, p2 in enumerate(split(short_conv(2)))]
        o_ref[...] = interleave(*halves).astype(o_ref.dtype)


def hyena_mixer(proj, conv_w, conv_b, hy_bias, hr, hi, tabs):
    b, s, n = proj.shape
    hw = HYENA_WIDTH
    half = s // 2
    kc = half // HY_CHUNKS
    nck = HY_CHUNKS
    spec = pl.BlockSpec((None, kc, hw), lambda bi, o, c: (o, c, 0))
    spec_p = pl.BlockSpec((None, kc, hw), lambda bi, o, c: (o, nck + c, 0))
    fwd_spec = pl.BlockSpec((None, 2 * kc, half), lambda bi, o, c: (c, 0, 0))
    inv_spec = pl.BlockSpec((None, half, 2 * kc), lambda bi, o, c: (c, 0, 0))
    fe, fo, ge, go = tabs
    return pl.pallas_call(
        functools.partial(_hyena_kernel, seq=s),
        out_shape=jax.ShapeDtypeStruct((b, s, hw), BF16),
        grid=(b, 2, nck),
        in_specs=[
            pl.BlockSpec((None, s, 3 * hw), lambda bi, o, c: (bi, 0, COL_HY // (3 * hw))),
            pl.BlockSpec((3, 3 * hw), lambda bi, o, c: (0, 0)),
            pl.BlockSpec((1, 3 * hw), lambda bi, o, c: (0, 0)),
            pl.BlockSpec((2, hw), lambda bi, o, c: (0, 0)),
            spec, spec, spec_p, spec_p, fwd_spec, fwd_spec, inv_spec, inv_spec,
        ],
        out_specs=pl.BlockSpec((None, s, hw), lambda bi, o, c: (bi, 0, 0)),
        scratch_shapes=[pltpu.VMEM((2, half, hw), BF16), pltpu.VMEM((2, half, hw), F32),
                        pltpu.VMEM((2, half, hw), F32), pltpu.VMEM((hw // 128, s, 128), F32)],
        compiler_params=_cparams(("parallel", "arbitrary", "arbitrary")),
        name="hyena_conv",
    )(proj, conv_w, conv_b.reshape(1, -1), hy_bias, hr, hi, hr, hi, fe, fo, ge, go)


def _layer_norm(y, g, b):
    mu = jnp.mean(y, axis=-1, keepdims=True)
    yc = y - mu
    var = jnp.mean(yc * yc, axis=-1, keepdims=True)
    return yc * lax.rsqrt(var + LN_EPS) * g + b


def _merge_kernel(*refs):
    att = refs[:12]
    (yb_ref, yc_ref, gl_ref, x_ref, g1_ref, sc2_ref, sh2_ref, lng_ref, lnb_ref, wa_ref, wb_ref, wc_ref, wo_ref,
     xo_ref, u2_ref) = refs[12:]
    d = D_MODEL
    halves = []
    for half in range(2):
        la, lb, lc = (att[4 * g + 2 + half][...] for g in range(3))
        m = jnp.maximum(jnp.maximum(la, lb), lc)
        ea, eb, ec = jnp.exp(la - m), jnp.exp(lb - m), jnp.exp(lc - m)
        inv = 1.0 / (ea + eb + ec)
        halves.append((ea * inv) * att[half][...] + (eb * inv) * att[4 + half][...] + (ec * inv) * att[8 + half][...])
    ya = jnp.concatenate(halves, axis=-1)
    za = jnp.dot(ya.astype(BF16), wa_ref[...], preferred_element_type=F32)
    zb = jnp.dot(yb_ref[...], wb_ref[...], preferred_element_type=F32)
    zc = jnp.dot(yc_ref[...], wc_ref[...], preferred_element_type=F32)
    merged = (jax.nn.sigmoid(gl_ref[:, 0:d].astype(F32)) * za
              + jax.nn.sigmoid(gl_ref[:, d:2 * d].astype(F32)) * zb
              + jax.nn.sigmoid(gl_ref[:, 2 * d:3 * d].astype(F32)) * zc)
    mix = jnp.dot(merged.astype(BF16), wo_ref[...], preferred_element_type=F32)
    xn = _layer_norm(ALPHA * x_ref[...] + g1_ref[...] * mix, lng_ref[...], lnb_ref[...])
    xo_ref[...] = xn
    u2_ref[...] = (xn * (1.0 + sc2_ref[...]) + sh2_ref[...]).T.astype(BF16)


def merge_mixers(att, yb, yc, proj, x, g1, sc2, sh2, ln_g, ln_b, wa, wb, wc, wo):
    b, s, d = x.shape
    tm = 512
    row = lambda w: pl.BlockSpec((None, tm, w), lambda bi, i: (bi, i, 0))
    per_b = pl.BlockSpec((None, 1, d), lambda bi, i: (bi, 0, 0))
    const = lambda r, c: pl.BlockSpec((r, c), lambda bi, i: (0, 0))
    return pl.pallas_call(
        _merge_kernel,
        out_shape=(jax.ShapeDtypeStruct((b, s, d), F32), jax.ShapeDtypeStruct((d, b * s), BF16)),
        grid=(b, s // tm),
        in_specs=[row(128)] * 12 + [row(512), row(512), row(3 * d), row(d), per_b, per_b, per_b,
                                    const(1, d), const(1, d), const(256, d), const(512, d), const(512, d),
                                    const(d, d)],
        out_specs=(row(d), pl.BlockSpec((d, tm), lambda bi, i: (0, bi * (s // tm) + i))),
        compiler_params=_cparams(("parallel", "parallel")),
        name="merge_mixers",
    )(*att, yb, yc, proj, x, g1, sc2, sh2, ln_g.reshape(1, d), ln_b.reshape(1, d), wa, wb, wc, wo)


_PEER_PAIRS = [(i, j) for i in range(1, PEER_TOPK + 1) for j in range(1, PEER_TOPK + 1) if i * j <= PEER_TOPK]
PEER_NOT_TOP = 100.0


def _bitonic_clean_desc(v):
    v = list(v)
    dist = len(v) // 2
    while dist >= 1:
        for a in range(len(v)):
            b = a ^ dist
            if b > a:
                v[a], v[b] = jnp.maximum(v[a], v[b]), jnp.minimum(v[a], v[b])
        dist //= 2
    return v


def _bitonic_sort_desc(v):
    v = list(v)
    k = 2
    while k <= len(v):
        dist = k // 2
        while dist >= 1:
            for a in range(len(v)):
                b = a ^ dist
                if b > a:
                    hi, lo = jnp.maximum(v[a], v[b]), jnp.minimum(v[a], v[b])
                    v[a], v[b] = (hi, lo) if (a & k) == 0 else (lo, hi)
            dist //= 2
        k *= 2
    return v


def _merge_top(v, w):
    n = len(v)
    return _bitonic_clean_desc([jnp.maximum(v[a], w[n - 1 - a]) for a in range(n)])


def _sorted_prefix_count(vals, pred):
    w = jnp.where
    t8 = pred(vals[7])
    t4 = pred(w(t8, vals[11], vals[3]))
    t2 = pred(w(t8, w(t4, vals[13], vals[9]), w(t4, vals[5], vals[1])))
    lo = w(t4, w(t2, vals[6], vals[4]), w(t2, vals[2], vals[0]))
    hi = w(t4, w(t2, vals[14], vals[12]), w(t2, vals[10], vals[8]))
    t1 = pred(w(t8, hi, lo))
    t16 = pred(vals[15])
    count = w(t8, 8.0, 0.0) + w(t4, 4.0, 0.0) + w(t2, 2.0, 0.0) + w(t1, 1.0, 0.0) + w(t16, 1.0, 0.0)
    return count, t16


def _peer_route_kernel(u_ref, wq_ref, keys_ref, e1_ref, cut_ref, e2_ref, r2_ref, q_s, s_s, ab_s, st_s, *, tr):
    nh, nk, topk = PEER_HEADS, PEER_KEYS, PEER_TOPK
    qt = jnp.dot(wq_ref[...], u_ref[...], preferred_element_type=F32)
    q_s[...] = qt.astype(BF16)
    for ph in range(2 * nh):
        s_s[ph] = jnp.dot(keys_ref[ph], q_s[ph * 128:(ph + 1) * 128, :], preferred_element_type=F32)

    def extract(h, carry):
        for p in range(2):
            sv = s_s[p * nh + h]
            v = _bitonic_sort_desc([sv[g * 8:(g + 1) * 8, :] for g in range(topk)])
            for shift in (4, 2, 1):
                v = _merge_top(v, [pltpu.roll(x, shift, 0) for x in v])
            for r in range(topk):
                ab_s[p, r, pl.ds(h, 1), :] = v[r][0:1, :]
        return carry

    lax.fori_loop(0, nh, extract, 0)

    for ch in range(tr // 128):
        ln = slice(ch * 128, (ch + 1) * 128)
        a = [ab_s[0, r, :, ln] for r in range(topk)]
        b = [ab_s[1, r, :, ln] for r in range(topk)]
        cand = [a[i - 1] + b[j - 1] for (i, j) in _PEER_PAIRS]
        rest = cand[topk:] + [jnp.full((nh, 128), -jnp.inf, F32)] * (4 * topk - len(cand))
        groups = [cand[:topk]] + [_bitonic_sort_desc(rest[g * topk:(g + 1) * topk]) for g in range(3)]
        m01 = _merge_top(groups[0], groups[1])
        m23 = _merge_top(groups[2], groups[3])
        tau = functools.reduce(jnp.minimum, [jnp.maximum(m01[r], m23[topk - 1 - r]) for r in range(topk)])
        top = cand[0]
        z = jnp.zeros((nh, 128), F32)
        for cx in cand:
            z = z + jnp.where(cx >= tau, jnp.exp(cx - top), 0.0)
        st_s[0, :, ln] = tau
        st_s[1, :, ln] = 1.0 / z

    for h in range(nh):
        rows = slice(h * nk, (h + 1) * nk)
        s1 = s_s[h]
        s2 = s_s[nh + h]
        hrow = slice(h, h + 1)
        tau = st_s[0, hrow, :]
        in1 = s1 >= ab_s[0, topk - 1, hrow, :]
        bs = [ab_s[1, r, hrow, :] for r in range(topk)]
        cut, _ = _sorted_prefix_count(bs, lambda v: s1 + v >= tau)
        above, below_all = _sorted_prefix_count(bs, lambda v: v > s2)
        e1_ref[rows, :] = jnp.where(in1, jnp.exp(s1 - ab_s[0, 0, hrow, :]), 0.0)
        cut_ref[rows, :] = jnp.where(in1, cut, 0.0)
        e2_ref[rows, :] = jnp.where(below_all, 0.0, jnp.exp(s2 - bs[0]) * st_s[1, hrow, :]).astype(BF16)
        r2_ref[rows, :] = jnp.where(below_all, PEER_NOT_TOP, above + 1.0).astype(BF16)


def peer_route(u2, wq_t, keys):
    d, t = u2.shape
    tr = 512
    nrow = PEER_HEADS * PEER_KEYS
    tab = lambda dt: jax.ShapeDtypeStruct((nrow, t), dt)
    col = pl.BlockSpec((nrow, tr), lambda i: (0, i))
    return pl.pallas_call(
        functools.partial(_peer_route_kernel, tr=tr),
        out_shape=(tab(F32), tab(F32), tab(BF16), tab(BF16)),
        grid=(t // tr,),
        in_specs=[
            pl.BlockSpec((d, tr), lambda i: (0, i)),
            pl.BlockSpec(wq_t.shape, lambda i: (0, 0)),
            pl.BlockSpec(keys.shape, lambda i: (0, 0, 0)),
        ],
        out_specs=(col, col, col, col),
        scratch_shapes=[
            pltpu.VMEM((2 * nrow, tr), BF16),
            pltpu.VMEM((2 * PEER_HEADS, PEER_KEYS, tr), F32),
            pltpu.VMEM((2, PEER_TOPK, PEER_HEADS, tr), F32),
            pltpu.VMEM((2, PEER_HEADS, tr), F32),
        ],
        compiler_params=_cparams(("parallel",)),
        name="peer_route",
    )(u2, wq_t, keys)


F8 = jnp.float8_e4m3fn
F8_MAX = 448.0
WG_SCALE = 16.0


def _peer_dense_kernel(sc_ref, x_ref, u_ref, vt_ref, e1_ref, cut_ref, e2_ref, r2_ref, xres_ref, g2_ref, lng_ref,
                       lnb_ref, o_ref, acc_s, wga_s, wgb_s, *, te, n_e):
    j = pl.program_id(1)
    nk = PEER_KEYS
    tn = x_ref.shape[1]
    sub = 16

    @pl.when(j == 0)
    def _():
        acc_s[...] = jnp.zeros_like(acc_s)
        wgb_s[...] = jnp.zeros_like(wgb_s)

    def step(w_cur, w_prev):
        inv_a = sc_ref[0]
        ts = 512
        c_lin = jnp.full((1, ts), 0.5 * WG_SCALE * inv_a, F32).astype(BF16)
        c_erf = jnp.full((1, ts), (2.0 ** -0.5) * inv_a, F32).astype(BF16)

        def row_bcast(ref, h, ii, lanes):
            chunks = [jnp.broadcast_to(ref[h, ii:ii + 1, c0:c0 + 128], (sub, 128)).astype(BF16)
                      for c0 in range(lanes.start, lanes.stop, 128)]
            return jnp.concatenate(chunks, axis=-1)[None]

        for sl in range(tn // ts):
            lanes = slice(sl * ts, (sl + 1) * ts)
            for ii in range(te // nk):
                rows = slice(ii * nk, (ii + 1) * nk)
                at = jnp.dot(u_ref[rows, :], x_ref[:, lanes], preferred_element_type=F32).astype(BF16)
                if ii == 1:
                    acc_s[:, lanes] += jnp.dot(vt_ref[...], w_prev[:, lanes], preferred_element_type=F32)
                g = (c_lin * at) * (1.0 + lax.erf(at * c_erf))
                w = None
                for h in range(PEER_HEADS):
                    cb = row_bcast(cut_ref, h, ii, lanes)
                    e1 = row_bcast(e1_ref, h, ii, lanes)
                    r2 = r2_ref[h * nk:(h + 1) * nk, lanes].reshape(nk // sub, sub, ts)
                    e2 = e2_ref[h * nk:(h + 1) * nk, lanes].reshape(nk // sub, sub, ts)
                    sel = jnp.where(r2 <= cb, e2, jnp.zeros((), BF16)) * e1
                    w = sel if w is None else w + sel
                wg = jnp.clip(g * w.reshape(nk, ts), -F8_MAX, F8_MAX)
                w_cur[rows, lanes] = wg.astype(F8)

    @pl.when((j % 2 == 0) & (j < n_e))
    def _():
        step(wga_s, wgb_s)

    @pl.when(j % 2 == 1)
    def _():
        step(wgb_s, wga_s)

    @pl.when(j == n_e)
    def _():
        acc = acc_s[...] + jnp.dot(vt_ref[...], wgb_s[...], preferred_element_type=F32)
        ffn = (acc * sc_ref[1]).T
        o_ref[...] = _layer_norm(ALPHA * xres_ref[...] + g2_ref[...] * ffn, lng_ref[...], lnb_ref[...])


def peer_dense(scales, x8, u_tab, vt_tab, e1, cut, e2, r2, x_res, g2, ln_g, ln_b):
    d, t = x8.shape
    b, s, _ = x_res.shape
    ne = u_tab.shape[0]
    tn, te = 1024, 1024
    n_e = ne // te
    nrow = PEER_HEADS * PEER_KEYS
    col = pl.BlockSpec((nrow, tn), lambda i, j: (0, i))
    step_rows = pl.BlockSpec((PEER_HEADS, te // PEER_KEYS, tn), lambda i, j: (0, jnp.minimum(j, n_e - 1), i))
    e1 = e1.reshape(PEER_HEADS, PEER_KEYS, t)
    cut = cut.reshape(PEER_HEADS, PEER_KEYS, t)
    row = pl.BlockSpec((None, tn, d), lambda i, j: (i // (s // tn), i % (s // tn), 0))
    const = pl.BlockSpec((1, d), lambda i, j: (0, 0))
    return pl.pallas_call(
        functools.partial(_peer_dense_kernel, te=te, n_e=n_e),
        out_shape=jax.ShapeDtypeStruct((b, s, d), F32),
        grid=(t // tn, n_e + 1),
        in_specs=[
            pl.BlockSpec(memory_space=pltpu.SMEM),
            pl.BlockSpec((d, tn), lambda i, j: (0, i)),
            pl.BlockSpec((te, d), lambda i, j: (jnp.minimum(j, n_e - 1), 0)),
            pl.BlockSpec((d, te), lambda i, j: (0, jnp.maximum(j - 1, 0))),
            step_rows, step_rows, col, col,
            row, pl.BlockSpec((None, 1, d), lambda i, j: (i // (s // tn), 0, 0)), const, const,
        ],
        out_specs=row,
        scratch_shapes=[pltpu.VMEM((d, tn), F32), pltpu.VMEM((te, tn), F8), pltpu.VMEM((te, tn), F8)],
        compiler_params=_cparams(("parallel", "arbitrary")),
        name="peer_dense",
    )(scales, x8, u_tab, vt_tab, e1, cut, e2, r2, x_res, g2, ln_g.reshape(1, d), ln_b.reshape(1, d))


def rope_tables(seq):
    pos = jnp.arange(seq, dtype=F32)
    inv = ROPE_THETA ** (-jnp.arange(0, HEAD_DIM, 2, dtype=F32) / HEAD_DIM)
    ang = pos[:, None] * inv[None, :]
    sign = jnp.where((jnp.arange(128) % 64) < 32, -1.0, 1.0).astype(F32)
    return jnp.tile(jnp.cos(ang), (1, 4)), jnp.tile(jnp.sin(ang), (1, 4)) * sign[None, :]


def _swa_head_perm():
    heads = [h for p in range(4) for h in (p, 4 + p)]
    return jnp.concatenate([jnp.arange(h * HEAD_DIM, (h + 1) * HEAD_DIM) for h in heads])


def relayout_w_in(w):
    qa, ka, va = w[:, 0:768], w[:, 768:1536], w[:, 1536:2304]
    hy = w[:, 2304:3840]
    qc, kvc = w[:, 3840:4352], w[:, 4352:4608]
    gates = w[:, 4608:7680]
    dil = [jnp.concatenate([t[:, 256 * g:256 * (g + 1)] for t in (qa, ka, va)], axis=1) for g in range(3)]
    main = jnp.concatenate([gates, hy, dil[0], qc[:, _swa_head_perm()], kvc], axis=1).astype(BF16)
    return main, jnp.concatenate(dil[1:], axis=1).astype(BF16)


def residue_major(tab, dil):
    s = tab.shape[0]
    return tab.reshape(s // dil, dil, 128).transpose(1, 0, 2).reshape(s, 128)


def token_mixer(x, ada, w_main, w_dil, conv_w, conv_b, hy_bias, attn_sink, hr, hi, dft_tabs, rope_tabs,
                wa, wb, wc, wo, ln_g, ln_b):
    b, s, _ = x.shape
    sh1, sc1, g1, sh2, sc2, _ = [a[:, None, :] for a in jnp.split(ada, 6, axis=-1)]
    proj = in_proj(x, sc1, sh1, w_main)
    qkv1, qkv2 = in_proj_dil(x, sc1, sh1, w_dil)
    att = (dilated_attention(proj.reshape(b, 1, s, MAIN_WIDTH), COL_DIL0 // QKV_W, *rope_tabs[0])
           + dilated_attention(qkv1, 0, *rope_tabs[1]) + dilated_attention(qkv2, 0, *rope_tabs[2]))
    yb = hyena_mixer(proj, conv_w, conv_b, hy_bias, hr, hi, dft_tabs)
    yc = swa_attention(proj, *rope_tabs[0], attn_sink)
    return merge_mixers(att, yb, yc, proj, x, g1, sc2, sh2, ln_g, ln_b, wa, wb, wc, wo)


def peer_sublayer(u2, wq, keys, u_tab, v_tab, x_res, g2, ln_g, ln_b):
    d = u2.shape[0]
    nh, nk = PEER_HEADS, PEER_KEYS
    wq_t = wq.reshape(d, nh, 2, nk).transpose(2, 1, 3, 0).reshape(2 * nh * nk, d).astype(BF16)
    keys_r = keys.transpose(1, 0, 2, 3).reshape(2 * nh, nk, keys.shape[-1]).astype(BF16)
    tabs = peer_route(u2, wq_t, keys_r)
    su, sv, sx = _pow2_scale(u_tab), _pow2_scale(v_tab), _pow2_scale(u2)
    scales = jnp.stack([1.0 / (su * sx), 1.0 / (sv * WG_SCALE)]).astype(F32)
    x8 = (u2.astype(F32) * sx).astype(F8)
    return peer_dense(scales, x8, (u_tab * su).astype(F8), (v_tab.T * sv).astype(F8), *tabs, x_res, g2, ln_g, ln_b)


def _pow2_scale(a):
    m = jnp.maximum(jnp.max(jnp.abs(a)).astype(F32), 2.0 ** -100)
    return jnp.exp2(jnp.floor(jnp.log2(F8_MAX / m)))


def kernel(x, c, w_ada, b_ada, w_in, conv_w, conv_b, hy_w1, hy_b1, hy_w2, hy_b2, hy_w3, hy_freq, hy_log_decay,
           hy_bias, attn_sink, w_branch_a, w_branch_b, w_branch_c, w_out, ln_g, ln_b, peer_wq, peer_keys, peer_u,
           peer_v):
    b, s, d = x.shape
    depth = w_in.shape[0]
    hw = HYENA_WIDTH
    cos_t, sin_t = rope_tables(s)
    dft_tabs = dft_tables_r2(s)
    ada = ada_ln(c, w_ada, b_ada)
    w3p = hy_w3.reshape(depth, -1, 2, 2, hw).transpose(0, 1, 3, 2, 4).reshape(depth, -1, 4 * hw)
    ldp = hy_log_decay.reshape(depth, 2, 2, hw).transpose(0, 2, 1, 3).reshape(depth, 4 * hw)
    hr, hi = hyena_filters(hy_w1, hy_b1, hy_w2, hy_b2, w3p, hy_freq, ldp, dft_tabs[0], dft_tabs[1], s)
    perm = _swa_head_perm()
    rope_tabs = [(residue_major(cos_t, dil), residue_major(sin_t, dil)) for _, dil in DIL_GROUPS]
    for l in range(depth):
        x1, u2 = token_mixer(x, ada[l], *relayout_w_in(w_in[l]), conv_w[l], conv_b[l], hy_bias[l], attn_sink[l],
                             hr[l], hi[l], dft_tabs, rope_tabs, w_branch_a[l].astype(BF16),
                             w_branch_b[l].astype(BF16), w_branch_c[l][perm].astype(BF16), w_out[l].astype(BF16),
                             ln_g[l, 0], ln_b[l, 0])
        x = peer_sublayer(u2, peer_wq[l], peer_keys[l], peer_u[l], peer_v[l], x1, ada[l][:, None, 5 * d:6 * d],
                          ln_g[l, 1], ln_b[l, 1])
    return x
```

```python
import functools
import math

import jax
import jax.numpy as jnp
from jax import lax
from jax.experimental import pallas as pl
from jax.experimental.pallas import tpu as pltpu

F32 = jnp.float32
BF16 = jnp.bfloat16

D_MODEL = 1024
HEAD_DIM = 64
DIL_GROUPS = ((128, 1), (512, 4), (2048, 16))
DIL_RADIUS = 64
HYENA_WIDTH = 512
HYENA_BANDS = 16
SWA_RADIUS = 128
SWA_Q_HEADS = 8
SWA_KV_HEADS = 2
PEER_HEADS = 8
PEER_KEYS = 128
PEER_TOPK = 16
DEPTH = 2
ALPHA = (2 * DEPTH) ** 0.25
LN_EPS = 1e-5
NEG_INF = -1e30
ROPE_THETA = 10000.0

COL_GATES = 0
COL_HY = 3072
COL_DIL0 = 4608
COL_SWA = 5376
MAIN_WIDTH = 6144
QKV_W = 768

VMEM_LIMIT = 56 * 1024 * 1024


def _cparams(sem, vmem=VMEM_LIMIT):
    return pltpu.CompilerParams(dimension_semantics=sem, vmem_limit_bytes=vmem)


def _ada_kernel(c_ref, w_ref, b_ref, o_ref):
    c = c_ref[...]
    cond = (c * jax.nn.sigmoid(c)).astype(BF16)
    o_ref[...] = jnp.dot(cond, w_ref[...].astype(BF16), preferred_element_type=F32) + b_ref[...]


def ada_ln(c, w_ada, b_ada):
    depth, d, n = w_ada.shape
    b = c.shape[0]
    tn = 1536
    return pl.pallas_call(
        _ada_kernel,
        out_shape=jax.ShapeDtypeStruct((depth, b, n), F32),
        grid=(depth, n // tn),
        in_specs=[
            pl.BlockSpec((b, d), lambda l, j: (0, 0)),
            pl.BlockSpec((None, d, tn), lambda l, j: (l, 0, j)),
            pl.BlockSpec((None, 1, tn), lambda l, j: (l, 0, j)),
        ],
        out_specs=pl.BlockSpec((None, b, tn), lambda l, j: (l, 0, j)),
        compiler_params=_cparams(("arbitrary", "arbitrary")),
        name="ada_ln",
    )(c, w_ada, b_ada.reshape(depth, 1, n))


def _inproj_kernel(x_ref, sc_ref, sh_ref, w_ref, o_ref, u_ref):
    @pl.when(pl.program_id(2) == 0)
    def _():
        u_ref[...] = (x_ref[...] * (1.0 + sc_ref[...]) + sh_ref[...]).astype(BF16)

    o_ref[...] = jnp.dot(u_ref[...], w_ref[...], preferred_element_type=F32).astype(o_ref.dtype)


def in_proj(x, sc, sh, w):
    b, s, d = x.shape
    n = w.shape[1]
    tm, tn = 1024, 1536
    return pl.pallas_call(
        _inproj_kernel,
        out_shape=jax.ShapeDtypeStruct((b, s, n), BF16),
        grid=(b, s // tm, n // tn),
        in_specs=[
            pl.BlockSpec((None, tm, d), lambda bi, i, j: (bi, i, 0)),
            pl.BlockSpec((None, 1, d), lambda bi, i, j: (bi, 0, 0)),
            pl.BlockSpec((None, 1, d), lambda bi, i, j: (bi, 0, 0)),
            pl.BlockSpec((d, tn), lambda bi, i, j: (0, j)),
        ],
        out_specs=pl.BlockSpec((None, tm, tn), lambda bi, i, j: (bi, i, j)),
        scratch_shapes=[pltpu.VMEM((tm, d), BF16)],
        compiler_params=_cparams(("parallel", "parallel", "arbitrary")),
        name="in_proj",
    )(x, sc, sh, w)


def _inproj_dil_kernel(x_ref, sc_ref, sh_ref, w_ref, o1_ref, o2_ref, res_s):
    tm = x_ref.shape[0]
    u = (x_ref[...] * (1.0 + sc_ref[...]) + sh_ref[...]).astype(BF16)
    res = jnp.dot(u, w_ref[...], preferred_element_type=F32)
    nslab = QKV_W // 128
    for c in range(2 * nslab):
        res_s[c] = res[:, c * 128:(c + 1) * 128]
    for g, o_ref in ((0, o1_ref), (1, o2_ref)):
        dil = o_ref.shape[0]
        for r in range(dil):
            for c in range(nslab):
                o_ref[r, :, c * 128:(c + 1) * 128] = res_s[g * nslab + c, pl.ds(r, tm // dil, stride=dil), :].astype(
                    o_ref.dtype)


def in_proj_dil(x, sc, sh, w):
    b, s, d = x.shape
    tm = 1024
    d1, d2 = DIL_GROUPS[1][1], DIL_GROUPS[2][1]
    return pl.pallas_call(
        _inproj_dil_kernel,
        out_shape=(jax.ShapeDtypeStruct((b, d1, s // d1, QKV_W), BF16),
                   jax.ShapeDtypeStruct((b, d2, s // d2, QKV_W), BF16)),
        grid=(b, s // tm),
        in_specs=[
            pl.BlockSpec((None, tm, d), lambda bi, i: (bi, i, 0)),
            pl.BlockSpec((None, 1, d), lambda bi, i: (bi, 0, 0)),
            pl.BlockSpec((None, 1, d), lambda bi, i: (bi, 0, 0)),
            pl.BlockSpec((d, 2 * QKV_W), lambda bi, i: (0, 0)),
        ],
        out_specs=(
            pl.BlockSpec((None, d1, tm // d1, QKV_W), lambda bi, i: (bi, 0, i, 0)),
            pl.BlockSpec((None, d2, tm // d2, QKV_W), lambda bi, i: (bi, 0, i, 0)),
        ),
        scratch_shapes=[pltpu.VMEM((2 * QKV_W // 128, tm, 128), F32)],
        compiler_params=_cparams(("parallel", "parallel")),
        name="in_proj_dil",
    )(x, sc, sh, w)


def _rope2(x, cos, sin_signed, lo):
    xr = jnp.where(lo, pltpu.roll(x, 96, 1), pltpu.roll(x, 32, 1))
    return x * cos + xr * sin_signed


def _lane_masks():
    lane = lax.broadcasted_iota(jnp.int32, (1, 128), 1)
    return (lane % 64) < 32, lane < 64


def _band_mask(q0, nq, nk, pad, radius, length):
    qpos = q0 + (lax.broadcasted_iota(jnp.int32, (nq, 1), 0) & 127)
    kpos = q0 - pad + lax.broadcasted_iota(jnp.int32, (1, nk), 1)
    kpos = jnp.where(kpos < 0, -4 * length, jnp.where(kpos >= length, -4 * length, kpos))
    return jnp.abs(qpos - kpos) <= radius


def _dil_attn_kernel(qkv_ref, cos_ref, sin_ref, olo_ref, ohi_ref, llo_ref, lhi_ref, q_s, k_s, v_s, *, dil, ls):
    lo, head0 = _lane_masks()
    heads = (head0, jnp.logical_not(head0))
    cos = cos_ref[...]
    sin = sin_ref[...]
    pad = DIL_RADIUS
    qb = 128
    nblk = ls // qb
    ngroup = 8
    zeros = jnp.zeros((dil, pad, 128), BF16)
    k_s[:, 0:pad, :] = zeros
    k_s[:, pad + ls:pad + ls + pad, :] = zeros
    v_s[:, 0:pad, :] = zeros
    v_s[:, pad + ls:pad + ls + pad, :] = zeros
    for hp, (o_ref, l_ref) in enumerate(((olo_ref, llo_ref), (ohi_ref, lhi_ref))):
        q = qkv_ref[:, :, hp * 128:(hp + 1) * 128].astype(F32).reshape(dil * ls, 128)
        k = qkv_ref[:, :, 256 + hp * 128:256 + (hp + 1) * 128].astype(F32).reshape(dil * ls, 128)
        q_s[...] = (_rope2(q, cos, sin, lo) * (HEAD_DIM ** -0.5)).astype(BF16)
        k_s[:, pad:pad + ls, :] = _rope2(k, cos, sin, lo).astype(BF16).reshape(dil, ls, 128)
        v_s[:, pad:pad + ls, :] = qkv_ref[:, :, 512 + hp * 128:512 + (hp + 1) * 128]

        def grp(gi, carry):
            tiles = []
            for t in range(ngroup):
                c = gi * ngroup + t
                r = c // nblk
                q0 = pl.multiple_of((c % nblk) * qb, qb)
                qblk = q_s[pl.ds(pl.multiple_of(c * qb, qb), qb), :]
                kw = k_s[r, pl.ds(q0, qb + 2 * pad), :]
                vw = v_s[r, pl.ds(q0, qb + 2 * pad), :]
                mask = _band_mask(q0, qb, qb + 2 * pad, pad, DIL_RADIUS, ls)
                ss = [lax.dot_general(jnp.where(hm, qblk, jnp.zeros_like(qblk)), kw, (((1,), (1,)), ((), ())),
                                      preferred_element_type=F32) for hm in heads]
                tiles.append((r, q0, vw, mask, ss))
            soft = []
            for r, q0, vw, mask, ss in tiles:
                for s in ss:
                    s = jnp.where(mask, s, NEG_INF)
                    m = jnp.max(s, axis=-1, keepdims=True)
                    p = jnp.exp(s - m)
                    soft.append((p.astype(BF16), jnp.sum(p, axis=-1, keepdims=True), m))
            for ti, (r, q0, vw, mask, ss) in enumerate(tiles):
                res = []
                for h in range(2):
                    p, den, m = soft[2 * ti + h]
                    res.append((jnp.dot(p, vw, preferred_element_type=F32) / den, m + jnp.log(den)))
                o = jnp.where(head0, res[0][0], res[1][0])
                lse = jnp.where(head0, res[0][1], res[1][1])
                rows = pl.ds(q0, qb) if dil == 1 else pl.ds(q0 * dil + r, qb, stride=dil)
                o_ref[rows, :] = o
                l_ref[rows, :] = lse
            return carry

        lax.fori_loop(0, dil * nblk // ngroup, grp, 0)


def dilated_attention(qkv, col_block, cos_d, sin_d):
    b, dil, ls, _ = qkv.shape
    s = dil * ls
    out_sds = jax.ShapeDtypeStruct((b, s, 128), F32)
    out_spec = pl.BlockSpec((None, s, 128), lambda bi: (bi, 0, 0))
    return pl.pallas_call(
        functools.partial(_dil_attn_kernel, dil=dil, ls=ls),
        out_shape=(out_sds,) * 4,
        grid=(b,),
        in_specs=[
            pl.BlockSpec((None, dil, ls, QKV_W), lambda bi: (bi, 0, 0, col_block)),
            pl.BlockSpec((s, 128), lambda bi: (0, 0)),
            pl.BlockSpec((s, 128), lambda bi: (0, 0)),
        ],
        out_specs=(out_spec,) * 4,
        scratch_shapes=[
            pltpu.VMEM((s, 128), BF16),
            pltpu.VMEM((dil, ls + 2 * DIL_RADIUS, 128), BF16),
            pltpu.VMEM((dil, ls + 2 * DIL_RADIUS, 128), BF16),
        ],
        compiler_params=_cparams(("parallel",)),
        name=f"dil_attn_d{dil}",
    )(qkv, cos_d, sin_d)


def _swa_kernel(qkv_ref, cos_ref, sin_ref, sink_ref, o_ref, q_s, k_s, v_s, *, s_len):
    lo, head0 = _lane_masks()
    heads = (head0, jnp.logical_not(head0))
    cos = cos_ref[...]
    sin = sin_ref[...]
    pad = SWA_RADIUS
    qb = 128
    grp = SWA_Q_HEADS // SWA_KV_HEADS
    zeros = jnp.zeros((pad, 128), BF16)
    k_s[0:pad, :] = zeros
    k_s[pad + s_len:pad + s_len + pad, :] = zeros
    v_s[0:pad, :] = zeros
    v_s[pad + s_len:pad + s_len + pad, :] = zeros
    k = qkv_ref[:, 512:640].astype(F32)
    k_s[pad:pad + s_len, :] = _rope2(k, cos, sin, lo).astype(BF16)
    v_s[pad:pad + s_len, :] = qkv_ref[:, 640:768]
    for qp in range(grp):
        q = qkv_ref[:, qp * 128:(qp + 1) * 128].astype(F32)
        q_s[qp] = (_rope2(q, cos, sin, lo) * (HEAD_DIM ** -0.5)).astype(BF16)
    sinks = [jnp.concatenate([jnp.broadcast_to(sink_ref[kv * grp + p:kv * grp + p + 1, 0:1], (qb, 1))
                              for p in range(grp)], axis=0) for kv in range(SWA_KV_HEADS)]

    def blk(i, carry):
        q0 = pl.multiple_of(i * qb, qb)
        kw = k_s[pl.ds(q0, qb + 2 * pad), :]
        vw = v_s[pl.ds(q0, qb + 2 * pad), :]
        qall = jnp.concatenate([q_s[p, pl.ds(q0, qb), :] for p in range(grp)], axis=0)
        mask = _band_mask(q0, grp * qb, qb + 2 * pad, pad, SWA_RADIUS, s_len)
        ss = [lax.dot_general(jnp.where(hm, qall, jnp.zeros_like(qall)), kw, (((1,), (1,)), ((), ())),
                              preferred_element_type=F32) for hm in heads]
        soft = []
        for s, sk in zip(ss, sinks):
            s = jnp.where(mask, s, NEG_INF)
            m = jnp.maximum(jnp.max(s, axis=-1, keepdims=True), sk)
            p = jnp.exp(s - m)
            soft.append((p.astype(BF16), jnp.sum(p, axis=-1, keepdims=True) + jnp.exp(sk - m)))
        res = [jnp.dot(p, vw, preferred_element_type=F32) / den for p, den in soft]
        o = jnp.where(head0, res[0], res[1]).astype(o_ref.dtype)
        for p in range(grp):
            o_ref[pl.ds(q0, qb), p * 128:(p + 1) * 128] = o[p * qb:(p + 1) * qb, :]
        return carry

    lax.fori_loop(0, s_len // qb, blk, 0, unroll=16)


def swa_attention(proj, cos_t, sin_t, sink):
    b, s, n = proj.shape
    return pl.pallas_call(
        functools.partial(_swa_kernel, s_len=s),
        out_shape=jax.ShapeDtypeStruct((b, s, 512), BF16),
        grid=(b,),
        in_specs=[
            pl.BlockSpec((None, s, QKV_W), lambda bi: (bi, 0, COL_SWA // QKV_W)),
            pl.BlockSpec((s, 128), lambda bi: (0, 0)),
            pl.BlockSpec((s, 128), lambda bi: (0, 0)),
            pl.BlockSpec((8, 128), lambda bi: (0, 0)),
        ],
        out_specs=pl.BlockSpec((None, s, 512), lambda bi: (bi, 0, 0)),
        scratch_shapes=[
            pltpu.VMEM((SWA_Q_HEADS // SWA_KV_HEADS, s, 128), BF16),
            pltpu.VMEM((s + 2 * SWA_RADIUS, 128), BF16),
            pltpu.VMEM((s + 2 * SWA_RADIUS, 128), BF16),
        ],
        compiler_params=_cparams(("parallel",)),
        name="swa_attn",
    )(proj, cos_t, sin_t, jnp.broadcast_to(sink.astype(F32)[:, None], (8, 128)))


HY_CHUNKS = 4


def _hy_filter_kernel(w1_ref, b1_ref, w2_ref, b2_ref, w3_ref, fr_ref, ld_ref, bands_ref, fe_ref, fo_ref,
                      hr_ref, hi_ref, hh_s, slab_s, *, seq):
    hw = HYENA_WIDTH
    hi_p = lax.Precision.HIGHEST

    @pl.when(pl.program_id(2) == 0)
    def _():
        idx = lax.broadcasted_iota(jnp.int32, (seq, 1), 0).astype(F32)
        t = idx / max(seq - 1, 1)
        w = 2.0 * math.pi * idx / seq
        ang = w * bands_ref[...]
        w1 = w1_ref[...]
        pre = (t * w1[0:1, :]
               + jnp.dot(jnp.cos(ang), w1[1:1 + HYENA_BANDS, :], precision=hi_p, preferred_element_type=F32)
               - jnp.dot(jnp.sin(ang), w1[1 + HYENA_BANDS:1 + 2 * HYENA_BANDS, :], precision=hi_p,
                         preferred_element_type=F32)
               + b1_ref[...])
        h = jnp.sin(fr_ref[0:1, :] * pre)
        h = jnp.sin(fr_ref[1:2, :] * (jnp.dot(h, w2_ref[...], precision=hi_p, preferred_element_type=F32)
                                      + b2_ref[...]))
        h = jnp.dot(h, w3_ref[...], precision=hi_p, preferred_element_type=F32)
        h = h * jnp.exp(-t * jnp.exp(ld_ref[...]))
        hf = h[:, :hw]
        hb = jnp.where(idx > 0.0, h[:, hw:], 0.0)
        inv = lax.rsqrt(jnp.sum(hf * hf + hb * hb, axis=0, keepdims=True) + 1e-12)
        for q, comb in enumerate(((hf + hb) * inv, (hf - hb) * inv)):
            for sl in range(hw // 128):
                slab_s[sl] = comb[:, sl * 128:(sl + 1) * 128]
            for par in range(2):
                hh_s[2 * q + par] = jnp.concatenate(
                    [slab_s[sl, pl.ds(par, seq // 2, stride=2), :] for sl in range(hw // 128)], axis=-1).astype(BF16)

    kc = seq // 2 // HY_CHUNKS
    ec = jnp.dot(fe_ref[0:kc, :], hh_s[0], preferred_element_type=F32)
    oc = jnp.dot(fo_ref[0:kc, :], hh_s[1], preferred_element_type=F32)
    es = jnp.dot(fe_ref[kc:2 * kc, :], hh_s[2], preferred_element_type=F32)
    os_ = jnp.dot(fo_ref[kc:2 * kc, :], hh_s[3], preferred_element_type=F32)
    hr_ref[0] = ec + oc
    hr_ref[1] = ec - oc
    hi_ref[0] = -(es + os_)
    hi_ref[1] = es - os_


def hyena_filters(w1, b1, w2, b2, w3p, freq, ldp, fe, fo, seq):
    depth = w1.shape[0]
    half = seq // 2
    kc = half // HY_CHUNKS
    hw = HYENA_WIDTH
    bands = jnp.linspace(1e-4, HYENA_BANDS - 1, HYENA_BANDS, dtype=F32).reshape(1, HYENA_BANDS)
    full = lambda *shape: pl.BlockSpec((None,) + shape, lambda l, o, c: (l,) + (0,) * len(shape))
    out_sds = jax.ShapeDtypeStruct((depth, 2, 2, half, hw), F32)
    out_spec = pl.BlockSpec((None, None, 2, kc, hw), lambda l, o, c: (l, o, 0, c, 0))
    tab_spec = pl.BlockSpec((None, 2 * kc, half), lambda l, o, c: (c, 0, 0))
    hr, hi = pl.pallas_call(
        functools.partial(_hy_filter_kernel, seq=seq),
        out_shape=(out_sds, out_sds),
        grid=(depth, 2, HY_CHUNKS),
        in_specs=[
            full(*w1.shape[1:]), full(1, b1.shape[-1]), full(*w2.shape[1:]), full(1, b2.shape[-1]),
            pl.BlockSpec((None, w3p.shape[1], 2 * hw), lambda l, o, c: (l, 0, o)),
            full(*freq.shape[1:]),
            pl.BlockSpec((None, 1, 2 * hw), lambda l, o, c: (l, 0, o)),
            pl.BlockSpec((1, HYENA_BANDS), lambda l, o, c: (0, 0)),
            tab_spec, tab_spec,
        ],
        out_specs=(out_spec, out_spec),
        scratch_shapes=[pltpu.VMEM((4, half, hw), BF16), pltpu.VMEM((hw // 128, seq, 128), F32)],
        compiler_params=_cparams(("arbitrary", "arbitrary", "arbitrary")),
        name="hyena_filters",
    )(w1, b1[:, None, :], w2, b2[:, None, :], w3p, freq, ldp[:, None, :], bands, fe, fo)
    return hr.reshape(depth, 2, seq, hw), hi.reshape(depth, 2, seq, hw)


def dft_tables_r2(seq):
    half = seq // 2
    kc = half // HY_CHUNKS
    k = jnp.arange(half, dtype=jnp.int32)[:, None]
    n = jnp.arange(half, dtype=jnp.int32)[None, :]
    out = []
    for off in (0, 1):
        ang = (((2 * k + 1) * (2 * n + off)) % (4 * seq)).astype(F32) * (2.0 * math.pi / (4 * seq))
        c = jnp.cos(ang).reshape(HY_CHUNKS, kc, half)
        s = jnp.sin(ang).reshape(HY_CHUNKS, kc, half)
        out.append(jnp.concatenate([c, s], axis=1).astype(BF16))
        out.append((jnp.concatenate([c, -s], axis=1) * (1.0 / seq)).transpose(0, 2, 1).astype(BF16))
    return out[0], out[2], out[1], out[3]


def _hyena_kernel(hy_ref, cw_ref, cb_ref, bias_ref, hr_ref, hi_ref, hrp_ref, hip_ref, fe_ref, fo_ref, ge_ref, go_ref,
                  o_ref, zb_s, zf_s, acc_s, slab_s, *, seq):
    hw = HYENA_WIDTH
    half = seq // 2
    o = pl.program_id(1)
    c = pl.program_id(2)
    kc = half // HY_CHUNKS
    nslab = hw // 128

    def short_conv(part):
        x = hy_ref[:, part * hw:(part + 1) * hw].astype(F32)
        row = lax.broadcasted_iota(jnp.int32, (seq, 1), 0)
        xm = jnp.where(row == 0, 0.0, pltpu.roll(x, 1, 0))
        xp = jnp.where(row == seq - 1, 0.0, pltpu.roll(x, seq - 1, 0))
        w = cw_ref[:, part * hw:(part + 1) * hw]
        return cb_ref[:, part * hw:(part + 1) * hw] + xm * w[0:1, :] + x * w[1:2, :] + xp * w[2:3, :]

    def split(x):
        for sl in range(nslab):
            slab_s[sl] = x[:, sl * 128:(sl + 1) * 128]
        return tuple(jnp.concatenate([slab_s[sl, pl.ds(par, half, stride=2), :] for sl in range(nslab)], axis=-1)
                     for par in (0, 1))

    def interleave(xe, xo):
        for sl in range(nslab):
            slab_s[sl, pl.ds(0, half, stride=2), :] = xe[:, sl * 128:(sl + 1) * 128]
            slab_s[sl, pl.ds(1, half, stride=2), :] = xo[:, sl * 128:(sl + 1) * 128]
        return jnp.concatenate([slab_s[sl] for sl in range(nslab)], axis=-1)

    @pl.when((o == 0) & (c == 0))
    def _():
        for par, z in enumerate(split(short_conv(0))):
            zf_s[par] = z
            zb_s[par] = z.astype(BF16)

    ze = jnp.dot(fe_ref[...], zb_s[0], preferred_element_type=F32)
    zo = jnp.dot(fo_ref[...], zb_s[1], preferred_element_type=F32)
    zc, zs = ze[:kc] + zo[:kc], ze[kc:] + zo[kc:]
    zcp, zsp = ze[:kc] - zo[:kc], zo[kc:] - ze[kc:]
    hr, hi, hrp, hip = hr_ref[...], hi_ref[...], hrp_ref[...], hip_ref[...]
    yr, yi = zc * hr + zs * hi, zc * hi - zs * hr
    yrp, yip = zcp * hrp + zsp * hip, zcp * hip - zsp * hrp
    pe = jnp.dot(ge_ref[...], jnp.concatenate([yr + yrp, yi - yip], axis=0).astype(BF16),
                 preferred_element_type=F32)
    po = jnp.dot(go_ref[...], jnp.concatenate([yr - yrp, yi + yip], axis=0).astype(BF16),
                 preferred_element_type=F32)

    @pl.when(c == 0)
    def _():
        acc_s[0] = pe
        acc_s[1] = po

    @pl.when(c > 0)
    def _():
        acc_s[0] += pe
        acc_s[1] += po

    @pl.when((c == HY_CHUNKS - 1) & (o == 0))
    def _():
        for par, p1 in enumerate(split(short_conv(1))):
            z1 = p1 * (acc_s[par] + bias_ref[0:1, :] * zf_s[par])
            zf_s[par] = z1
            zb_s[par] = z1.astype(BF16)

    @pl.when((c == HY_CHUNKS - 1) & (o == 1))
    def _():
        halves = [p2 * (acc_s[par] + bias_ref[1:2, :] * zf_s[par]) for par, p2 in enumerate(split(short_conv(2)))]
        o_ref[...] = interleave(*halves).astype(o_ref.dtype)


def hyena_mixer(proj, conv_w, conv_b, hy_bias, hr, hi, tabs):
    b, s, n = proj.shape
    hw = HYENA_WIDTH
    half = s // 2
    kc = half // HY_CHUNKS
    nck = HY_CHUNKS
    spec = pl.BlockSpec((None, kc, hw), lambda bi, o, c: (o, c, 0))
    spec_p = pl.BlockSpec((None, kc, hw), lambda bi, o, c: (o, nck + c, 0))
    fwd_spec = pl.BlockSpec((None, 2 * kc, half), lambda bi, o, c: (c, 0, 0))
    inv_spec = pl.BlockSpec((None, half, 2 * kc), lambda bi, o, c: (c, 0, 0))
    fe, fo, ge, go = tabs
    return pl.pallas_call(
        functools.partial(_hyena_kernel, seq=s),
        out_shape=jax.ShapeDtypeStruct((b, s, hw), BF16),
        grid=(b, 2, nck),
        in_specs=[
            pl.BlockSpec((None, s, 3 * hw), lambda bi, o, c: (bi, 0, COL_HY // (3 * hw))),
            pl.BlockSpec((3, 3 * hw), lambda bi, o, c: (0, 0)),
            pl.BlockSpec((1, 3 * hw), lambda bi, o, c: (0, 0)),
            pl.BlockSpec((2, hw), lambda bi, o, c: (0, 0)),
            spec, spec, spec_p, spec_p, fwd_spec, fwd_spec, inv_spec, inv_spec,
        ],
        out_specs=pl.BlockSpec((None, s, hw), lambda bi, o, c: (bi, 0, 0)),
        scratch_shapes=[pltpu.VMEM((2, half, hw), BF16), pltpu.VMEM((2, half, hw), F32),
                        pltpu.VMEM((2, half, hw), F32), pltpu.VMEM((hw // 128, s, 128), F32)],
        compiler_params=_cparams(("parallel", "arbitrary", "arbitrary")),
        name="hyena_conv",
    )(proj, conv_w, conv_b.reshape(1, -1), hy_bias, hr, hi, hr, hi, fe, fo, ge, go)


def _layer_norm(y, g, b):
    mu = jnp.mean(y, axis=-1, keepdims=True)
    yc = y - mu
    var = jnp.mean(yc * yc, axis=-1, keepdims=True)
    return yc * lax.rsqrt(var + LN_EPS) * g + b


def _merge_kernel(*refs):
    att = refs[:12]
    (yb_ref, yc_ref, gl_ref, x_ref, g1_ref, sc2_ref, sh2_ref, lng_ref, lnb_ref, wa_ref, wb_ref, wc_ref, wo_ref,
     xo_ref, u2_ref) = refs[12:]
    d = D_MODEL
    halves = []
    for half in range(2):
        la, lb, lc = (att[4 * g + 2 + half][...] for g in range(3))
        m = jnp.maximum(jnp.maximum(la, lb), lc)
        ea, eb, ec = jnp.exp(la - m), jnp.exp(lb - m), jnp.exp(lc - m)
        inv = 1.0 / (ea + eb + ec)
        halves.append((ea * inv) * att[half][...] + (eb * inv) * att[4 + half][...] + (ec * inv) * att[8 + half][...])
    ya = jnp.concatenate(halves, axis=-1)
    za = jnp.dot(ya.astype(BF16), wa_ref[...], preferred_element_type=F32)
    zb = jnp.dot(yb_ref[...], wb_ref[...], preferred_element_type=F32)
    zc = jnp.dot(yc_ref[...], wc_ref[...], preferred_element_type=F32)
    merged = (jax.nn.sigmoid(gl_ref[:, 0:d].astype(F32)) * za
              + jax.nn.sigmoid(gl_ref[:, d:2 * d].astype(F32)) * zb
              + jax.nn.sigmoid(gl_ref[:, 2 * d:3 * d].astype(F32)) * zc)
    mix = jnp.dot(merged.astype(BF16), wo_ref[...], preferred_element_type=F32)
    xn = _layer_norm(ALPHA * x_ref[...] + g1_ref[...] * mix, lng_ref[...], lnb_ref[...])
    xo_ref[...] = xn
    u2_ref[...] = (xn * (1.0 + sc2_ref[...]) + sh2_ref[...]).T.astype(BF16)


def merge_mixers(att, yb, yc, proj, x, g1, sc2, sh2, ln_g, ln_b, wa, wb, wc, wo):
    b, s, d = x.shape
    tm = 512
    row = lambda w: pl.BlockSpec((None, tm, w), lambda bi, i: (bi, i, 0))
    per_b = pl.BlockSpec((None, 1, d), lambda bi, i: (bi, 0, 0))
    const = lambda r, c: pl.BlockSpec((r, c), lambda bi, i: (0, 0))
    return pl.pallas_call(
        _merge_kernel,
        out_shape=(jax.ShapeDtypeStruct((b, s, d), F32), jax.ShapeDtypeStruct((d, b * s), BF16)),
        grid=(b, s // tm),
        in_specs=[row(128)] * 12 + [row(512), row(512), row(3 * d), row(d), per_b, per_b, per_b,
                                    const(1, d), const(1, d), const(256, d), const(512, d), const(512, d),
                                    const(d, d)],
        out_specs=(row(d), pl.BlockSpec((d, tm), lambda bi, i: (0, bi * (s // tm) + i))),
        compiler_params=_cparams(("parallel", "parallel")),
        name="merge_mixers",
    )(*att, yb, yc, proj, x, g1, sc2, sh2, ln_g.reshape(1, d), ln_b.reshape(1, d), wa, wb, wc, wo)


_PEER_PAIRS = [(i, j) for i in range(1, PEER_TOPK + 1) for j in range(1, PEER_TOPK + 1) if i * j <= PEER_TOPK]
PEER_NOT_TOP = 100.0


def _bitonic_clean_desc(v):
    v = list(v)
    dist = len(v) // 2
    while dist >= 1:
        for a in range(len(v)):
            b = a ^ dist
            if b > a:
                v[a], v[b] = jnp.maximum(v[a], v[b]), jnp.minimum(v[a], v[b])
        dist //= 2
    return v


def _bitonic_sort_desc(v):
    v = list(v)
    k = 2
    while k <= len(v):
        dist = k // 2
        while dist >= 1:
            for a in range(len(v)):
                b = a ^ dist
                if b > a:
                    hi, lo = jnp.maximum(v[a], v[b]), jnp.minimum(v[a], v[b])
                    v[a], v[b] = (hi, lo) if (a & k) == 0 else (lo, hi)
            dist //= 2
        k *= 2
    return v


def _merge_top(v, w):
    n = len(v)
    return _bitonic_clean_desc([jnp.maximum(v[a], w[n - 1 - a]) for a in range(n)])


def _sorted_prefix_count(vals, pred):
    w = jnp.where
    t8 = pred(vals[7])
    t4 = pred(w(t8, vals[11], vals[3]))
    t2 = pred(w(t8, w(t4, vals[13], vals[9]), w(t4, vals[5], vals[1])))
    lo = w(t4, w(t2, vals[6], vals[4]), w(t2, vals[2], vals[0]))
    hi = w(t4, w(t2, vals[14], vals[12]), w(t2, vals[10], vals[8]))
    t1 = pred(w(t8, hi, lo))
    t16 = pred(vals[15])
    count = w(t8, 8.0, 0.0) + w(t4, 4.0, 0.0) + w(t2, 2.0, 0.0) + w(t1, 1.0, 0.0) + w(t16, 1.0, 0.0)
    return count, t16


def _peer_route_kernel(u_ref, wq_ref, keys_ref, e1_ref, cut_ref, e2_ref, r2_ref, q_s, s_s, ab_s, st_s, *, tr):
    nh, nk, topk = PEER_HEADS, PEER_KEYS, PEER_TOPK
    qt = jnp.dot(wq_ref[...], u_ref[...], preferred_element_type=F32)
    q_s[...] = qt.astype(BF16)
    for ph in range(2 * nh):
        s_s[ph] = jnp.dot(keys_ref[ph], q_s[ph * 128:(ph + 1) * 128, :], preferred_element_type=F32)

    def extract(h, carry):
        for p in range(2):
            sv = s_s[p * nh + h]
            v = _bitonic_sort_desc([sv[g * 8:(g + 1) * 8, :] for g in range(topk)])
            for shift in (4, 2, 1):
                v = _merge_top(v, [pltpu.roll(x, shift, 0) for x in v])
            for r in range(topk):
                ab_s[p, r, pl.ds(h, 1), :] = v[r][0:1, :]
        return carry

    lax.fori_loop(0, nh, extract, 0)

    for ch in range(tr // 128):
        ln = slice(ch * 128, (ch + 1) * 128)
        a = [ab_s[0, r, :, ln] for r in range(topk)]
        b = [ab_s[1, r, :, ln] for r in range(topk)]
        cand = [a[i - 1] + b[j - 1] for (i, j) in _PEER_PAIRS]
        rest = cand[topk:] + [jnp.full((nh, 128), -jnp.inf, F32)] * (4 * topk - len(cand))
        groups = [cand[:topk]] + [_bitonic_sort_desc(rest[g * topk:(g + 1) * topk]) for g in range(3)]
        m01 = _merge_top(groups[0], groups[1])
        m23 = _merge_top(groups[2], groups[3])
        tau = functools.reduce(jnp.minimum, [jnp.maximum(m01[r], m23[topk - 1 - r]) for r in range(topk)])
        top = cand[0]
        z = jnp.zeros((nh, 128), F32)
        for cx in cand:
            z = z + jnp.where(cx >= tau, jnp.exp(cx - top), 0.0)
        st_s[0, :, ln] = tau
        st_s[1, :, ln] = 1.0 / z

    for h in range(nh):
        rows = slice(h * nk, (h + 1) * nk)
        s1 = s_s[h]
        s2 = s_s[nh + h]
        hrow = slice(h, h + 1)
        tau = st_s[0, hrow, :]
        in1 = s1 >= ab_s[0, topk - 1, hrow, :]
        bs = [ab_s[1, r, hrow, :] for r in range(topk)]
        cut, _ = _sorted_prefix_count(bs, lambda v: s1 + v >= tau)
        above, below_all = _sorted_prefix_count(bs, lambda v: v > s2)
        e1_ref[rows, :] = jnp.where(in1, jnp.exp(s1 - ab_s[0, 0, hrow, :]), 0.0)
        cut_ref[rows, :] = jnp.where(in1, cut, 0.0)
        e2_ref[rows, :] = jnp.where(below_all, 0.0, jnp.exp(s2 - bs[0]) * st_s[1, hrow, :]).astype(BF16)
        r2_ref[rows, :] = jnp.where(below_all, PEER_NOT_TOP, above + 1.0).astype(BF16)


def peer_route(u2, wq_t, keys):
    d, t = u2.shape
    tr = 512
    nrow = PEER_HEADS * PEER_KEYS
    tab = lambda dt: jax.ShapeDtypeStruct((nrow, t), dt)
    col = pl.BlockSpec((nrow, tr), lambda i: (0, i))
    return pl.pallas_call(
        functools.partial(_peer_route_kernel, tr=tr),
        out_shape=(tab(F32), tab(F32), tab(BF16), tab(BF16)),
        grid=(t // tr,),
        in_specs=[
            pl.BlockSpec((d, tr), lambda i: (0, i)),
            pl.BlockSpec(wq_t.shape, lambda i: (0, 0)),
            pl.BlockSpec(keys.shape, lambda i: (0, 0, 0)),
        ],
        out_specs=(col, col, col, col),
        scratch_shapes=[
            pltpu.VMEM((2 * nrow, tr), BF16),
            pltpu.VMEM((2 * PEER_HEADS, PEER_KEYS, tr), F32),
            pltpu.VMEM((2, PEER_TOPK, PEER_HEADS, tr), F32),
            pltpu.VMEM((2, PEER_HEADS, tr), F32),
        ],
        compiler_params=_cparams(("parallel",)),
        name="peer_route",
    )(u2, wq_t, keys)


F8 = jnp.float8_e4m3fn
F8_MAX = 448.0
WG_SCALE = 16.0


def _peer_dense_kernel(sc_ref, x_ref, u_ref, vt_ref, e1_ref, cut_ref, e2_ref, r2_ref, xres_ref, g2_ref, lng_ref,
                       lnb_ref, o_ref, acc_s, wga_s, wgb_s, *, te, n_e):
    j = pl.program_id(1)
    nk = PEER_KEYS
    tn = x_ref.shape[1]
    sub = 16

    @pl.when(j == 0)
    def _():
        acc_s[...] = jnp.zeros_like(acc_s)
        wgb_s[...] = jnp.zeros_like(wgb_s)

    def step(w_cur, w_prev):
        inv_a = sc_ref[0]
        ts = 512
        c_lin = jnp.full((1, ts), 0.5 * WG_SCALE * inv_a, F32).astype(BF16)
        c_erf = jnp.full((1, ts), (2.0 ** -0.5) * inv_a, F32).astype(BF16)

        def row_bcast(ref, h, ii, lanes):
            chunks = [jnp.broadcast_to(ref[h, ii:ii + 1, c0:c0 + 128], (sub, 128)).astype(BF16)
                      for c0 in range(lanes.start, lanes.stop, 128)]
            return jnp.concatenate(chunks, axis=-1)[None]

        for sl in range(tn // ts):
            lanes = slice(sl * ts, (sl + 1) * ts)
            for ii in range(te // nk):
                rows = slice(ii * nk, (ii + 1) * nk)
                at = jnp.dot(u_ref[rows, :], x_ref[:, lanes], preferred_element_type=F32).astype(BF16)
                if ii == 1:
                    acc_s[:, lanes] += jnp.dot(vt_ref[...], w_prev[:, lanes], preferred_element_type=F32)
                g = (c_lin * at) * (1.0 + lax.erf(at * c_erf))
                w = None
                for h in range(PEER_HEADS):
                    cb = row_bcast(cut_ref, h, ii, lanes)
                    e1 = row_bcast(e1_ref, h, ii, lanes)
                    r2 = r2_ref[h * nk:(h + 1) * nk, lanes].reshape(nk // sub, sub, ts)
                    e2 = e2_ref[h * nk:(h + 1) * nk, lanes].reshape(nk // sub, sub, ts)
                    sel = jnp.where(r2 <= cb, e2, jnp.zeros((), BF16)) * e1
                    w = sel if w is None else w + sel
                wg = jnp.clip(g * w.reshape(nk, ts), -F8_MAX, F8_MAX)
                w_cur[rows, lanes] = wg.astype(F8)

    @pl.when((j % 2 == 0) & (j < n_e))
    def _():
        step(wga_s, wgb_s)

    @pl.when(j % 2 == 1)
    def _():
        step(wgb_s, wga_s)

    @pl.when(j == n_e)
    def _():
        acc = acc_s[...] + jnp.dot(vt_ref[...], wgb_s[...], preferred_element_type=F32)
        ffn = (acc * sc_ref[1]).T
        o_ref[...] = _layer_norm(ALPHA * xres_ref[...] + g2_ref[...] * ffn, lng_ref[...], lnb_ref[...])


def peer_dense(scales, x8, u_tab, vt_tab, e1, cut, e2, r2, x_res, g2, ln_g, ln_b):
    d, t = x8.shape
    b, s, _ = x_res.shape
    ne = u_tab.shape[0]
    tn, te = 1024, 1024
    n_e = ne // te
    nrow = PEER_HEADS * PEER_KEYS
    col = pl.BlockSpec((nrow, tn), lambda i, j: (0, i))
    step_rows = pl.BlockSpec((PEER_HEADS, te // PEER_KEYS, tn), lambda i, j: (0, jnp.minimum(j, n_e - 1), i))
    e1 = e1.reshape(PEER_HEADS, PEER_KEYS, t)
    cut = cut.reshape(PEER_HEADS, PEER_KEYS, t)
    row = pl.BlockSpec((None, tn, d), lambda i, j: (i // (s // tn), i % (s // tn), 0))
    const = pl.BlockSpec((1, d), lambda i, j: (0, 0))
    return pl.pallas_call(
        functools.partial(_peer_dense_kernel, te=te, n_e=n_e),
        out_shape=jax.ShapeDtypeStruct((b, s, d), F32),
        grid=(t // tn, n_e + 1),
        in_specs=[
            pl.BlockSpec(memory_space=pltpu.SMEM),
            pl.BlockSpec((d, tn), lambda i, j: (0, i)),
            pl.BlockSpec((te, d), lambda i, j: (jnp.minimum(j, n_e - 1), 0)),
            pl.BlockSpec((d, te), lambda i, j: (0, jnp.maximum(j - 1, 0))),
            step_rows, step_rows, col, col,
            row, pl.BlockSpec((None, 1, d), lambda i, j: (i // (s // tn), 0, 0)), const, const,
        ],
        out_specs=row,
        scratch_shapes=[pltpu.VMEM((d, tn), F32), pltpu.VMEM((te, tn), F8), pltpu.VMEM((te, tn), F8)],
        compiler_params=_cparams(("parallel", "arbitrary")),
        name="peer_dense",
    )(scales, x8, u_tab, vt_tab, e1, cut, e2, r2, x_res, g2, ln_g.reshape(1, d), ln_b.reshape(1, d))


def rope_tables(seq):
    pos = jnp.arange(seq, dtype=F32)
    inv = ROPE_THETA ** (-jnp.arange(0, HEAD_DIM, 2, dtype=F32) / HEAD_DIM)
    ang = pos[:, None] * inv[None, :]
    sign = jnp.where((jnp.arange(128) % 64) < 32, -1.0, 1.0).astype(F32)
    return jnp.tile(jnp.cos(ang), (1, 4)), jnp.tile(jnp.sin(ang), (1, 4)) * sign[None, :]


def _swa_head_perm():
    heads = [h for p in range(4) for h in (p, 4 + p)]
    return jnp.concatenate([jnp.arange(h * HEAD_DIM, (h + 1) * HEAD_DIM) for h in heads])


def relayout_w_in(w):
    qa, ka, va = w[:, 0:768], w[:, 768:1536], w[:, 1536:2304]
    hy = w[:, 2304:3840]
    qc, kvc = w[:, 3840:4352], w[:, 4352:4608]
    gates = w[:, 4608:7680]
    dil = [jnp.concatenate([t[:, 256 * g:256 * (g + 1)] for t in (qa, ka, va)], axis=1) for g in range(3)]
    main = jnp.concatenate([gates, hy, dil[0], qc[:, _swa_head_perm()], kvc], axis=1).astype(BF16)
    return main, jnp.concatenate(dil[1:], axis=1).astype(BF16)


def residue_major(tab, dil):
    s = tab.shape[0]
    return tab.reshape(s // dil, dil, 128).transpose(1, 0, 2).reshape(s, 128)


def token_mixer(x, ada, w_main, w_dil, conv_w, conv_b, hy_bias, attn_sink, hr, hi, dft_tabs, rope_tabs,
                wa, wb, wc, wo, ln_g, ln_b):
    b, s, _ = x.shape
    sh1, sc1, g1, sh2, sc2, _ = [a[:, None, :] for a in jnp.split(ada, 6, axis=-1)]
    proj = in_proj(x, sc1, sh1, w_main)
    qkv1, qkv2 = in_proj_dil(x, sc1, sh1, w_dil)
    att = (dilated_attention(proj.reshape(b, 1, s, MAIN_WIDTH), COL_DIL0 // QKV_W, *rope_tabs[0])
           + dilated_attention(qkv1, 0, *rope_tabs[1]) + dilated_attention(qkv2, 0, *rope_tabs[2]))
    yb = hyena_mixer(proj, conv_w, conv_b, hy_bias, hr, hi, dft_tabs)
    yc = swa_attention(proj, *rope_tabs[0], attn_sink)
    return merge_mixers(att, yb, yc, proj, x, g1, sc2, sh2, ln_g, ln_b, wa, wb, wc, wo)


def peer_sublayer(u2, wq, keys, u_tab, v_tab, x_res, g2, ln_g, ln_b):
    d = u2.shape[0]
    nh, nk = PEER_HEADS, PEER_KEYS
    wq_t = wq.reshape(d, nh, 2, nk).transpose(2, 1, 3, 0).reshape(2 * nh * nk, d).astype(BF16)
    keys_r = keys.transpose(1, 0, 2, 3).reshape(2 * nh, nk, keys.shape[-1]).astype(BF16)
    tabs = peer_route(u2, wq_t, keys_r)
    su, sv, sx = _pow2_scale(u_tab), _pow2_scale(v_tab), _pow2_scale(u2)
    scales = jnp.stack([1.0 / (su * sx), 1.0 / (sv * WG_SCALE)]).astype(F32)
    x8 = (u2.astype(F32) * sx).astype(F8)
    return peer_dense(scales, x8, (u_tab * su).astype(F8), (v_tab.T * sv).astype(F8), *tabs, x_res, g2, ln_g, ln_b)


def _pow2_scale(a):
    m = jnp.maximum(jnp.max(jnp.abs(a)).astype(F32), 2.0 ** -100)
    return jnp.exp2(jnp.floor(jnp.log2(F8_MAX / m)))


def kernel(x, c, w_ada, b_ada, w_in, conv_w, conv_b, hy_w1, hy_b1, hy_w2, hy_b2, hy_w3, hy_freq, hy_log_decay,
           hy_bias, attn_sink, w_branch_a, w_branch_b, w_branch_c, w_out, ln_g, ln_b, peer_wq, peer_keys, peer_u,
           peer_v):
    b, s, d = x.shape
    depth = w_in.shape[0]
    hw = HYENA_WIDTH
    cos_t, sin_t = rope_tables(s)
    dft_tabs = dft_tables_r2(s)
    ada = ada_ln(c, w_ada, b_ada)
    w3p = hy_w3.reshape(depth, -1, 2, 2, hw).transpose(0, 1, 3, 2, 4).reshape(depth, -1, 4 * hw)
    ldp = hy_log_decay.reshape(depth, 2, 2, hw).transpose(0, 2, 1, 3).reshape(depth, 4 * hw)
    hr, hi = hyena_filters(hy_w1, hy_b1, hy_w2, hy_b2, w3p, hy_freq, ldp, dft_tabs[0], dft_tabs[1], s)
    perm = _swa_head_perm()
    rope_tabs = [(residue_major(cos_t, dil), residue_major(sin_t, dil)) for _, dil in DIL_GROUPS]
    for l in range(depth):
        x1, u2 = token_mixer(x, ada[l], *relayout_w_in(w_in[l]), conv_w[l], conv_b[l], hy_bias[l], attn_sink[l],
                             hr[l], hi[l], dft_tabs, rope_tabs, w_branch_a[l].astype(BF16),
                             w_branch_b[l].astype(BF16), w_branch_c[l][perm].astype(BF16), w_out[l].astype(BF16),
                             ln_g[l, 0], ln_b[l, 0])
        x = peer_sublayer(u2, peer_wq[l], peer_keys[l], peer_u[l], peer_v[l], x1, ada[l][:, None, 5 * d:6 * d],
                          ln_g[l, 1], ln_b[l, 1])
    return x
```

```python
import functools
import math

import jax
import jax.numpy as jnp
from jax import lax
from jax.experimental import pallas as pl
from jax.experimental.pallas import tpu as pltpu

F32 = jnp.float32
BF16 = jnp.bfloat16

D_MODEL = 1024
HEAD_DIM = 64
DIL_GROUPS = ((128, 1), (512, 4), (2048, 16))
DIL_RADIUS = 64
HYENA_WIDTH = 512
HYENA_BANDS = 16
SWA_RADIUS = 128
SWA_Q_HEADS = 8
SWA_KV_HEADS = 2
PEER_HEADS = 8
PEER_KEYS = 128
PEER_TOPK = 16
DEPTH = 2
ALPHA = (2 * DEPTH) ** 0.25
LN_EPS = 1e-5
NEG_INF = -1e30
ROPE_THETA = 10000.0

COL_GATES = 0
COL_HY = 3072
COL_DIL0 = 4608
COL_SWA = 5376
MAIN_WIDTH = 6144
QKV_W = 768

V7X_VMEM_BYTES = 64 * 1024 * 1024
VMEM_LIMIT = V7X_VMEM_BYTES - 8 * 1024 * 1024


def _cparams(sem, vmem=VMEM_LIMIT):
    return pltpu.CompilerParams(dimension_semantics=sem, vmem_limit_bytes=vmem)


def _ada_kernel(c_ref, w_ref, b_ref, o_ref):
    c = c_ref[...]
    cond = (c * jax.nn.sigmoid(c)).astype(BF16)
    o_ref[...] = jnp.dot(cond, w_ref[...].astype(BF16), preferred_element_type=F32) + b_ref[...]


def ada_ln(c, w_ada, b_ada):
    depth, d, n = w_ada.shape
    b = c.shape[0]
    tn = 1536
    return pl.pallas_call(
        _ada_kernel,
        out_shape=jax.ShapeDtypeStruct((depth, b, n), F32),
        grid=(depth, n // tn),
        in_specs=[
            pl.BlockSpec((b, d), lambda l, j: (0, 0)),
            pl.BlockSpec((None, d, tn), lambda l, j: (l, 0, j)),
            pl.BlockSpec((None, 1, tn), lambda l, j: (l, 0, j)),
        ],
        out_specs=pl.BlockSpec((None, b, tn), lambda l, j: (l, 0, j)),
        compiler_params=_cparams(("arbitrary", "arbitrary")),
        name="ada_ln",
    )(c, w_ada, b_ada.reshape(depth, 1, n))


def _inproj_kernel(x_ref, sc_ref, sh_ref, w_ref, o_ref, u_ref):
    @pl.when(pl.program_id(2) == 0)
    def _():
        u_ref[...] = (x_ref[...] * (1.0 + sc_ref[...]) + sh_ref[...]).astype(BF16)

    o_ref[...] = jnp.dot(u_ref[...], w_ref[...], preferred_element_type=F32).astype(o_ref.dtype)


def in_proj(x, sc, sh, w):
    b, s, d = x.shape
    n = w.shape[1]
    tm, tn = 1024, 1536
    return pl.pallas_call(
        _inproj_kernel,
        out_shape=jax.ShapeDtypeStruct((b, s, n), BF16),
        grid=(b, s // tm, n // tn),
        in_specs=[
            pl.BlockSpec((None, tm, d), lambda bi, i, j: (bi, i, 0)),
            pl.BlockSpec((None, 1, d), lambda bi, i, j: (bi, 0, 0)),
            pl.BlockSpec((None, 1, d), lambda bi, i, j: (bi, 0, 0)),
            pl.BlockSpec((d, tn), lambda bi, i, j: (0, j)),
        ],
        out_specs=pl.BlockSpec((None, tm, tn), lambda bi, i, j: (bi, i, j)),
        scratch_shapes=[pltpu.VMEM((tm, d), BF16)],
        compiler_params=_cparams(("parallel", "parallel", "arbitrary")),
        name="in_proj",
    )(x, sc, sh, w)


def _inproj_dil_kernel(x_ref, sc_ref, sh_ref, w_ref, o1_ref, o2_ref, res_s):
    tm = x_ref.shape[0]
    u = (x_ref[...] * (1.0 + sc_ref[...]) + sh_ref[...]).astype(BF16)
    res = jnp.dot(u, w_ref[...], preferred_element_type=F32)
    nslab = QKV_W // 128
    for c in range(2 * nslab):
        res_s[c] = res[:, c * 128:(c + 1) * 128]
    for g, o_ref in ((0, o1_ref), (1, o2_ref)):
        dil = o_ref.shape[0]
        for r in range(dil):
            for c in range(nslab):
                o_ref[r, :, c * 128:(c + 1) * 128] = res_s[g * nslab + c, pl.ds(r, tm // dil, stride=dil), :].astype(
                    o_ref.dtype)


def in_proj_dil(x, sc, sh, w):
    b, s, d = x.shape
    tm = 1024
    d1, d2 = DIL_GROUPS[1][1], DIL_GROUPS[2][1]
    return pl.pallas_call(
        _inproj_dil_kernel,
        out_shape=(jax.ShapeDtypeStruct((b, d1, s // d1, QKV_W), BF16),
                   jax.ShapeDtypeStruct((b, d2, s // d2, QKV_W), BF16)),
        grid=(b, s // tm),
        in_specs=[
            pl.BlockSpec((None, tm, d), lambda bi, i: (bi, i, 0)),
            pl.BlockSpec((None, 1, d), lambda bi, i: (bi, 0, 0)),
            pl.BlockSpec((None, 1, d), lambda bi, i: (bi, 0, 0)),
            pl.BlockSpec((d, 2 * QKV_W), lambda bi, i: (0, 0)),
        ],
        out_specs=(
            pl.BlockSpec((None, d1, tm // d1, QKV_W), lambda bi, i: (bi, 0, i, 0)),
            pl.BlockSpec((None, d2, tm // d2, QKV_W), lambda bi, i: (bi, 0, i, 0)),
        ),
        scratch_shapes=[pltpu.VMEM((2 * QKV_W // 128, tm, 128), F32)],
        compiler_params=_cparams(("parallel", "parallel")),
        name="in_proj_dil",
    )(x, sc, sh, w)


def _rope2(x, cos, sin_signed, lo):
    xr = jnp.where(lo, pltpu.roll(x, 96, 1), pltpu.roll(x, 32, 1))
    return x * cos + xr * sin_signed


def _lane_masks():
    lane = lax.broadcasted_iota(jnp.int32, (1, 128), 1)
    return (lane % 64) < 32, lane < 64


def _band_mask(q0, nq, nk, pad, radius, length):
    qpos = q0 + (lax.broadcasted_iota(jnp.int32, (nq, 1), 0) & 127)
    kpos = q0 - pad + lax.broadcasted_iota(jnp.int32, (1, nk), 1)
    kpos = jnp.where(kpos < 0, -4 * length, jnp.where(kpos >= length, -4 * length, kpos))
    return jnp.abs(qpos - kpos) <= radius


def _dil_attn_kernel(qkv_ref, cos_ref, sin_ref, olo_ref, ohi_ref, llo_ref, lhi_ref, q_s, k_s, v_s, *, dil, ls):
    lo, head0 = _lane_masks()
    heads = (head0, jnp.logical_not(head0))
    cos = cos_ref[...]
    sin = sin_ref[...]
    pad = DIL_RADIUS
    qb = 128
    nblk = ls // qb
    ngroup = 8
    zeros = jnp.zeros((dil, pad, 128), BF16)
    k_s[:, 0:pad, :] = zeros
    k_s[:, pad + ls:pad + ls + pad, :] = zeros
    v_s[:, 0:pad, :] = zeros
    v_s[:, pad + ls:pad + ls + pad, :] = zeros
    for hp, (o_ref, l_ref) in enumerate(((olo_ref, llo_ref), (ohi_ref, lhi_ref))):
        q = qkv_ref[:, :, hp * 128:(hp + 1) * 128].astype(F32).reshape(dil * ls, 128)
        k = qkv_ref[:, :, 256 + hp * 128:256 + (hp + 1) * 128].astype(F32).reshape(dil * ls, 128)
        q_s[...] = (_rope2(q, cos, sin, lo) * (HEAD_DIM ** -0.5)).astype(BF16)
        k_s[:, pad:pad + ls, :] = _rope2(k, cos, sin, lo).astype(BF16).reshape(dil, ls, 128)
        v_s[:, pad:pad + ls, :] = qkv_ref[:, :, 512 + hp * 128:512 + (hp + 1) * 128]

        def grp(gi, carry):
            tiles = []
            for t in range(ngroup):
                c = gi * ngroup + t
                r = c // nblk
                q0 = pl.multiple_of((c % nblk) * qb, qb)
                qblk = q_s[pl.ds(pl.multiple_of(c * qb, qb), qb), :]
                kw = k_s[r, pl.ds(q0, qb + 2 * pad), :]
                vw = v_s[r, pl.ds(q0, qb + 2 * pad), :]
                mask = _band_mask(q0, qb, qb + 2 * pad, pad, DIL_RADIUS, ls)
                ss = [lax.dot_general(jnp.where(hm, qblk, jnp.zeros_like(qblk)), kw, (((1,), (1,)), ((), ())),
                                      preferred_element_type=F32) for hm in heads]
                tiles.append((r, q0, vw, mask, ss))
            soft = []
            for r, q0, vw, mask, ss in tiles:
                for s in ss:
                    s = jnp.where(mask, s, NEG_INF)
                    m = jnp.max(s, axis=-1, keepdims=True)
                    p = jnp.exp(s - m)
                    soft.append((p.astype(BF16), jnp.sum(p, axis=-1, keepdims=True), m))
            for ti, (r, q0, vw, mask, ss) in enumerate(tiles):
                res = []
                for h in range(2):
                    p, den, m = soft[2 * ti + h]
                    res.append((jnp.dot(p, vw, preferred_element_type=F32) / den, m + jnp.log(den)))
                o = jnp.where(head0, res[0][0], res[1][0])
                lse = jnp.where(head0, res[0][1], res[1][1])
                rows = pl.ds(q0, qb) if dil == 1 else pl.ds(q0 * dil + r, qb, stride=dil)
                o_ref[rows, :] = o
                l_ref[rows, :] = lse
            return carry

        lax.fori_loop(0, dil * nblk // ngroup, grp, 0)


def dilated_attention(qkv, col_block, cos_d, sin_d):
    b, dil, ls, _ = qkv.shape
    s = dil * ls
    out_sds = jax.ShapeDtypeStruct((b, s, 128), F32)
    out_spec = pl.BlockSpec((None, s, 128), lambda bi: (bi, 0, 0))
    return pl.pallas_call(
        functools.partial(_dil_attn_kernel, dil=dil, ls=ls),
        out_shape=(out_sds,) * 4,
        grid=(b,),
        in_specs=[
            pl.BlockSpec((None, dil, ls, QKV_W), lambda bi: (bi, 0, 0, col_block)),
            pl.BlockSpec((s, 128), lambda bi: (0, 0)),
            pl.BlockSpec((s, 128), lambda bi: (0, 0)),
        ],
        out_specs=(out_spec,) * 4,
        scratch_shapes=[
            pltpu.VMEM((s, 128), BF16),
            pltpu.VMEM((dil, ls + 2 * DIL_RADIUS, 128), BF16),
            pltpu.VMEM((dil, ls + 2 * DIL_RADIUS, 128), BF16),
        ],
        compiler_params=_cparams(("parallel",)),
        name=f"dil_attn_d{dil}",
    )(qkv, cos_d, sin_d)


def _swa_kernel(qkv_ref, cos_ref, sin_ref, sink_ref, o_ref, q_s, k_s, v_s, *, s_len):
    lo, head0 = _lane_masks()
    heads = (head0, jnp.logical_not(head0))
    cos = cos_ref[...]
    sin = sin_ref[...]
    pad = SWA_RADIUS
    qb = 128
    grp = SWA_Q_HEADS // SWA_KV_HEADS
    zeros = jnp.zeros((pad, 128), BF16)
    k_s[0:pad, :] = zeros
    k_s[pad + s_len:pad + s_len + pad, :] = zeros
    v_s[0:pad, :] = zeros
    v_s[pad + s_len:pad + s_len + pad, :] = zeros
    k = qkv_ref[:, 512:640].astype(F32)
    k_s[pad:pad + s_len, :] = _rope2(k, cos, sin, lo).astype(BF16)
    v_s[pad:pad + s_len, :] = qkv_ref[:, 640:768]
    for qp in range(grp):
        q = qkv_ref[:, qp * 128:(qp + 1) * 128].astype(F32)
        q_s[qp] = (_rope2(q, cos, sin, lo) * (HEAD_DIM ** -0.5)).astype(BF16)
    sinks = [jnp.concatenate([jnp.broadcast_to(sink_ref[kv * grp + p:kv * grp + p + 1, 0:1], (qb, 1))
                              for p in range(grp)], axis=0) for kv in range(SWA_KV_HEADS)]

    def blk(i, carry):
        q0 = pl.multiple_of(i * qb, qb)
        kw = k_s[pl.ds(q0, qb + 2 * pad), :]
        vw = v_s[pl.ds(q0, qb + 2 * pad), :]
        qall = jnp.concatenate([q_s[p, pl.ds(q0, qb), :] for p in range(grp)], axis=0)
        mask = _band_mask(q0, grp * qb, qb + 2 * pad, pad, SWA_RADIUS, s_len)
        ss = [lax.dot_general(jnp.where(hm, qall, jnp.zeros_like(qall)), kw, (((1,), (1,)), ((), ())),
                              preferred_element_type=F32) for hm in heads]
        soft = []
        for s, sk in zip(ss, sinks):
            s = jnp.where(mask, s, NEG_INF)
            m = jnp.maximum(jnp.max(s, axis=-1, keepdims=True), sk)
            p = jnp.exp(s - m)
            soft.append((p.astype(BF16), jnp.sum(p, axis=-1, keepdims=True) + jnp.exp(sk - m)))
        res = [jnp.dot(p, vw, preferred_element_type=F32) / den for p, den in soft]
        o = jnp.where(head0, res[0], res[1]).astype(o_ref.dtype)
        for p in range(grp):
            o_ref[pl.ds(q0, qb), p * 128:(p + 1) * 128] = o[p * qb:(p + 1) * qb, :]
        return carry

    lax.fori_loop(0, s_len // qb, blk, 0, unroll=16)


def swa_attention(proj, cos_t, sin_t, sink):
    b, s, n = proj.shape
    return pl.pallas_call(
        functools.partial(_swa_kernel, s_len=s),
        out_shape=jax.ShapeDtypeStruct((b, s, 512), BF16),
        grid=(b,),
        in_specs=[
            pl.BlockSpec((None, s, QKV_W), lambda bi: (bi, 0, COL_SWA // QKV_W)),
            pl.BlockSpec((s, 128), lambda bi: (0, 0)),
            pl.BlockSpec((s, 128), lambda bi: (0, 0)),
            pl.BlockSpec((8, 128), lambda bi: (0, 0)),
        ],
        out_specs=pl.BlockSpec((None, s, 512), lambda bi: (bi, 0, 0)),
        scratch_shapes=[
            pltpu.VMEM((SWA_Q_HEADS // SWA_KV_HEADS, s, 128), BF16),
            pltpu.VMEM((s + 2 * SWA_RADIUS, 128), BF16),
            pltpu.VMEM((s + 2 * SWA_RADIUS, 128), BF16),
        ],
        compiler_params=_cparams(("parallel",)),
        name="swa_attn",
    )(proj, cos_t, sin_t, jnp.broadcast_to(sink.astype(F32)[:, None], (8, 128)))


HY_CHUNKS = 4


def _hy_filter_kernel(w1_ref, b1_ref, w2_ref, b2_ref, w3_ref, fr_ref, ld_ref, bands_ref, fe_ref, fo_ref,
                      hr_ref, hi_ref, hh_s, slab_s, *, seq):
    hw = HYENA_WIDTH
    hi_p = lax.Precision.HIGHEST

    @pl.when(pl.program_id(2) == 0)
    def _():
        idx = lax.broadcasted_iota(jnp.int32, (seq, 1), 0).astype(F32)
        t = idx / max(seq - 1, 1)
        w = 2.0 * math.pi * idx / seq
        ang = w * bands_ref[...]
        w1 = w1_ref[...]
        pre = (t * w1[0:1, :]
               + jnp.dot(jnp.cos(ang), w1[1:1 + HYENA_BANDS, :], precision=hi_p, preferred_element_type=F32)
               - jnp.dot(jnp.sin(ang), w1[1 + HYENA_BANDS:1 + 2 * HYENA_BANDS, :], precision=hi_p,
                         preferred_element_type=F32)
               + b1_ref[...])
        h = jnp.sin(fr_ref[0:1, :] * pre)
        h = jnp.sin(fr_ref[1:2, :] * (jnp.dot(h, w2_ref[...], precision=hi_p, preferred_element_type=F32)
                                      + b2_ref[...]))
        h = jnp.dot(h, w3_ref[...], precision=hi_p, preferred_element_type=F32)
        h = h * jnp.exp(-t * jnp.exp(ld_ref[...]))
        hf = h[:, :hw]
        hb = jnp.where(idx > 0.0, h[:, hw:], 0.0)
        inv = lax.rsqrt(jnp.sum(hf * hf + hb * hb, axis=0, keepdims=True) + 1e-12)
        for q, comb in enumerate(((hf + hb) * inv, (hf - hb) * inv)):
            for sl in range(hw // 128):
                slab_s[sl] = comb[:, sl * 128:(sl + 1) * 128]
            for par in range(2):
                hh_s[2 * q + par] = jnp.concatenate(
                    [slab_s[sl, pl.ds(par, seq // 2, stride=2), :] for sl in range(hw // 128)], axis=-1).astype(BF16)

    kc = seq // 2 // HY_CHUNKS
    ec = jnp.dot(fe_ref[0:kc, :], hh_s[0], preferred_element_type=F32)
    oc = jnp.dot(fo_ref[0:kc, :], hh_s[1], preferred_element_type=F32)
    es = jnp.dot(fe_ref[kc:2 * kc, :], hh_s[2], preferred_element_type=F32)
    os_ = jnp.dot(fo_ref[kc:2 * kc, :], hh_s[3], preferred_element_type=F32)
    hr_ref[0] = ec + oc
    hr_ref[1] = ec - oc
    hi_ref[0] = -(es + os_)
    hi_ref[1] = es - os_


def hyena_filters(w1, b1, w2, b2, w3p, freq, ldp, fe, fo, seq):
    depth = w1.shape[0]
    half = seq // 2
    kc = half // HY_CHUNKS
    hw = HYENA_WIDTH
    bands = jnp.linspace(1e-4, HYENA_BANDS - 1, HYENA_BANDS, dtype=F32).reshape(1, HYENA_BANDS)
    full = lambda *shape: pl.BlockSpec((None,) + shape, lambda l, o, c: (l,) + (0,) * len(shape))
    out_sds = jax.ShapeDtypeStruct((depth, 2, 2, half, hw), F32)
    out_spec = pl.BlockSpec((None, None, 2, kc, hw), lambda l, o, c: (l, o, 0, c, 0))
    tab_spec = pl.BlockSpec((None, 2 * kc, half), lambda l, o, c: (c, 0, 0))
    hr, hi = pl.pallas_call(
        functools.partial(_hy_filter_kernel, seq=seq),
        out_shape=(out_sds, out_sds),
        grid=(depth, 2, HY_CHUNKS),
        in_specs=[
            full(*w1.shape[1:]), full(1, b1.shape[-1]), full(*w2.shape[1:]), full(1, b2.shape[-1]),
            pl.BlockSpec((None, w3p.shape[1], 2 * hw), lambda l, o, c: (l, 0, o)),
            full(*freq.shape[1:]),
            pl.BlockSpec((None, 1, 2 * hw), lambda l, o, c: (l, 0, o)),
            pl.BlockSpec((1, HYENA_BANDS), lambda l, o, c: (0, 0)),
            tab_spec, tab_spec,
        ],
        out_specs=(out_spec, out_spec),
        scratch_shapes=[pltpu.VMEM((4, half, hw), BF16), pltpu.VMEM((hw // 128, seq, 128), F32)],
        compiler_params=_cparams(("arbitrary", "arbitrary", "arbitrary")),
        name="hyena_filters",
    )(w1, b1[:, None, :], w2, b2[:, None, :], w3p, freq, ldp[:, None, :], bands, fe, fo)
    return hr.reshape(depth, 2, seq, hw), hi.reshape(depth, 2, seq, hw)


def dft_tables_r2(seq):
    half = seq // 2
    kc = half // HY_CHUNKS
    k = jnp.arange(half, dtype=jnp.int32)[:, None]
    n = jnp.arange(half, dtype=jnp.int32)[None, :]
    out = []
    for off in (0, 1):
        ang = (((2 * k + 1) * (2 * n + off)) % (4 * seq)).astype(F32) * (2.0 * math.pi / (4 * seq))
        c = jnp.cos(ang).reshape(HY_CHUNKS, kc, half)
        s = jnp.sin(ang).reshape(HY_CHUNKS, kc, half)
        out.append(jnp.concatenate([c, s], axis=1).astype(BF16))
        out.append((jnp.concatenate([c, -s], axis=1) * (1.0 / seq)).transpose(0, 2, 1).astype(BF16))
    return out[0], out[2], out[1], out[3]


def _hyena_kernel(hy_ref, cw_ref, cb_ref, bias_ref, hr_ref, hi_ref, hrp_ref, hip_ref, fe_ref, fo_ref, ge_ref, go_ref,
                  o_ref, zb_s, zf_s, acc_s, slab_s, *, seq):
    hw = HYENA_WIDTH
    half = seq // 2
    o = pl.program_id(1)
    c = pl.program_id(2)
    kc = half // HY_CHUNKS
    nslab = hw // 128

    def short_conv(part):
        x = hy_ref[:, part * hw:(part + 1) * hw].astype(F32)
        row = lax.broadcasted_iota(jnp.int32, (seq, 1), 0)
        xm = jnp.where(row == 0, 0.0, pltpu.roll(x, 1, 0))
        xp = jnp.where(row == seq - 1, 0.0, pltpu.roll(x, seq - 1, 0))
        w = cw_ref[:, part * hw:(part + 1) * hw]
        return cb_ref[:, part * hw:(part + 1) * hw] + xm * w[0:1, :] + x * w[1:2, :] + xp * w[2:3, :]

    def split(x):
        for sl in range(nslab):
            slab_s[sl] = x[:, sl * 128:(sl + 1) * 128]
        return tuple(jnp.concatenate([slab_s[sl, pl.ds(par, half, stride=2), :] for sl in range(nslab)], axis=-1)
                     for par in (0, 1))

    def interleave(xe, xo):
        for sl in range(nslab):
            slab_s[sl, pl.ds(0, half, stride=2), :] = xe[:, sl * 128:(sl + 1) * 128]
            slab_s[sl, pl.ds(1, half, stride=2), :] = xo[:, sl * 128:(sl + 1) * 128]
        return jnp.concatenate([slab_s[sl] for sl in range(nslab)], axis=-1)

    @pl.when((o == 0) & (c == 0))
    def _():
        for par, z in enumerate(split(short_conv(0))):
            zf_s[par] = z
            zb_s[par] = z.astype(BF16)

    ze = jnp.dot(fe_ref[...], zb_s[0], preferred_element_type=F32)
    zo = jnp.dot(fo_ref[...], zb_s[1], preferred_element_type=F32)
    zc, zs = ze[:kc] + zo[:kc], ze[kc:] + zo[kc:]
    zcp, zsp = ze[:kc] - zo[:kc], zo[kc:] - ze[kc:]
    hr, hi, hrp, hip = hr_ref[...], hi_ref[...], hrp_ref[...], hip_ref[...]
    yr, yi = zc * hr + zs * hi, zc * hi - zs * hr
    yrp, yip = zcp * hrp + zsp * hip, zcp * hip - zsp * hrp
    pe = jnp.dot(ge_ref[...], jnp.concatenate([yr + yrp, yi - yip], axis=0).astype(BF16),
                 preferred_element_type=F32)
    po = jnp.dot(go_ref[...], jnp.concatenate([yr - yrp, yi + yip], axis=0).astype(BF16),
                 preferred_element_type=F32)

    @pl.when(c == 0)
    def _():
        acc_s[0] = pe
        acc_s[1] = po

    @pl.when(c > 0)
    def _():
        acc_s[0] += pe
        acc_s[1] += po

    @pl.when((c == HY_CHUNKS - 1) & (o == 0))
    def _():
        for par, p1 in enumerate(split(short_conv(1))):
            z1 = p1 * (acc_s[par] + bias_ref[0:1, :] * zf_s[par])
            zf_s[par] = z1
            zb_s[par] = z1.astype(BF16)

    @pl.when((c == HY_CHUNKS - 1) & (o == 1))
    def _():
        halves = [p2 * (acc_s[par] + bias_ref[1:2, :] * zf_s[par]) for par, p2 in enumerate(split(short_conv(2)))]
        o_ref[...] = interleave(*halves).astype(o_ref.dtype)


def hyena_mixer(proj, conv_w, conv_b, hy_bias, hr, hi, tabs):
    b, s, n = proj.shape
    hw = HYENA_WIDTH
    half = s // 2
    kc = half // HY_CHUNKS
    nck = HY_CHUNKS
    spec = pl.BlockSpec((None, kc, hw), lambda bi, o, c: (o, c, 0))
    spec_p = pl.BlockSpec((None, kc, hw), lambda bi, o, c: (o, nck + c, 0))
    fwd_spec = pl.BlockSpec((None, 2 * kc, half), lambda bi, o, c: (c, 0, 0))
    inv_spec = pl.BlockSpec((None, half, 2 * kc), lambda bi, o, c: (c, 0, 0))
    fe, fo, ge, go = tabs
    return pl.pallas_call(
        functools.partial(_hyena_kernel, seq=s),
        out_shape=jax.ShapeDtypeStruct((b, s, hw), BF16),
        grid=(b, 2, nck),
        in_specs=[
            pl.BlockSpec((None, s, 3 * hw), lambda bi, o, c: (bi, 0, COL_HY // (3 * hw))),
            pl.BlockSpec((3, 3 * hw), lambda bi, o, c: (0, 0)),
            pl.BlockSpec((1, 3 * hw), lambda bi, o, c: (0, 0)),
            pl.BlockSpec((2, hw), lambda bi, o, c: (0, 0)),
            spec, spec, spec_p, spec_p, fwd_spec, fwd_spec, inv_spec, inv_spec,
        ],
        out_specs=pl.BlockSpec((None, s, hw), lambda bi, o, c: (bi, 0, 0)),
        scratch_shapes=[pltpu.VMEM((2, half, hw), BF16), pltpu.VMEM((2, half, hw), F32),
                        pltpu.VMEM((2, half, hw), F32), pltpu.VMEM((hw // 128, s, 128), F32)],
        compiler_params=_cparams(("parallel", "arbitrary", "arbitrary")),
        name="hyena_conv",
    )(proj, conv_w, conv_b.reshape(1, -1), hy_bias, hr, hi, hr, hi, fe, fo, ge, go)


def _layer_norm(y, g, b):
    mu = jnp.mean(y, axis=-1, keepdims=True)
    yc = y - mu
    var = jnp.mean(yc * yc, axis=-1, keepdims=True)
    return yc * lax.rsqrt(var + LN_EPS) * g + b


def _merge_kernel(*refs):
    att = refs[:12]
    (yb_ref, yc_ref, gl_ref, x_ref, g1_ref, sc2_ref, sh2_ref, lng_ref, lnb_ref, wa_ref, wb_ref, wc_ref, wo_ref,
     xo_ref, u2_ref) = refs[12:]
    d = D_MODEL
    halves = []
    for half in range(2):
        la, lb, lc = (att[4 * g + 2 + half][...] for g in range(3))
        m = jnp.maximum(jnp.maximum(la, lb), lc)
        ea, eb, ec = jnp.exp(la - m), jnp.exp(lb - m), jnp.exp(lc - m)
        inv = 1.0 / (ea + eb + ec)
        halves.append((ea * inv) * att[half][...] + (eb * inv) * att[4 + half][...] + (ec * inv) * att[8 + half][...])
    ya = jnp.concatenate(halves, axis=-1)
    za = jnp.dot(ya.astype(BF16), wa_ref[...], preferred_element_type=F32)
    zb = jnp.dot(yb_ref[...], wb_ref[...], preferred_element_type=F32)
    zc = jnp.dot(yc_ref[...], wc_ref[...], preferred_element_type=F32)
    merged = (jax.nn.sigmoid(gl_ref[:, 0:d].astype(F32)) * za
              + jax.nn.sigmoid(gl_ref[:, d:2 * d].astype(F32)) * zb
              + jax.nn.sigmoid(gl_ref[:, 2 * d:3 * d].astype(F32)) * zc)
    mix = jnp.dot(merged.astype(BF16), wo_ref[...], preferred_element_type=F32)
    xn = _layer_norm(ALPHA * x_ref[...] + g1_ref[...] * mix, lng_ref[...], lnb_ref[...])
    xo_ref[...] = xn
    u2_ref[...] = (xn * (1.0 + sc2_ref[...]) + sh2_ref[...]).T.astype(BF16)


def merge_mixers(att, yb, yc, proj, x, g1, sc2, sh2, ln_g, ln_b, wa, wb, wc, wo):
    b, s, d = x.shape
    tm = 512
    row = lambda w: pl.BlockSpec((None, tm, w), lambda bi, i: (bi, i, 0))
    per_b = pl.BlockSpec((None, 1, d), lambda bi, i: (bi, 0, 0))
    const = lambda r, c: pl.BlockSpec((r, c), lambda bi, i: (0, 0))
    return pl.pallas_call(
        _merge_kernel,
        out_shape=(jax.ShapeDtypeStruct((b, s, d), F32), jax.ShapeDtypeStruct((d, b * s), BF16)),
        grid=(b, s // tm),
        in_specs=[row(128)] * 12 + [row(512), row(512), row(3 * d), row(d), per_b, per_b, per_b,
                                    const(1, d), const(1, d), const(256, d), const(512, d), const(512, d),
                                    const(d, d)],
        out_specs=(row(d), pl.BlockSpec((d, tm), lambda bi, i: (0, bi * (s // tm) + i))),
        compiler_params=_cparams(("parallel", "parallel")),
        name="merge_mixers",
    )(*att, yb, yc, proj, x, g1, sc2, sh2, ln_g.reshape(1, d), ln_b.reshape(1, d), wa, wb, wc, wo)


_PEER_PAIRS = [(i, j) for i in range(1, PEER_TOPK + 1) for j in range(1, PEER_TOPK + 1) if i * j <= PEER_TOPK]
PEER_NOT_TOP = 100.0


def _bitonic_clean_desc(v):
    v = list(v)
    dist = len(v) // 2
    while dist >= 1:
        for a in range(len(v)):
            b = a ^ dist
            if b > a:
                v[a], v[b] = jnp.maximum(v[a], v[b]), jnp.minimum(v[a], v[b])
        dist //= 2
    return v


def _bitonic_sort_desc(v):
    v = list(v)
    k = 2
    while k <= len(v):
        dist = k // 2
        while dist >= 1:
            for a in range(len(v)):
                b = a ^ dist
                if b > a:
                    hi, lo = jnp.maximum(v[a], v[b]), jnp.minimum(v[a], v[b])
                    v[a], v[b] = (hi, lo) if (a & k) == 0 else (lo, hi)
            dist //= 2
        k *= 2
    return v


def _merge_top(v, w):
    n = len(v)
    return _bitonic_clean_desc([jnp.maximum(v[a], w[n - 1 - a]) for a in range(n)])


def _sorted_prefix_count(vals, pred):
    w = jnp.where
    t8 = pred(vals[7])
    t4 = pred(w(t8, vals[11], vals[3]))
    t2 = pred(w(t8, w(t4, vals[13], vals[9]), w(t4, vals[5], vals[1])))
    lo = w(t4, w(t2, vals[6], vals[4]), w(t2, vals[2], vals[0]))
    hi = w(t4, w(t2, vals[14], vals[12]), w(t2, vals[10], vals[8]))
    t1 = pred(w(t8, hi, lo))
    t16 = pred(vals[15])
    count = w(t8, 8.0, 0.0) + w(t4, 4.0, 0.0) + w(t2, 2.0, 0.0) + w(t1, 1.0, 0.0) + w(t16, 1.0, 0.0)
    return count, t16


def _peer_route_kernel(u_ref, wq_ref, keys_ref, e1_ref, cut_ref, e2_ref, r2_ref, q_s, s_s, ab_s, st_s, *, tr):
    nh, nk, topk = PEER_HEADS, PEER_KEYS, PEER_TOPK
    qt = jnp.dot(wq_ref[...], u_ref[...], preferred_element_type=F32)
    q_s[...] = qt.astype(BF16)
    for ph in range(2 * nh):
        s_s[ph] = jnp.dot(keys_ref[ph], q_s[ph * 128:(ph + 1) * 128, :], preferred_element_type=F32)

    def extract(h, carry):
        for p in range(2):
            sv = s_s[p * nh + h]
            v = _bitonic_sort_desc([sv[g * 8:(g + 1) * 8, :] for g in range(topk)])
            for shift in (4, 2, 1):
                v = _merge_top(v, [pltpu.roll(x, shift, 0) for x in v])
            for r in range(topk):
                ab_s[p, r, pl.ds(h, 1), :] = v[r][0:1, :]
        return carry

    lax.fori_loop(0, nh, extract, 0)

    for ch in range(tr // 128):
        ln = slice(ch * 128, (ch + 1) * 128)
        a = [ab_s[0, r, :, ln] for r in range(topk)]
        b = [ab_s[1, r, :, ln] for r in range(topk)]
        cand = [a[i - 1] + b[j - 1] for (i, j) in _PEER_PAIRS]
        rest = cand[topk:] + [jnp.full((nh, 128), -jnp.inf, F32)] * (4 * topk - len(cand))
        groups = [cand[:topk]] + [_bitonic_sort_desc(rest[g * topk:(g + 1) * topk]) for g in range(3)]
        m01 = _merge_top(groups[0], groups[1])
        m23 = _merge_top(groups[2], groups[3])
        tau = functools.reduce(jnp.minimum, [jnp.maximum(m01[r], m23[topk - 1 - r]) for r in range(topk)])
        top = cand[0]
        z = jnp.zeros((nh, 128), F32)
        for cx in cand:
            z = z + jnp.where(cx >= tau, jnp.exp(cx - top), 0.0)
        st_s[0, :, ln] = tau
        st_s[1, :, ln] = 1.0 / z

    for h in range(nh):
        rows = slice(h * nk, (h + 1) * nk)
        s1 = s_s[h]
        s2 = s_s[nh + h]
        hrow = slice(h, h + 1)
        tau = st_s[0, hrow, :]
        in1 = s1 >= ab_s[0, topk - 1, hrow, :]
        bs = [ab_s[1, r, hrow, :] for r in range(topk)]
        cut, _ = _sorted_prefix_count(bs, lambda v: s1 + v >= tau)
        above, below_all = _sorted_prefix_count(bs, lambda v: v > s2)
        e1_ref[rows, :] = jnp.where(in1, jnp.exp(s1 - ab_s[0, 0, hrow, :]), 0.0)
        cut_ref[rows, :] = jnp.where(in1, cut, 0.0)
        e2_ref[rows, :] = jnp.where(below_all, 0.0, jnp.exp(s2 - bs[0]) * st_s[1, hrow, :]).astype(BF16)
        r2_ref[rows, :] = jnp.where(below_all, PEER_NOT_TOP, above + 1.0).astype(BF16)


def peer_route(u2, wq_t, keys):
    d, t = u2.shape
    tr = 512
    nrow = PEER_HEADS * PEER_KEYS
    tab = lambda dt: jax.ShapeDtypeStruct((nrow, t), dt)
    col = pl.BlockSpec((nrow, tr), lambda i: (0, i))
    return pl.pallas_call(
        functools.partial(_peer_route_kernel, tr=tr),
        out_shape=(tab(F32), tab(F32), tab(BF16), tab(BF16)),
        grid=(t // tr,),
        in_specs=[
            pl.BlockSpec((d, tr), lambda i: (0, i)),
            pl.BlockSpec(wq_t.shape, lambda i: (0, 0)),
            pl.BlockSpec(keys.shape, lambda i: (0, 0, 0)),
        ],
        out_specs=(col, col, col, col),
        scratch_shapes=[
            pltpu.VMEM((2 * nrow, tr), BF16),
            pltpu.VMEM((2 * PEER_HEADS, PEER_KEYS, tr), F32),
            pltpu.VMEM((2, PEER_TOPK, PEER_HEADS, tr), F32),
            pltpu.VMEM((2, PEER_HEADS, tr), F32),
        ],
        compiler_params=_cparams(("parallel",)),
        name="peer_route",
    )(u2, wq_t, keys)


F8 = jnp.float8_e4m3fn
F8_MAX = 448.0
WG_SCALE = 16.0


def _peer_dense_kernel(sc_ref, x_ref, u_ref, vt_ref, e1_ref, cut_ref, e2_ref, r2_ref, xres_ref, g2_ref, lng_ref,
                       lnb_ref, o_ref, acc_s, wga_s, wgb_s, *, te, n_e):
    j = pl.program_id(1)
    nk = PEER_KEYS
    tn = x_ref.shape[1]
    sub = 16

    @pl.when(j == 0)
    def _():
        acc_s[...] = jnp.zeros_like(acc_s)
        wgb_s[...] = jnp.zeros_like(wgb_s)

    def step(w_cur, w_prev):
        inv_a = sc_ref[0]
        ts = 512
        c_lin = jnp.full((1, ts), 0.5 * WG_SCALE * inv_a, F32).astype(BF16)
        c_erf = jnp.full((1, ts), (2.0 ** -0.5) * inv_a, F32).astype(BF16)

        def row_bcast(ref, h, ii, lanes):
            chunks = [jnp.broadcast_to(ref[h, ii:ii + 1, c0:c0 + 128], (sub, 128)).astype(BF16)
                      for c0 in range(lanes.start, lanes.stop, 128)]
            return jnp.concatenate(chunks, axis=-1)[None]

        for sl in range(tn // ts):
            lanes = slice(sl * ts, (sl + 1) * ts)
            for ii in range(te // nk):
                rows = slice(ii * nk, (ii + 1) * nk)
                at = jnp.dot(u_ref[rows, :], x_ref[:, lanes], preferred_element_type=F32).astype(BF16)
                if ii == 1:
                    acc_s[:, lanes] += jnp.dot(vt_ref[...], w_prev[:, lanes], preferred_element_type=F32)
                g = (c_lin * at) * (1.0 + lax.erf(at * c_erf))
                w = None
                for h in range(PEER_HEADS):
                    cb = row_bcast(cut_ref, h, ii, lanes)
                    e1 = row_bcast(e1_ref, h, ii, lanes)
                    r2 = r2_ref[h * nk:(h + 1) * nk, lanes].reshape(nk // sub, sub, ts)
                    e2 = e2_ref[h * nk:(h + 1) * nk, lanes].reshape(nk // sub, sub, ts)
                    sel = jnp.where(r2 <= cb, e2, jnp.zeros((), BF16)) * e1
                    w = sel if w is None else w + sel
                wg = jnp.clip(g * w.reshape(nk, ts), -F8_MAX, F8_MAX)
                w_cur[rows, lanes] = wg.astype(F8)

    @pl.when((j % 2 == 0) & (j < n_e))
    def _():
        step(wga_s, wgb_s)

    @pl.when(j % 2 == 1)
    def _():
        step(wgb_s, wga_s)

    @pl.when(j == n_e)
    def _():
        acc = acc_s[...] + jnp.dot(vt_ref[...], wgb_s[...], preferred_element_type=F32)
        ffn = (acc * sc_ref[1]).T
        o_ref[...] = _layer_norm(ALPHA * xres_ref[...] + g2_ref[...] * ffn, lng_ref[...], lnb_ref[...])


def peer_dense(scales, x8, u_tab, vt_tab, e1, cut, e2, r2, x_res, g2, ln_g, ln_b):
    d, t = x8.shape
    b, s, _ = x_res.shape
    ne = u_tab.shape[0]
    tn, te = 1024, 1024
    n_e = ne // te
    assert n_e % 2 == 0, "the flush step reads the buffer written by the odd step n_e - 1"
    nrow = PEER_HEADS * PEER_KEYS
    col = pl.BlockSpec((nrow, tn), lambda i, j: (0, i))
    step_rows = pl.BlockSpec((PEER_HEADS, te // PEER_KEYS, tn), lambda i, j: (0, jnp.minimum(j, n_e - 1), i))
    e1 = e1.reshape(PEER_HEADS, PEER_KEYS, t)
    cut = cut.reshape(PEER_HEADS, PEER_KEYS, t)
    nts = s // tn
    row = pl.BlockSpec((None, tn, d), lambda i, j: (i // nts, i % nts, 0))

    def res_index(i, j):
        ii = jnp.where(j == 0, jnp.maximum(i - 1, 0), i)
        return ii // nts, ii % nts, 0

    const = pl.BlockSpec((1, d), lambda i, j: (0, 0))
    return pl.pallas_call(
        functools.partial(_peer_dense_kernel, te=te, n_e=n_e),
        out_shape=jax.ShapeDtypeStruct((b, s, d), F32),
        grid=(t // tn, n_e + 1),
        in_specs=[
            pl.BlockSpec(memory_space=pltpu.SMEM),
            pl.BlockSpec((d, tn), lambda i, j: (0, i)),
            pl.BlockSpec((te, d), lambda i, j: (jnp.minimum(j, n_e - 1), 0)),
            pl.BlockSpec((d, te), lambda i, j: (0, jnp.maximum(j - 1, 0))),
            step_rows, step_rows, col, col,
            pl.BlockSpec((None, tn, d), res_index), pl.BlockSpec((None, 1, d), lambda i, j: (i // nts, 0, 0)),
            const, const,
        ],
        out_specs=row,
        scratch_shapes=[pltpu.VMEM((d, tn), F32), pltpu.VMEM((te, tn), F8), pltpu.VMEM((te, tn), F8)],
        compiler_params=_cparams(("parallel", "arbitrary")),
        name="peer_dense",
    )(scales, x8, u_tab, vt_tab, e1, cut, e2, r2, x_res, g2, ln_g.reshape(1, d), ln_b.reshape(1, d))


def rope_tables(seq):
    pos = jnp.arange(seq, dtype=F32)
    inv = ROPE_THETA ** (-jnp.arange(0, HEAD_DIM, 2, dtype=F32) / HEAD_DIM)
    ang = pos[:, None] * inv[None, :]
    sign = jnp.where((jnp.arange(128) % 64) < 32, -1.0, 1.0).astype(F32)
    return jnp.tile(jnp.cos(ang), (1, 4)), jnp.tile(jnp.sin(ang), (1, 4)) * sign[None, :]


def _swa_head_perm():
    heads = [h for p in range(4) for h in (p, 4 + p)]
    return jnp.concatenate([jnp.arange(h * HEAD_DIM, (h + 1) * HEAD_DIM) for h in heads])


def relayout_w_in(w):
    qa, ka, va = w[:, 0:768], w[:, 768:1536], w[:, 1536:2304]
    hy = w[:, 2304:3840]
    qc, kvc = w[:, 3840:4352], w[:, 4352:4608]
    gates = w[:, 4608:7680]
    dil = [jnp.concatenate([t[:, 256 * g:256 * (g + 1)] for t in (qa, ka, va)], axis=1) for g in range(3)]
    main = jnp.concatenate([gates, hy, dil[0], qc[:, _swa_head_perm()], kvc], axis=1).astype(BF16)
    return main, jnp.concatenate(dil[1:], axis=1).astype(BF16)


def residue_major(tab, dil):
    s = tab.shape[0]
    return tab.reshape(s // dil, dil, 128).transpose(1, 0, 2).reshape(s, 128)


def token_mixer(x, ada, w_main, w_dil, conv_w, conv_b, hy_bias, attn_sink, hr, hi, dft_tabs, rope_tabs,
                wa, wb, wc, wo, ln_g, ln_b):
    b, s, _ = x.shape
    sh1, sc1, g1, sh2, sc2, _ = [a[:, None, :] for a in jnp.split(ada, 6, axis=-1)]
    proj = in_proj(x, sc1, sh1, w_main)
    qkv1, qkv2 = in_proj_dil(x, sc1, sh1, w_dil)
    att = (dilated_attention(proj.reshape(b, 1, s, MAIN_WIDTH), COL_DIL0 // QKV_W, *rope_tabs[0])
           + dilated_attention(qkv1, 0, *rope_tabs[1]) + dilated_attention(qkv2, 0, *rope_tabs[2]))
    yb = hyena_mixer(proj, conv_w, conv_b, hy_bias, hr, hi, dft_tabs)
    yc = swa_attention(proj, *rope_tabs[0], attn_sink)
    return merge_mixers(att, yb, yc, proj, x, g1, sc2, sh2, ln_g, ln_b, wa, wb, wc, wo)


def peer_sublayer(u2, wq, keys, u_tab, v_tab, x_res, g2, ln_g, ln_b):
    d = u2.shape[0]
    nh, nk = PEER_HEADS, PEER_KEYS
    wq_t = wq.reshape(d, nh, 2, nk).transpose(2, 1, 3, 0).reshape(2 * nh * nk, d).astype(BF16)
    keys_r = keys.transpose(1, 0, 2, 3).reshape(2 * nh, nk, keys.shape[-1]).astype(BF16)
    tabs = peer_route(u2, wq_t, keys_r)
    su, sv, sx = _pow2_scale(u_tab), _pow2_scale(v_tab), _pow2_scale(u2)
    scales = jnp.stack([1.0 / (su * sx), 1.0 / (sv * WG_SCALE)]).astype(F32)
    x8 = (u2.astype(F32) * sx).astype(F8)
    return peer_dense(scales, x8, (u_tab * su).astype(F8), (v_tab.T * sv).astype(F8), *tabs, x_res, g2, ln_g, ln_b)


def _pow2_scale(a):
    m = jnp.maximum(jnp.max(jnp.abs(a)).astype(F32), 2.0 ** -100)
    return jnp.exp2(jnp.floor(jnp.log2(F8_MAX / m)))


def kernel(x, c, w_ada, b_ada, w_in, conv_w, conv_b, hy_w1, hy_b1, hy_w2, hy_b2, hy_w3, hy_freq, hy_log_decay,
           hy_bias, attn_sink, w_branch_a, w_branch_b, w_branch_c, w_out, ln_g, ln_b, peer_wq, peer_keys, peer_u,
           peer_v):
    b, s, d = x.shape
    depth = w_in.shape[0]
    hw = HYENA_WIDTH
    cos_t, sin_t = rope_tables(s)
    dft_tabs = dft_tables_r2(s)
    ada = ada_ln(c, w_ada, b_ada)
    w3p = hy_w3.reshape(depth, -1, 2, 2, hw).transpose(0, 1, 3, 2, 4).reshape(depth, -1, 4 * hw)
    ldp = hy_log_decay.reshape(depth, 2, 2, hw).transpose(0, 2, 1, 3).reshape(depth, 4 * hw)
    hr, hi = hyena_filters(hy_w1, hy_b1, hy_w2, hy_b2, w3p, hy_freq, ldp, dft_tabs[0], dft_tabs[1], s)
    perm = _swa_head_perm()
    rope_tabs = [(residue_major(cos_t, dil), residue_major(sin_t, dil)) for _, dil in DIL_GROUPS]
    for l in range(depth):
        x1, u2 = token_mixer(x, ada[l], *relayout_w_in(w_in[l]), conv_w[l], conv_b[l], hy_bias[l], attn_sink[l],
                             hr[l], hi[l], dft_tabs, rope_tabs, w_branch_a[l].astype(BF16),
                             w_branch_b[l].astype(BF16), w_branch_c[l][perm].astype(BF16), w_out[l].astype(BF16),
                             ln_g[l, 0], ln_b[l, 0])
        x = peer_sublayer(u2, peer_wq[l], peer_keys[l], peer_u[l], peer_v[l], x1, ada[l][:, None, 5 * d:6 * d],
                          ln_g[l, 1], ln_b[l, 1])
    return x
```

```python
import functools
import math

import jax
import jax.numpy as jnp
from jax import lax
from jax.experimental import pallas as pl
from jax.experimental.pallas import tpu as pltpu

F32 = jnp.float32
BF16 = jnp.bfloat16

D_MODEL = 1024
HEAD_DIM = 64
DIL_GROUPS = ((128, 1), (512, 4), (2048, 16))
DIL_RADIUS = 64
HYENA_WIDTH = 512
HYENA_BANDS = 16
SWA_RADIUS = 128
SWA_Q_HEADS = 8
SWA_KV_HEADS = 2
PEER_HEADS = 8
PEER_KEYS = 128
PEER_TOPK = 16
DEPTH = 2
ALPHA = (2 * DEPTH) ** 0.25
LN_EPS = 1e-5
NEG_INF = -1e30
ROPE_THETA = 10000.0

COL_GATES = 0
COL_HY = 3072
COL_DIL0 = 4608
COL_SWA = 5376
MAIN_WIDTH = 6144
QKV_W = 768

V7X_VMEM_BYTES = 64 * 1024 * 1024
VMEM_LIMIT = V7X_VMEM_BYTES - 8 * 1024 * 1024


def _cparams(sem, vmem=VMEM_LIMIT):
    return pltpu.CompilerParams(dimension_semantics=sem, vmem_limit_bytes=vmem)


def _ada_kernel(c_ref, w_ref, b_ref, o_ref):
    c = c_ref[...]
    cond = (c * jax.nn.sigmoid(c)).astype(BF16)
    o_ref[...] = jnp.dot(cond, w_ref[...].astype(BF16), preferred_element_type=F32) + b_ref[...]


def ada_ln(c, w_ada, b_ada):
    depth, d, n = w_ada.shape
    b = c.shape[0]
    tn = 1536
    return pl.pallas_call(
        _ada_kernel,
        out_shape=jax.ShapeDtypeStruct((depth, b, n), F32),
        grid=(depth, n // tn),
        in_specs=[
            pl.BlockSpec((b, d), lambda l, j: (0, 0)),
            pl.BlockSpec((None, d, tn), lambda l, j: (l, 0, j)),
            pl.BlockSpec((None, 1, tn), lambda l, j: (l, 0, j)),
        ],
        out_specs=pl.BlockSpec((None, b, tn), lambda l, j: (l, 0, j)),
        compiler_params=_cparams(("arbitrary", "arbitrary")),
        name="ada_ln",
    )(c, w_ada, b_ada.reshape(depth, 1, n))


def _inproj_kernel(x_ref, sc_ref, sh_ref, w_ref, o_ref, u_ref):
    @pl.when(pl.program_id(2) == 0)
    def _():
        u_ref[...] = (x_ref[...] * (1.0 + sc_ref[...]) + sh_ref[...]).astype(BF16)

    o_ref[...] = jnp.dot(u_ref[...], w_ref[...], preferred_element_type=F32).astype(o_ref.dtype)


def in_proj(x, sc, sh, w):
    b, s, d = x.shape
    n = w.shape[1]
    tm, tn = 1024, 1536
    return pl.pallas_call(
        _inproj_kernel,
        out_shape=jax.ShapeDtypeStruct((b, s, n), BF16),
        grid=(b, s // tm, n // tn),
        in_specs=[
            pl.BlockSpec((None, tm, d), lambda bi, i, j: (bi, i, 0)),
            pl.BlockSpec((None, 1, d), lambda bi, i, j: (bi, 0, 0)),
            pl.BlockSpec((None, 1, d), lambda bi, i, j: (bi, 0, 0)),
            pl.BlockSpec((d, tn), lambda bi, i, j: (0, j)),
        ],
        out_specs=pl.BlockSpec((None, tm, tn), lambda bi, i, j: (bi, i, j)),
        scratch_shapes=[pltpu.VMEM((tm, d), BF16)],
        compiler_params=_cparams(("parallel", "parallel", "arbitrary")),
        name="in_proj",
    )(x, sc, sh, w)


def _inproj_dil_kernel(x_ref, sc_ref, sh_ref, w_ref, o1_ref, o2_ref, res_s):
    tm = x_ref.shape[0]
    u = (x_ref[...] * (1.0 + sc_ref[...]) + sh_ref[...]).astype(BF16)
    res = jnp.dot(u, w_ref[...], preferred_element_type=F32)
    nslab = QKV_W // 128
    for c in range(2 * nslab):
        res_s[c] = res[:, c * 128:(c + 1) * 128]
    for g, o_ref in ((0, o1_ref), (1, o2_ref)):
        dil = o_ref.shape[0]
        for r in range(dil):
            for c in range(nslab):
                o_ref[r, :, c * 128:(c + 1) * 128] = res_s[g * nslab + c, pl.ds(r, tm // dil, stride=dil), :].astype(
                    o_ref.dtype)


def in_proj_dil(x, sc, sh, w):
    b, s, d = x.shape
    tm = 1024
    d1, d2 = DIL_GROUPS[1][1], DIL_GROUPS[2][1]
    return pl.pallas_call(
        _inproj_dil_kernel,
        out_shape=(jax.ShapeDtypeStruct((b, d1, s // d1, QKV_W), BF16),
                   jax.ShapeDtypeStruct((b, d2, s // d2, QKV_W), BF16)),
        grid=(b, s // tm),
        in_specs=[
            pl.BlockSpec((None, tm, d), lambda bi, i: (bi, i, 0)),
            pl.BlockSpec((None, 1, d), lambda bi, i: (bi, 0, 0)),
            pl.BlockSpec((None, 1, d), lambda bi, i: (bi, 0, 0)),
            pl.BlockSpec((d, 2 * QKV_W), lambda bi, i: (0, 0)),
        ],
        out_specs=(
            pl.BlockSpec((None, d1, tm // d1, QKV_W), lambda bi, i: (bi, 0, i, 0)),
            pl.BlockSpec((None, d2, tm // d2, QKV_W), lambda bi, i: (bi, 0, i, 0)),
        ),
        scratch_shapes=[pltpu.VMEM((2 * QKV_W // 128, tm, 128), F32)],
        compiler_params=_cparams(("parallel", "parallel")),
        name="in_proj_dil",
    )(x, sc, sh, w)


def _rope2(x, cos, sin_signed, lo):
    xr = jnp.where(lo, pltpu.roll(x, 96, 1), pltpu.roll(x, 32, 1))
    return x * cos + xr * sin_signed


def _lane_masks():
    lane = lax.broadcasted_iota(jnp.int32, (1, 128), 1)
    return (lane % 64) < 32, lane < 64


def _band_mask(q0, nq, nk, pad, radius, length):
    qpos = q0 + (lax.broadcasted_iota(jnp.int32, (nq, 1), 0) & 127)
    kpos = q0 - pad + lax.broadcasted_iota(jnp.int32, (1, nk), 1)
    kpos = jnp.where(kpos < 0, -4 * length, jnp.where(kpos >= length, -4 * length, kpos))
    return jnp.abs(qpos - kpos) <= radius


def _dil_attn_kernel(qkv_ref, cos_ref, sin_ref, olo_ref, ohi_ref, llo_ref, lhi_ref, q_s, k_s, v_s, *, dil, ls):
    lo, head0 = _lane_masks()
    heads = (head0, jnp.logical_not(head0))
    cos = cos_ref[...]
    sin = sin_ref[...]
    pad = DIL_RADIUS
    qb = 128
    nblk = ls // qb
    ngroup = 8
    zeros = jnp.zeros((dil, pad, 128), BF16)
    k_s[:, 0:pad, :] = zeros
    k_s[:, pad + ls:pad + ls + pad, :] = zeros
    v_s[:, 0:pad, :] = zeros
    v_s[:, pad + ls:pad + ls + pad, :] = zeros
    for hp, (o_ref, l_ref) in enumerate(((olo_ref, llo_ref), (ohi_ref, lhi_ref))):
        q = qkv_ref[:, :, hp * 128:(hp + 1) * 128].astype(F32).reshape(dil * ls, 128)
        k = qkv_ref[:, :, 256 + hp * 128:256 + (hp + 1) * 128].astype(F32).reshape(dil * ls, 128)
        q_s[...] = (_rope2(q, cos, sin, lo) * (HEAD_DIM ** -0.5)).astype(BF16)
        k_s[:, pad:pad + ls, :] = _rope2(k, cos, sin, lo).astype(BF16).reshape(dil, ls, 128)
        v_s[:, pad:pad + ls, :] = qkv_ref[:, :, 512 + hp * 128:512 + (hp + 1) * 128]

        def grp(gi, carry):
            tiles = []
            for t in range(ngroup):
                c = gi * ngroup + t
                r = c // nblk
                q0 = pl.multiple_of((c % nblk) * qb, qb)
                qblk = q_s[pl.ds(pl.multiple_of(c * qb, qb), qb), :]
                kw = k_s[r, pl.ds(q0, qb + 2 * pad), :]
                vw = v_s[r, pl.ds(q0, qb + 2 * pad), :]
                mask = _band_mask(q0, qb, qb + 2 * pad, pad, DIL_RADIUS, ls)
                ss = [lax.dot_general(jnp.where(hm, qblk, jnp.zeros_like(qblk)), kw, (((1,), (1,)), ((), ())),
                                      preferred_element_type=F32) for hm in heads]
                tiles.append((r, q0, vw, mask, ss))
            soft = []
            for r, q0, vw, mask, ss in tiles:
                for s in ss:
                    s = jnp.where(mask, s, NEG_INF)
                    m = jnp.max(s, axis=-1, keepdims=True)
                    p = jnp.exp(s - m)
                    soft.append((p.astype(BF16), jnp.sum(p, axis=-1, keepdims=True), m))
            for ti, (r, q0, vw, mask, ss) in enumerate(tiles):
                res = []
                for h in range(2):
                    p, den, m = soft[2 * ti + h]
                    res.append((jnp.dot(p, vw, preferred_element_type=F32) / den, m + jnp.log(den)))
                o = jnp.where(head0, res[0][0], res[1][0])
                lse = jnp.where(head0, res[0][1], res[1][1])
                rows = pl.ds(q0, qb) if dil == 1 else pl.ds(q0 * dil + r, qb, stride=dil)
                o_ref[rows, :] = o
                l_ref[rows, :] = lse
            return carry

        lax.fori_loop(0, dil * nblk // ngroup, grp, 0)


def dilated_attention(qkv, col_block, cos_d, sin_d):
    b, dil, ls, _ = qkv.shape
    s = dil * ls
    out_sds = jax.ShapeDtypeStruct((b, s, 128), F32)
    out_spec = pl.BlockSpec((None, s, 128), lambda bi: (bi, 0, 0))
    return pl.pallas_call(
        functools.partial(_dil_attn_kernel, dil=dil, ls=ls),
        out_shape=(out_sds,) * 4,
        grid=(b,),
        in_specs=[
            pl.BlockSpec((None, dil, ls, QKV_W), lambda bi: (bi, 0, 0, col_block)),
            pl.BlockSpec((s, 128), lambda bi: (0, 0)),
            pl.BlockSpec((s, 128), lambda bi: (0, 0)),
        ],
        out_specs=(out_spec,) * 4,
        scratch_shapes=[
            pltpu.VMEM((s, 128), BF16),
            pltpu.VMEM((dil, ls + 2 * DIL_RADIUS, 128), BF16),
            pltpu.VMEM((dil, ls + 2 * DIL_RADIUS, 128), BF16),
        ],
        compiler_params=_cparams(("parallel",)),
        name=f"dil_attn_d{dil}",
    )(qkv, cos_d, sin_d)


def _swa_kernel(qkv_ref, cos_ref, sin_ref, sink_ref, o_ref, q_s, k_s, v_s, *, s_len):
    lo, head0 = _lane_masks()
    heads = (head0, jnp.logical_not(head0))
    cos = cos_ref[...]
    sin = sin_ref[...]
    pad = SWA_RADIUS
    qb = 128
    grp = SWA_Q_HEADS // SWA_KV_HEADS
    zeros = jnp.zeros((pad, 128), BF16)
    k_s[0:pad, :] = zeros
    k_s[pad + s_len:pad + s_len + pad, :] = zeros
    v_s[0:pad, :] = zeros
    v_s[pad + s_len:pad + s_len + pad, :] = zeros
    k = qkv_ref[:, 512:640].astype(F32)
    k_s[pad:pad + s_len, :] = _rope2(k, cos, sin, lo).astype(BF16)
    v_s[pad:pad + s_len, :] = qkv_ref[:, 640:768]
    for qp in range(grp):
        q = qkv_ref[:, qp * 128:(qp + 1) * 128].astype(F32)
        q_s[qp] = (_rope2(q, cos, sin, lo) * (HEAD_DIM ** -0.5)).astype(BF16)
    sinks = [jnp.concatenate([jnp.broadcast_to(sink_ref[kv * grp + p:kv * grp + p + 1, 0:1], (qb, 1))
                              for p in range(grp)], axis=0) for kv in range(SWA_KV_HEADS)]

    def blk(i, carry):
        q0 = pl.multiple_of(i * qb, qb)
        kw = k_s[pl.ds(q0, qb + 2 * pad), :]
        vw = v_s[pl.ds(q0, qb + 2 * pad), :]
        qall = jnp.concatenate([q_s[p, pl.ds(q0, qb), :] for p in range(grp)], axis=0)
        mask = _band_mask(q0, grp * qb, qb + 2 * pad, pad, SWA_RADIUS, s_len)
        ss = [lax.dot_general(jnp.where(hm, qall, jnp.zeros_like(qall)), kw, (((1,), (1,)), ((), ())),
                              preferred_element_type=F32) for hm in heads]
        soft = []
        for s, sk in zip(ss, sinks):
            s = jnp.where(mask, s, NEG_INF)
            m = jnp.maximum(jnp.max(s, axis=-1, keepdims=True), sk)
            p = jnp.exp(s - m)
            soft.append((p.astype(BF16), jnp.sum(p, axis=-1, keepdims=True) + jnp.exp(sk - m)))
        res = [jnp.dot(p, vw, preferred_element_type=F32) / den for p, den in soft]
        o = jnp.where(head0, res[0], res[1]).astype(o_ref.dtype)
        for p in range(grp):
            o_ref[pl.ds(q0, qb), p * 128:(p + 1) * 128] = o[p * qb:(p + 1) * qb, :]
        return carry

    lax.fori_loop(0, s_len // qb, blk, 0, unroll=16)


def swa_attention(proj, cos_t, sin_t, sink):
    b, s, n = proj.shape
    return pl.pallas_call(
        functools.partial(_swa_kernel, s_len=s),
        out_shape=jax.ShapeDtypeStruct((b, s, 512), BF16),
        grid=(b,),
        in_specs=[
            pl.BlockSpec((None, s, QKV_W), lambda bi: (bi, 0, COL_SWA // QKV_W)),
            pl.BlockSpec((s, 128), lambda bi: (0, 0)),
            pl.BlockSpec((s, 128), lambda bi: (0, 0)),
            pl.BlockSpec((8, 128), lambda bi: (0, 0)),
        ],
        out_specs=pl.BlockSpec((None, s, 512), lambda bi: (bi, 0, 0)),
        scratch_shapes=[
            pltpu.VMEM((SWA_Q_HEADS // SWA_KV_HEADS, s, 128), BF16),
            pltpu.VMEM((s + 2 * SWA_RADIUS, 128), BF16),
            pltpu.VMEM((s + 2 * SWA_RADIUS, 128), BF16),
        ],
        compiler_params=_cparams(("parallel",)),
        name="swa_attn",
    )(proj, cos_t, sin_t, jnp.broadcast_to(sink.astype(F32)[:, None], (8, 128)))


HY_CHUNKS = 4


def _hy_filter_kernel(w1_ref, b1_ref, w2_ref, b2_ref, w3_ref, fr_ref, ld_ref, bands_ref, fe_ref, fo_ref,
                      hr_ref, hi_ref, hh_s, slab_s, *, seq):
    hw = HYENA_WIDTH
    hi_p = lax.Precision.HIGHEST

    @pl.when(pl.program_id(2) == 0)
    def _():
        idx = lax.broadcasted_iota(jnp.int32, (seq, 1), 0).astype(F32)
        t = idx / max(seq - 1, 1)
        w = 2.0 * math.pi * idx / seq
        ang = w * bands_ref[...]
        w1 = w1_ref[...]
        pre = (t * w1[0:1, :]
               + jnp.dot(jnp.cos(ang), w1[1:1 + HYENA_BANDS, :], precision=hi_p, preferred_element_type=F32)
               - jnp.dot(jnp.sin(ang), w1[1 + HYENA_BANDS:1 + 2 * HYENA_BANDS, :], precision=hi_p,
                         preferred_element_type=F32)
               + b1_ref[...])
        h = jnp.sin(fr_ref[0:1, :] * pre)
        h = jnp.sin(fr_ref[1:2, :] * (jnp.dot(h, w2_ref[...], precision=hi_p, preferred_element_type=F32)
                                      + b2_ref[...]))
        h = jnp.dot(h, w3_ref[...], precision=hi_p, preferred_element_type=F32)
        h = h * jnp.exp(-t * jnp.exp(ld_ref[...]))
        hf = h[:, :hw]
        hb = jnp.where(idx > 0.0, h[:, hw:], 0.0)
        inv = lax.rsqrt(jnp.sum(hf * hf + hb * hb, axis=0, keepdims=True) + 1e-12)
        for q, comb in enumerate(((hf + hb) * inv, (hf - hb) * inv)):
            for sl in range(hw // 128):
                slab_s[sl] = comb[:, sl * 128:(sl + 1) * 128]
            for par in range(2):
                hh_s[2 * q + par] = jnp.concatenate(
                    [slab_s[sl, pl.ds(par, seq // 2, stride=2), :] for sl in range(hw // 128)], axis=-1).astype(BF16)

    kc = seq // 2 // HY_CHUNKS
    ec = jnp.dot(fe_ref[0:kc, :], hh_s[0], preferred_element_type=F32)
    oc = jnp.dot(fo_ref[0:kc, :], hh_s[1], preferred_element_type=F32)
    es = jnp.dot(fe_ref[kc:2 * kc, :], hh_s[2], preferred_element_type=F32)
    os_ = jnp.dot(fo_ref[kc:2 * kc, :], hh_s[3], preferred_element_type=F32)
    hr_ref[0] = ec + oc
    hr_ref[1] = ec - oc
    hi_ref[0] = -(es + os_)
    hi_ref[1] = es - os_


def hyena_filters(w1, b1, w2, b2, w3p, freq, ldp, fe, fo, seq):
    depth = w1.shape[0]
    half = seq // 2
    kc = half // HY_CHUNKS
    hw = HYENA_WIDTH
    bands = jnp.linspace(1e-4, HYENA_BANDS - 1, HYENA_BANDS, dtype=F32).reshape(1, HYENA_BANDS)
    full = lambda *shape: pl.BlockSpec((None,) + shape, lambda l, o, c: (l,) + (0,) * len(shape))
    out_sds = jax.ShapeDtypeStruct((depth, 2, 2, half, hw), F32)
    out_spec = pl.BlockSpec((None, None, 2, kc, hw), lambda l, o, c: (l, o, 0, c, 0))
    tab_spec = pl.BlockSpec((None, 2 * kc, half), lambda l, o, c: (c, 0, 0))
    hr, hi = pl.pallas_call(
        functools.partial(_hy_filter_kernel, seq=seq),
        out_shape=(out_sds, out_sds),
        grid=(depth, 2, HY_CHUNKS),
        in_specs=[
            full(*w1.shape[1:]), full(1, b1.shape[-1]), full(*w2.shape[1:]), full(1, b2.shape[-1]),
            pl.BlockSpec((None, w3p.shape[1], 2 * hw), lambda l, o, c: (l, 0, o)),
            full(*freq.shape[1:]),
            pl.BlockSpec((None, 1, 2 * hw), lambda l, o, c: (l, 0, o)),
            pl.BlockSpec((1, HYENA_BANDS), lambda l, o, c: (0, 0)),
            tab_spec, tab_spec,
        ],
        out_specs=(out_spec, out_spec),
        scratch_shapes=[pltpu.VMEM((4, half, hw), BF16), pltpu.VMEM((hw // 128, seq, 128), F32)],
        compiler_params=_cparams(("arbitrary", "arbitrary", "arbitrary")),
        name="hyena_filters",
    )(w1, b1[:, None, :], w2, b2[:, None, :], w3p, freq, ldp[:, None, :], bands, fe, fo)
    return hr.reshape(depth, 2, seq, hw), hi.reshape(depth, 2, seq, hw)


def dft_tables_r2(seq):
    half = seq // 2
    kc = half // HY_CHUNKS
    k = jnp.arange(half, dtype=jnp.int32)[:, None]
    n = jnp.arange(half, dtype=jnp.int32)[None, :]
    out = []
    for off in (0, 1):
        ang = (((2 * k + 1) * (2 * n + off)) % (4 * seq)).astype(F32) * (2.0 * math.pi / (4 * seq))
        c = jnp.cos(ang).reshape(HY_CHUNKS, kc, half)
        s = jnp.sin(ang).reshape(HY_CHUNKS, kc, half)
        out.append(jnp.concatenate([c, s], axis=1).astype(BF16))
        out.append((jnp.concatenate([c, -s], axis=1) * (1.0 / seq)).transpose(0, 2, 1).astype(BF16))
    return out[0], out[2], out[1], out[3]


def _hyena_kernel(hy_ref, cw_ref, cb_ref, bias_ref, hr_ref, hi_ref, hrp_ref, hip_ref, fe_ref, fo_ref, ge_ref, go_ref,
                  o_ref, zb_s, zf_s, acc_s, slab_s, *, seq):
    hw = HYENA_WIDTH
    half = seq // 2
    o = pl.program_id(1)
    c = pl.program_id(2)
    kc = half // HY_CHUNKS
    nslab = hw // 128

    def short_conv(part):
        x = hy_ref[:, part * hw:(part + 1) * hw].astype(F32)
        row = lax.broadcasted_iota(jnp.int32, (seq, 1), 0)
        xm = jnp.where(row == 0, 0.0, pltpu.roll(x, 1, 0))
        xp = jnp.where(row == seq - 1, 0.0, pltpu.roll(x, seq - 1, 0))
        w = cw_ref[:, part * hw:(part + 1) * hw]
        return cb_ref[:, part * hw:(part + 1) * hw] + xm * w[0:1, :] + x * w[1:2, :] + xp * w[2:3, :]

    def split(x):
        for sl in range(nslab):
            slab_s[sl] = x[:, sl * 128:(sl + 1) * 128]
        return tuple(jnp.concatenate([slab_s[sl, pl.ds(par, half, stride=2), :] for sl in range(nslab)], axis=-1)
                     for par in (0, 1))

    def interleave(xe, xo):
        for sl in range(nslab):
            slab_s[sl, pl.ds(0, half, stride=2), :] = xe[:, sl * 128:(sl + 1) * 128]
            slab_s[sl, pl.ds(1, half, stride=2), :] = xo[:, sl * 128:(sl + 1) * 128]
        return jnp.concatenate([slab_s[sl] for sl in range(nslab)], axis=-1)

    @pl.when((o == 0) & (c == 0))
    def _():
        for par, z in enumerate(split(short_conv(0))):
            zf_s[par] = z
            zb_s[par] = z.astype(BF16)

    ze = jnp.dot(fe_ref[...], zb_s[0], preferred_element_type=F32)
    zo = jnp.dot(fo_ref[...], zb_s[1], preferred_element_type=F32)
    zc, zs = ze[:kc] + zo[:kc], ze[kc:] + zo[kc:]
    zcp, zsp = ze[:kc] - zo[:kc], zo[kc:] - ze[kc:]
    hr, hi, hrp, hip = hr_ref[...], hi_ref[...], hrp_ref[...], hip_ref[...]
    yr, yi = zc * hr + zs * hi, zc * hi - zs * hr
    yrp, yip = zcp * hrp + zsp * hip, zcp * hip - zsp * hrp
    pe = jnp.dot(ge_ref[...], jnp.concatenate([yr + yrp, yi - yip], axis=0).astype(BF16),
                 preferred_element_type=F32)
    po = jnp.dot(go_ref[...], jnp.concatenate([yr - yrp, yi + yip], axis=0).astype(BF16),
                 preferred_element_type=F32)

    @pl.when(c == 0)
    def _():
        acc_s[0] = pe
        acc_s[1] = po

    @pl.when(c > 0)
    def _():
        acc_s[0] += pe
        acc_s[1] += po

    @pl.when((c == HY_CHUNKS - 1) & (o == 0))
    def _():
        for par, p1 in enumerate(split(short_conv(1))):
            z1 = p1 * (acc_s[par] + bias_ref[0:1, :] * zf_s[par])
            zf_s[par] = z1
            zb_s[par] = z1.astype(BF16)

    @pl.when((c == HY_CHUNKS - 1) & (o == 1))
    def _():
        halves = [p2 * (acc_s[par] + bias_ref[1:2, :] * zf_s[par]) for par, p2 in enumerate(split(short_conv(2)))]
        o_ref[...] = interleave(*halves).astype(o_ref.dtype)


def hyena_mixer(proj, conv_w, conv_b, hy_bias, hr, hi, tabs):
    b, s, n = proj.shape
    hw = HYENA_WIDTH
    half = s // 2
    kc = half // HY_CHUNKS
    nck = HY_CHUNKS
    spec = pl.BlockSpec((None, kc, hw), lambda bi, o, c: (o, c, 0))
    spec_p = pl.BlockSpec((None, kc, hw), lambda bi, o, c: (o, nck + c, 0))
    fwd_spec = pl.BlockSpec((None, 2 * kc, half), lambda bi, o, c: (c, 0, 0))
    inv_spec = pl.BlockSpec((None, half, 2 * kc), lambda bi, o, c: (c, 0, 0))
    fe, fo, ge, go = tabs
    return pl.pallas_call(
        functools.partial(_hyena_kernel, seq=s),
        out_shape=jax.ShapeDtypeStruct((b, s, hw), BF16),
        grid=(b, 2, nck),
        in_specs=[
            pl.BlockSpec((None, s, 3 * hw), lambda bi, o, c: (bi, 0, COL_HY // (3 * hw))),
            pl.BlockSpec((3, 3 * hw), lambda bi, o, c: (0, 0)),
            pl.BlockSpec((1, 3 * hw), lambda bi, o, c: (0, 0)),
            pl.BlockSpec((2, hw), lambda bi, o, c: (0, 0)),
            spec, spec, spec_p, spec_p, fwd_spec, fwd_spec, inv_spec, inv_spec,
        ],
        out_specs=pl.BlockSpec((None, s, hw), lambda bi, o, c: (bi, 0, 0)),
        scratch_shapes=[pltpu.VMEM((2, half, hw), BF16), pltpu.VMEM((2, half, hw), F32),
                        pltpu.VMEM((2, half, hw), F32), pltpu.VMEM((hw // 128, s, 128), F32)],
        compiler_params=_cparams(("parallel", "arbitrary", "arbitrary")),
        name="hyena_conv",
    )(proj, conv_w, conv_b.reshape(1, -1), hy_bias, hr, hi, hr, hi, fe, fo, ge, go)


def _layer_norm(y, g, b):
    mu = jnp.mean(y, axis=-1, keepdims=True)
    yc = y - mu
    var = jnp.mean(yc * yc, axis=-1, keepdims=True)
    return yc * lax.rsqrt(var + LN_EPS) * g + b


def _merge_kernel(*refs):
    att = refs[:12]
    (yb_ref, yc_ref, gl_ref, x_ref, g1_ref, sc2_ref, sh2_ref, lng_ref, lnb_ref, wa_ref, wb_ref, wc_ref, wo_ref,
     xo_ref, u2_ref, umax_ref) = refs[12:]
    d = D_MODEL
    halves = []
    for half in range(2):
        la, lb, lc = (att[4 * g + 2 + half][...] for g in range(3))
        m = jnp.maximum(jnp.maximum(la, lb), lc)
        ea, eb, ec = jnp.exp(la - m), jnp.exp(lb - m), jnp.exp(lc - m)
        inv = 1.0 / (ea + eb + ec)
        halves.append((ea * inv) * att[half][...] + (eb * inv) * att[4 + half][...] + (ec * inv) * att[8 + half][...])
    ya = jnp.concatenate(halves, axis=-1)
    za = jnp.dot(ya.astype(BF16), wa_ref[...], preferred_element_type=F32)
    zb = jnp.dot(yb_ref[...], wb_ref[...], preferred_element_type=F32)
    zc = jnp.dot(yc_ref[...], wc_ref[...], preferred_element_type=F32)
    merged = (jax.nn.sigmoid(gl_ref[:, 0:d].astype(F32)) * za
              + jax.nn.sigmoid(gl_ref[:, d:2 * d].astype(F32)) * zb
              + jax.nn.sigmoid(gl_ref[:, 2 * d:3 * d].astype(F32)) * zc)
    mix = jnp.dot(merged.astype(BF16), wo_ref[...], preferred_element_type=F32)
    xn = _layer_norm(ALPHA * x_ref[...] + g1_ref[...] * mix, lng_ref[...], lnb_ref[...])
    xo_ref[...] = xn
    u2 = (xn * (1.0 + sc2_ref[...]) + sh2_ref[...]).astype(BF16)
    u2_ref[...] = u2.astype(F32).T.astype(BF16)
    mag = jnp.abs(u2.astype(F32))
    m128 = functools.reduce(jnp.maximum, [mag[:, c0:c0 + 128] for c0 in range(0, d, 128)])
    umax_ref[...] = jnp.broadcast_to(jnp.max(m128, axis=0, keepdims=True), umax_ref.shape)


def merge_mixers(att, yb, yc, proj, x, g1, sc2, sh2, ln_g, ln_b, wa, wb, wc, wo):
    b, s, d = x.shape
    tm = 512
    row = lambda w: pl.BlockSpec((None, tm, w), lambda bi, i: (bi, i, 0))
    per_b = pl.BlockSpec((None, 1, d), lambda bi, i: (bi, 0, 0))
    const = lambda r, c: pl.BlockSpec((r, c), lambda bi, i: (0, 0))
    return pl.pallas_call(
        _merge_kernel,
        out_shape=(jax.ShapeDtypeStruct((b, s, d), F32), jax.ShapeDtypeStruct((d, b * s), BF16),
                   jax.ShapeDtypeStruct((b, s // tm, 8, 128), F32)),
        grid=(b, s // tm),
        in_specs=[row(128)] * 12 + [row(512), row(512), row(3 * d), row(d), per_b, per_b, per_b,
                                    const(1, d), const(1, d), const(256, d), const(512, d), const(512, d),
                                    const(d, d)],
        out_specs=(row(d), pl.BlockSpec((d, tm), lambda bi, i: (0, bi * (s // tm) + i)),
                   pl.BlockSpec((None, None, 8, 128), lambda bi, i: (bi, i, 0, 0))),
        compiler_params=_cparams(("parallel", "parallel")),
        name="merge_mixers",
    )(*att, yb, yc, proj, x, g1, sc2, sh2, ln_g.reshape(1, d), ln_b.reshape(1, d), wa, wb, wc, wo)


_PEER_PAIRS = [(i, j) for i in range(1, PEER_TOPK + 1) for j in range(1, PEER_TOPK + 1) if i * j <= PEER_TOPK]
PEER_NOT_TOP = 100.0


def _bitonic_clean_desc(v):
    v = list(v)
    dist = len(v) // 2
    while dist >= 1:
        for a in range(len(v)):
            b = a ^ dist
            if b > a:
                v[a], v[b] = jnp.maximum(v[a], v[b]), jnp.minimum(v[a], v[b])
        dist //= 2
    return v


def _bitonic_sort_desc(v):
    v = list(v)
    k = 2
    while k <= len(v):
        dist = k // 2
        while dist >= 1:
            for a in range(len(v)):
                b = a ^ dist
                if b > a:
                    hi, lo = jnp.maximum(v[a], v[b]), jnp.minimum(v[a], v[b])
                    v[a], v[b] = (hi, lo) if (a & k) == 0 else (lo, hi)
            dist //= 2
        k *= 2
    return v


def _merge_top(v, w):
    n = len(v)
    return _bitonic_clean_desc([jnp.maximum(v[a], w[n - 1 - a]) for a in range(n)])


def _sorted_prefix_count(vals, pred):
    w = jnp.where
    t8 = pred(vals[7])
    t4 = pred(w(t8, vals[11], vals[3]))
    t2 = pred(w(t8, w(t4, vals[13], vals[9]), w(t4, vals[5], vals[1])))
    lo = w(t4, w(t2, vals[6], vals[4]), w(t2, vals[2], vals[0]))
    hi = w(t4, w(t2, vals[14], vals[12]), w(t2, vals[10], vals[8]))
    t1 = pred(w(t8, hi, lo))
    t16 = pred(vals[15])
    count = w(t8, 8.0, 0.0) + w(t4, 4.0, 0.0) + w(t2, 2.0, 0.0) + w(t1, 1.0, 0.0) + w(t16, 1.0, 0.0)
    return count, t16


def _peer_route_kernel(u_ref, wq_ref, keys_ref, e1_ref, cut_ref, e2_ref, r2_ref, q_s, s_s, ab_s, st_s, *, tr):
    nh, nk, topk = PEER_HEADS, PEER_KEYS, PEER_TOPK
    qt = jnp.dot(wq_ref[...], u_ref[...], preferred_element_type=F32)
    q_s[...] = qt.astype(BF16)
    for ph in range(2 * nh):
        s_s[ph] = jnp.dot(keys_ref[ph], q_s[ph * 128:(ph + 1) * 128, :], preferred_element_type=F32)

    def extract(h, carry):
        for p in range(2):
            sv = s_s[p * nh + h]
            v = _bitonic_sort_desc([sv[g * 8:(g + 1) * 8, :] for g in range(topk)])
            for shift in (4, 2, 1):
                v = _merge_top(v, [pltpu.roll(x, shift, 0) for x in v])
            for r in range(topk):
                ab_s[p, r, pl.ds(h, 1), :] = v[r][0:1, :]
        return carry

    lax.fori_loop(0, nh, extract, 0)

    for ch in range(tr // 128):
        ln = slice(ch * 128, (ch + 1) * 128)
        a = [ab_s[0, r, :, ln] for r in range(topk)]
        b = [ab_s[1, r, :, ln] for r in range(topk)]
        cand = [a[i - 1] + b[j - 1] for (i, j) in _PEER_PAIRS]
        rest = cand[topk:] + [jnp.full((nh, 128), -jnp.inf, F32)] * (4 * topk - len(cand))
        groups = [cand[:topk]] + [_bitonic_sort_desc(rest[g * topk:(g + 1) * topk]) for g in range(3)]
        m01 = _merge_top(groups[0], groups[1])
        m23 = _merge_top(groups[2], groups[3])
        tau = functools.reduce(jnp.minimum, [jnp.maximum(m01[r], m23[topk - 1 - r]) for r in range(topk)])
        top = cand[0]
        z = jnp.zeros((nh, 128), F32)
        for cx in cand:
            z = z + jnp.where(cx >= tau, jnp.exp(cx - top), 0.0)
        st_s[0, :, ln] = tau
        st_s[1, :, ln] = 1.0 / z

    for h in range(nh):
        rows = slice(h * nk, (h + 1) * nk)
        s1 = s_s[h]
        s2 = s_s[nh + h]
        hrow = slice(h, h + 1)
        tau = st_s[0, hrow, :]
        in1 = s1 >= ab_s[0, topk - 1, hrow, :]
        bs = [ab_s[1, r, hrow, :] for r in range(topk)]
        cut, _ = _sorted_prefix_count(bs, lambda v: s1 + v >= tau)
        above, below_all = _sorted_prefix_count(bs, lambda v: v > s2)
        e1_ref[rows, :] = jnp.where(in1, jnp.exp(s1 - ab_s[0, 0, hrow, :]), 0.0)
        cut_ref[rows, :] = jnp.where(in1, cut, 0.0)
        e2_ref[rows, :] = jnp.where(below_all, 0.0, jnp.exp(s2 - bs[0]) * st_s[1, hrow, :]).astype(BF16)
        r2_ref[rows, :] = jnp.where(below_all, PEER_NOT_TOP, above + 1.0).astype(BF16)


def peer_route(u2, wq_t, keys):
    d, t = u2.shape
    tr = 512
    nrow = PEER_HEADS * PEER_KEYS
    tab = lambda dt: jax.ShapeDtypeStruct((nrow, t), dt)
    col = pl.BlockSpec((nrow, tr), lambda i: (0, i))
    return pl.pallas_call(
        functools.partial(_peer_route_kernel, tr=tr),
        out_shape=(tab(F32), tab(F32), tab(BF16), tab(BF16)),
        grid=(t // tr,),
        in_specs=[
            pl.BlockSpec((d, tr), lambda i: (0, i)),
            pl.BlockSpec(wq_t.shape, lambda i: (0, 0)),
            pl.BlockSpec(keys.shape, lambda i: (0, 0, 0)),
        ],
        out_specs=(col, col, col, col),
        scratch_shapes=[
            pltpu.VMEM((2 * nrow, tr), BF16),
            pltpu.VMEM((2 * PEER_HEADS, PEER_KEYS, tr), F32),
            pltpu.VMEM((2, PEER_TOPK, PEER_HEADS, tr), F32),
            pltpu.VMEM((2, PEER_HEADS, tr), F32),
        ],
        compiler_params=_cparams(("parallel",)),
        name="peer_route",
    )(u2, wq_t, keys)


F8 = jnp.float8_e4m3fn
F8_MAX = 448.0
WG_SCALE = 16.0


def _peer_dense_kernel(sc_ref, x_ref, u_ref, vt_ref, e1_ref, cut_ref, e2_ref, r2_ref, xres_ref, g2_ref, lng_ref,
                       lnb_ref, o_ref, acc_s, wga_s, wgb_s, x8_s, *, te, n_e):
    j = pl.program_id(1)
    nk = PEER_KEYS
    tn = x_ref.shape[1]
    sub = 16

    @pl.when(j == 0)
    def _():
        acc_s[...] = jnp.zeros_like(acc_s)
        wgb_s[...] = jnp.zeros_like(wgb_s)
        x8_s[...] = (x_ref[...].astype(F32) * sc_ref[2]).astype(F8)

    def step(w_cur, w_prev):
        inv_a = sc_ref[0]
        ts = 512
        c_lin = jnp.full((1, ts), 0.5 * WG_SCALE * inv_a, F32).astype(BF16)
        c_erf = jnp.full((1, ts), (2.0 ** -0.5) * inv_a, F32).astype(BF16)

        def row_bcast(ref, h, ii, lanes):
            chunks = [jnp.broadcast_to(ref[h, ii:ii + 1, c0:c0 + 128], (sub, 128)).astype(BF16)
                      for c0 in range(lanes.start, lanes.stop, 128)]
            return jnp.concatenate(chunks, axis=-1)[None]

        for sl in range(tn // ts):
            lanes = slice(sl * ts, (sl + 1) * ts)
            for ii in range(te // nk):
                rows = slice(ii * nk, (ii + 1) * nk)
                at = jnp.dot(u_ref[rows, :], x8_s[:, lanes], preferred_element_type=F32).astype(BF16)
                if ii == 1:
                    acc_s[:, lanes] += jnp.dot(vt_ref[...], w_prev[:, lanes], preferred_element_type=F32)
                g = (c_lin * at) * (1.0 + lax.erf(at * c_erf))
                w = None
                for h in range(PEER_HEADS):
                    cb = row_bcast(cut_ref, h, ii, lanes)
                    e1 = row_bcast(e1_ref, h, ii, lanes)
                    r2 = r2_ref[h * nk:(h + 1) * nk, lanes].reshape(nk // sub, sub, ts)
                    e2 = e2_ref[h * nk:(h + 1) * nk, lanes].reshape(nk // sub, sub, ts)
                    sel = jnp.where(r2 <= cb, e2, jnp.zeros((), BF16)) * e1
                    w = sel if w is None else w + sel
                wg = jnp.clip(g * w.reshape(nk, ts), -F8_MAX, F8_MAX)
                w_cur[rows, lanes] = wg.astype(F8)

    @pl.when((j % 2 == 0) & (j < n_e))
    def _():
        step(wga_s, wgb_s)

    @pl.when(j % 2 == 1)
    def _():
        step(wgb_s, wga_s)

    @pl.when(j == n_e)
    def _():
        acc = acc_s[...] + jnp.dot(vt_ref[...], wgb_s[...], preferred_element_type=F32)
        ffn = (acc * sc_ref[1]).T
        o_ref[...] = _layer_norm(ALPHA * xres_ref[...] + g2_ref[...] * ffn, lng_ref[...], lnb_ref[...])


def peer_dense(scales, u2, u_tab, vt_tab, e1, cut, e2, r2, x_res, g2, ln_g, ln_b):
    d, t = u2.shape
    b, s, _ = x_res.shape
    ne = u_tab.shape[0]
    tn, te = 1024, 1024
    n_e = ne // te
    assert n_e % 2 == 0, "the flush step reads the buffer written by the odd step n_e - 1"
    nrow = PEER_HEADS * PEER_KEYS
    col = pl.BlockSpec((nrow, tn), lambda i, j: (0, i))
    step_rows = pl.BlockSpec((PEER_HEADS, te // PEER_KEYS, tn), lambda i, j: (0, jnp.minimum(j, n_e - 1), i))
    e1 = e1.reshape(PEER_HEADS, PEER_KEYS, t)
    cut = cut.reshape(PEER_HEADS, PEER_KEYS, t)
    nts = s // tn
    row = pl.BlockSpec((None, tn, d), lambda i, j: (i // nts, i % nts, 0))

    def res_index(i, j):
        ii = jnp.where(j == 0, jnp.maximum(i - 1, 0), i)
        return ii // nts, ii % nts, 0

    const = pl.BlockSpec((1, d), lambda i, j: (0, 0))
    return pl.pallas_call(
        functools.partial(_peer_dense_kernel, te=te, n_e=n_e),
        out_shape=jax.ShapeDtypeStruct((b, s, d), F32),
        grid=(t // tn, n_e + 1),
        in_specs=[
            pl.BlockSpec(memory_space=pltpu.SMEM),
            pl.BlockSpec((d, tn), lambda i, j: (0, i)),
            pl.BlockSpec((te, d), lambda i, j: (jnp.minimum(j, n_e - 1), 0)),
            pl.BlockSpec((d, te), lambda i, j: (0, jnp.maximum(j - 1, 0))),
            step_rows, step_rows, col, col,
            pl.BlockSpec((None, tn, d), res_index), pl.BlockSpec((None, 1, d), lambda i, j: (i // nts, 0, 0)),
            const, const,
        ],
        out_specs=row,
        scratch_shapes=[pltpu.VMEM((d, tn), F32), pltpu.VMEM((te, tn), F8), pltpu.VMEM((te, tn), F8),
                        pltpu.VMEM((d, tn), F8)],
        compiler_params=_cparams(("parallel", "arbitrary")),
        name="peer_dense",
    )(scales, u2, u_tab, vt_tab, e1, cut, e2, r2, x_res, g2, ln_g.reshape(1, d), ln_b.reshape(1, d))


def rope_tables(seq):
    pos = jnp.arange(seq, dtype=F32)
    inv = ROPE_THETA ** (-jnp.arange(0, HEAD_DIM, 2, dtype=F32) / HEAD_DIM)
    ang = pos[:, None] * inv[None, :]
    sign = jnp.where((jnp.arange(128) % 64) < 32, -1.0, 1.0).astype(F32)
    return jnp.tile(jnp.cos(ang), (1, 4)), jnp.tile(jnp.sin(ang), (1, 4)) * sign[None, :]


def _swa_head_perm():
    heads = [h for p in range(4) for h in (p, 4 + p)]
    return jnp.concatenate([jnp.arange(h * HEAD_DIM, (h + 1) * HEAD_DIM) for h in heads])


def relayout_w_in(w):
    qa, ka, va = w[:, 0:768], w[:, 768:1536], w[:, 1536:2304]
    hy = w[:, 2304:3840]
    qc, kvc = w[:, 3840:4352], w[:, 4352:4608]
    gates = w[:, 4608:7680]
    dil = [jnp.concatenate([t[:, 256 * g:256 * (g + 1)] for t in (qa, ka, va)], axis=1) for g in range(3)]
    main = jnp.concatenate([gates, hy, dil[0], qc[:, _swa_head_perm()], kvc], axis=1).astype(BF16)
    return main, jnp.concatenate(dil[1:], axis=1).astype(BF16)


def residue_major(tab, dil):
    s = tab.shape[0]
    return tab.reshape(s // dil, dil, 128).transpose(1, 0, 2).reshape(s, 128)


def token_mixer(x, ada, w_main, w_dil, conv_w, conv_b, hy_bias, attn_sink, hr, hi, dft_tabs, rope_tabs,
                wa, wb, wc, wo, ln_g, ln_b):
    b, s, _ = x.shape
    sh1, sc1, g1, sh2, sc2, _ = [a[:, None, :] for a in jnp.split(ada, 6, axis=-1)]
    proj = in_proj(x, sc1, sh1, w_main)
    qkv1, qkv2 = in_proj_dil(x, sc1, sh1, w_dil)
    att = (dilated_attention(proj.reshape(b, 1, s, MAIN_WIDTH), COL_DIL0 // QKV_W, *rope_tabs[0])
           + dilated_attention(qkv1, 0, *rope_tabs[1]) + dilated_attention(qkv2, 0, *rope_tabs[2]))
    yb = hyena_mixer(proj, conv_w, conv_b, hy_bias, hr, hi, dft_tabs)
    yc = swa_attention(proj, *rope_tabs[0], attn_sink)
    return merge_mixers(att, yb, yc, proj, x, g1, sc2, sh2, ln_g, ln_b, wa, wb, wc, wo)


def peer_sublayer(u2, u2_max, wq, keys, u_tab, v_tab, x_res, g2, ln_g, ln_b):
    d = u2.shape[0]
    nh, nk = PEER_HEADS, PEER_KEYS
    wq_t = wq.reshape(d, nh, 2, nk).transpose(2, 1, 3, 0).reshape(2 * nh * nk, d).astype(BF16)
    keys_r = keys.transpose(1, 0, 2, 3).reshape(2 * nh, nk, keys.shape[-1]).astype(BF16)
    tabs = peer_route(u2, wq_t, keys_r)
    su, sv, sx = _pow2_scale(u_tab), _pow2_scale(v_tab), _pow2_scale(u2_max)
    scales = jnp.stack([1.0 / (su * sx), 1.0 / (sv * WG_SCALE), sx]).astype(F32)
    return peer_dense(scales, u2, (u_tab * su).astype(F8), (v_tab.T * sv).astype(F8), *tabs, x_res, g2, ln_g, ln_b)


def _pow2_scale(a):
    m = jnp.maximum(jnp.max(jnp.abs(a)).astype(F32), 2.0 ** -100)
    return jnp.exp2(jnp.floor(jnp.log2(F8_MAX / m)))


def kernel(x, c, w_ada, b_ada, w_in, conv_w, conv_b, hy_w1, hy_b1, hy_w2, hy_b2, hy_w3, hy_freq, hy_log_decay,
           hy_bias, attn_sink, w_branch_a, w_branch_b, w_branch_c, w_out, ln_g, ln_b, peer_wq, peer_keys, peer_u,
           peer_v):
    b, s, d = x.shape
    depth = w_in.shape[0]
    hw = HYENA_WIDTH
    cos_t, sin_t = rope_tables(s)
    dft_tabs = dft_tables_r2(s)
    ada = ada_ln(c, w_ada, b_ada)
    w3p = hy_w3.reshape(depth, -1, 2, 2, hw).transpose(0, 1, 3, 2, 4).reshape(depth, -1, 4 * hw)
    ldp = hy_log_decay.reshape(depth, 2, 2, hw).transpose(0, 2, 1, 3).reshape(depth, 4 * hw)
    hr, hi = hyena_filters(hy_w1, hy_b1, hy_w2, hy_b2, w3p, hy_freq, ldp, dft_tabs[0], dft_tabs[1], s)
    perm = _swa_head_perm()
    rope_tabs = [(residue_major(cos_t, dil), residue_major(sin_t, dil)) for _, dil in DIL_GROUPS]
    for l in range(depth):
        x1, u2, u2_max = token_mixer(x, ada[l], *relayout_w_in(w_in[l]), conv_w[l], conv_b[l], hy_bias[l], attn_sink[l],
                             hr[l], hi[l], dft_tabs, rope_tabs, w_branch_a[l].astype(BF16),
                             w_branch_b[l].astype(BF16), w_branch_c[l][perm].astype(BF16), w_out[l].astype(BF16),
                             ln_g[l, 0], ln_b[l, 0])
        x = peer_sublayer(u2, u2_max, peer_wq[l], peer_keys[l], peer_u[l], peer_v[l], x1, ada[l][:, None, 5 * d:6 * d],
                          ln_g[l, 1], ln_b[l, 1])
    return x
```
